```python
import math
import jax, jax.numpy as jnp
from jax import lax
import numpy as np

D_MODEL = 1024
BATCH = 2
SEQ = 8192
DEPTH = 4

GRID_W = 64
N_HEADS = 16
HEAD_DIM = D_MODEL // N_HEADS
MAX_WIN_ROWS = 8
WIN_COLS = 16
COL_QBLOCK = 16
COL_BAND = COL_QBLOCK + WIN_COLS
NEG_INF = -1e30
CONV_WIDTH = 3
PEER_HEADS = 8
PEER_N_KEYS = 128
PEER_N_EXPERTS = PEER_N_KEYS * PEER_N_KEYS
PEER_TOPK = 16
PEER_KEY_DIM = 256
PEER_HALF_DIM = PEER_KEY_DIM // 2
PEER_TOK_CHUNK = 128
N_MIXERS = 2
RMS_EPS = 1e-6

kernel_name = "hybrid_natten_shortconv_peer_encoder"


def rms_norm(x, gain):
    x32 = x.astype(jnp.float32)
    y = x32 * lax.rsqrt(jnp.mean(x32 * x32, axis=-1, keepdims=True) + RMS_EPS)
    return (y * gain.astype(jnp.float32)).astype(x.dtype)


def _col_window_tables():
    n_blk = GRID_W // COL_QBLOCK
    qcol = np.arange(GRID_W).reshape(n_blk, COL_QBLOCK)
    band_start = np.clip(np.arange(n_blk) * COL_QBLOCK - WIN_COLS // 2, 0, GRID_W - COL_BAND)
    band_cols = band_start[:, None] + np.arange(COL_BAND)
    win_start = np.clip(qcol - WIN_COLS // 2, 0, GRID_W - WIN_COLS)
    kc = band_cols[:, None, :]
    mask = (kc >= win_start[..., None]) & (kc < win_start[..., None] + WIN_COLS)
    col_off = np.clip(kc - qcol[..., None] + WIN_COLS - 1, 0, 2 * WIN_COLS - 2)
    return (jnp.asarray(band_cols, jnp.int32), jnp.asarray(mask),
            jnp.asarray(col_off, jnp.int32))


def neighbourhood_attention(h, w_qkv, w_o, rel_bias):
    b, s, d = h.shape
    rows = s // GRID_W
    win_rows = min(MAX_WIN_ROWS, rows)
    n_cblk = GRID_W // COL_QBLOCK
    band_cols, col_mask, col_off = _col_window_tables()
    qkv = (h @ w_qkv).reshape(b, rows, GRID_W, 3, N_HEADS, HEAD_DIM)
    qkv = qkv.transpose(3, 0, 4, 1, 2, 5)
    q = qkv[0] * (1.0 / math.sqrt(HEAD_DIM))
    k = qkv[1]
    v = qkv[2]

    def one_row(r):
        rs = jnp.clip(r - win_rows // 2, 0, rows - win_rows)
        q_r = lax.dynamic_index_in_dim(q, r, axis=2, keepdims=False)
        q_r = q_r.reshape(b, N_HEADS, n_cblk, COL_QBLOCK, HEAD_DIM)
        k_band = lax.dynamic_slice_in_dim(k, rs, win_rows, axis=2)[:, :, :, band_cols]
        v_band = lax.dynamic_slice_in_dim(v, rs, win_rows, axis=2)[:, :, :, band_cols]
        scores = jnp.einsum('bhjqd,bhrjkd->bhjqrk', q_r.astype(jnp.float32),
                            k_band.astype(jnp.float32))
        row_off = rs + jnp.arange(win_rows) - r + MAX_WIN_ROWS - 1
        bias = rel_bias[:, row_off[None, None, :, None], col_off[:, :, None, :]]
        scores = jnp.where(col_mask[:, :, None, :], scores + bias.astype(jnp.float32), NEG_INF)
        p = jax.nn.softmax(scores, axis=(-2, -1)).astype(v.dtype)
        o = jnp.einsum('bhjqrk,bhrjkd->bhjqd', p, v_band)
        return o.reshape(b, N_HEADS, GRID_W, HEAD_DIM)

    out = lax.map(one_row, jnp.arange(rows))
    out = out.transpose(1, 0, 3, 2, 4).reshape(b, s, d)
    return out @ w_o


def short_conv_mixer(h, w_in, w_conv, w_out):
    s = h.shape[1]
    gate_b, gate_c, val = jnp.split(h @ w_in, 3, axis=-1)
    u = gate_c * val
    pad = CONV_WIDTH // 2
    up = jnp.pad(u, ((0, 0), (pad, CONV_WIDTH - 1 - pad), (0, 0)))
    conv = sum(up[:, j:j + s] * w_conv[j] for j in range(CONV_WIDTH))
    return (gate_b * conv) @ w_out


def peer_mixer(h, w_query, sub_keys, w_up, w_down):
    b, s, d = h.shape
    xt = h.reshape(-1, PEER_TOK_CHUNK, d)
    keys32 = sub_keys.astype(jnp.float32)

    def chunk(xc):
        c = xc.shape[0]
        q = (xc @ w_query).reshape(c, PEER_HEADS, 2, PEER_HALF_DIM).astype(jnp.float32)
        sc = jnp.einsum('chpd,hpnd->chpn', q, keys32)
        sv, si = lax.top_k(sc, PEER_TOPK)
        cand = (sv[:, :, 0, :, None] + sv[:, :, 1, None, :]).reshape(c, PEER_HEADS, -1)
        cidx = (si[:, :, 0, :, None] * PEER_N_KEYS + si[:, :, 1, None, :]).reshape(c, PEER_HEADS, -1)
        best, pos = lax.top_k(cand, PEER_TOPK)
        eidx = jnp.take_along_axis(cidx, pos, axis=-1)
        g = jax.nn.softmax(best, axis=-1)
        u = w_up[eidx]
        a = jnp.einsum('cd,chkd->chk', xc, u).astype(jnp.float32)
        act = (jax.nn.gelu(a, approximate=False) * g).astype(xc.dtype)
        vv = w_down[eidx]
        return jnp.einsum('chk,chkd->cd', act, vv)

    return lax.map(chunk, xt).reshape(b, s, d)


def setup_inputs(seed: int = 0) -> dict:
    key = jax.random.key(seed)
    ks = jax.random.split(key, 16)
    n_attn = (DEPTH + 1) // 2
    n_conv = DEPTH // 2
    d = D_MODEL
    nrm = jax.random.normal
    f32 = jnp.float32
    return {
        "x": nrm(ks[0], (BATCH, SEQ, d), f32),
        "norm_mix": 1.0 + 0.02 * nrm(ks[1], (DEPTH, d), f32),
        "norm_ffn": 1.0 + 0.02 * nrm(ks[2], (DEPTH, d), f32),
        "norm_final": 1.0 + 0.02 * nrm(ks[3], (d,), f32),
        "attn_w_qkv": nrm(ks[4], (n_attn, d, 3 * d), f32) * d ** -0.5,
        "attn_w_o": nrm(ks[5], (n_attn, d, d), f32) * d ** -0.5,
        "attn_rel_bias": 0.1 * nrm(ks[6], (n_attn, N_HEADS, 2 * MAX_WIN_ROWS - 1, 2 * WIN_COLS - 1), f32),
        "conv_w_in": nrm(ks[7], (n_conv, d, 3 * d), f32) * d ** -0.5,
        "conv_w_conv": nrm(ks[8], (n_conv, CONV_WIDTH, d), f32) * CONV_WIDTH ** -0.5,
        "conv_w_out": nrm(ks[9], (n_conv, d, d), f32) * d ** -0.5,
        "peer_w_query": nrm(ks[10], (DEPTH, d, PEER_HEADS * PEER_KEY_DIM), f32) * d ** -0.5,
        "peer_sub_keys": nrm(ks[11], (DEPTH, PEER_HEADS, 2, PEER_N_KEYS, PEER_HALF_DIM), f32) * PEER_HALF_DIM ** -0.5,
        "peer_w_up": nrm(ks[12], (DEPTH, PEER_N_EXPERTS, d), f32) * d ** -0.5,
        "peer_w_down": nrm(ks[13], (DEPTH, PEER_N_EXPERTS, d), f32) * (PEER_HEADS * PEER_TOPK) ** -0.5,
    }


def reference(x, norm_mix, norm_ffn, norm_final, attn_w_qkv, attn_w_o, attn_rel_bias,
              conv_w_in, conv_w_conv, conv_w_out, peer_w_query, peer_sub_keys,
              peer_w_up, peer_w_down):
    h = x
    for i in range(DEPTH):
        j = i // N_MIXERS
        hn = rms_norm(h, norm_mix[i])
        if i % N_MIXERS == 0:
            h = h + neighbourhood_attention(hn, attn_w_qkv[j], attn_w_o[j], attn_rel_bias[j])
        else:
            h = h + short_conv_mixer(hn, conv_w_in[j], conv_w_conv[j], conv_w_out[j])
        hn = rms_norm(h, norm_ffn[i])
        h = h + peer_mixer(hn, peer_w_query[i], peer_sub_keys[i], peer_w_up[i], peer_w_down[i])
    return rms_norm(h, norm_final)
```

```python
import functools
import math

import numpy as np
import jax
import jax.numpy as jnp
from jax import lax
from jax.experimental import pallas as pl
from jax.experimental.pallas import tpu as pltpu

F32 = jnp.float32
BF16 = jnp.bfloat16

RMS_EPS = 1e-6
NEG_INF = -1e30

GRID_W = 64
N_HEADS = 16
HEAD_DIM = 64
MAX_WIN_ROWS = 8
WIN_COLS = 16
ATT_Q_ROWS = 4
ATT_BAND_ROWS = ATT_Q_ROWS + MAX_WIN_ROWS - 1
ATT_LANES = 128

PEER_HEADS = 8
PEER_N_KEYS = 128
PEER_TOPK = 16
PEER_HALF_DIM = 128

TOK_TILE = 512
PEER_PRE_TILE = 256
PEER_TOK_TILE = 512
PEER_EXP_TILE = 512
VMEM_LIMIT = 56 * 1024 * 1024


def _cparams(sem):
    return pltpu.CompilerParams(dimension_semantics=sem, vmem_limit_bytes=VMEM_LIMIT)


def _rms(x, g):
    return x * lax.rsqrt(jnp.mean(x * x, axis=-1, keepdims=True) + RMS_EPS) * g


def _norm_matmul_kernel(x_ref, g_ref, w_ref, o_ref):
    y = _rms(x_ref[...], g_ref[...])
    o_ref[...] = jnp.dot(y.astype(BF16), w_ref[...],
                         preferred_element_type=F32).astype(o_ref.dtype)


def norm_matmul(x, gain, w, out_dtype):
    t, d = x.shape
    n = w.shape[1]
    return pl.pallas_call(
        _norm_matmul_kernel,
        grid=(t // TOK_TILE,),
        in_specs=[pl.BlockSpec((TOK_TILE, d), lambda i: (i, 0)),
                  pl.BlockSpec((1, d), lambda i: (0, 0)),
                  pl.BlockSpec((d, n), lambda i: (0, 0))],
        out_specs=pl.BlockSpec((TOK_TILE, n), lambda i: (i, 0)),
        out_shape=jax.ShapeDtypeStruct((t, n), out_dtype),
        compiler_params=_cparams(("parallel",)),
        name="norm_matmul",
    )(x, gain.reshape(1, d), w)


def _matmul_residual_kernel(a_ref, w_ref, h_ref, o_ref):
    o_ref[...] = h_ref[...] + jnp.dot(a_ref[...], w_ref[...], preferred_element_type=F32)


def matmul_residual(a, w, h):
    t, k = a.shape
    d = w.shape[1]
    return pl.pallas_call(
        _matmul_residual_kernel,
        grid=(t // TOK_TILE,),
        in_specs=[pl.BlockSpec((TOK_TILE, k), lambda i: (i, 0)),
                  pl.BlockSpec((k, d), lambda i: (0, 0)),
                  pl.BlockSpec((TOK_TILE, d), lambda i: (i, 0))],
        out_specs=pl.BlockSpec((TOK_TILE, d), lambda i: (i, 0)),
        out_shape=jax.ShapeDtypeStruct((t, d), F32),
        compiler_params=_cparams(("parallel",)),
        name="matmul_residual",
    )(a, w, h)


def _norm_transpose_kernel(x_ref, g_ref, o_ref):
    o_ref[...] = _rms(x_ref[...], g_ref[...]).T.astype(o_ref.dtype)


def norm_transpose(x, gain):
    t, d = x.shape
    return pl.pallas_call(
        _norm_transpose_kernel,
        grid=(t // TOK_TILE,),
        in_specs=[pl.BlockSpec((TOK_TILE, d), lambda i: (i, 0)),
                  pl.BlockSpec((1, d), lambda i: (0, 0))],
        out_specs=pl.BlockSpec((d, TOK_TILE), lambda i: (0, i)),
        out_shape=jax.ShapeDtypeStruct((d, t), BF16),
        compiler_params=_cparams(("parallel",)),
        name="norm_transpose",
    )(x, gain.reshape(1, d))


def _final_norm_kernel(x_ref, g_ref, o_ref):
    o_ref[...] = _rms(x_ref[...], g_ref[...])


def final_norm(x, gain):
    t, d = x.shape
    return pl.pallas_call(
        _final_norm_kernel,
        grid=(t // TOK_TILE,),
        in_specs=[pl.BlockSpec((TOK_TILE, d), lambda i: (i, 0)),
                  pl.BlockSpec((1, d), lambda i: (0, 0))],
        out_specs=pl.BlockSpec((TOK_TILE, d), lambda i: (i, 0)),
        out_shape=jax.ShapeDtypeStruct((t, d), F32),
        compiler_params=_cparams(("parallel",)),
        name="final_norm",
    )(x, gain.reshape(1, d))


def _attention_bias_tables(rel_bias, rows):
    tabs = []
    for r0 in (0, ATT_Q_ROWS, rows - ATT_Q_ROWS):
        band = int(np.clip(r0 - MAX_WIN_ROWS // 2, 0, rows - ATT_BAND_ROWS))
        qr = r0 + np.arange(ATT_Q_ROWS)[:, None, None, None]
        qc = np.arange(GRID_W)[None, :, None, None]
        kr = band + np.arange(ATT_BAND_ROWS)[None, None, :, None]
        kc = np.arange(GRID_W)[None, None, None, :]
        rs = np.clip(qr - MAX_WIN_ROWS // 2, 0, rows - MAX_WIN_ROWS)
        ws = np.clip(qc - WIN_COLS // 2, 0, GRID_W - WIN_COLS)
        valid = (kr >= rs) & (kr < rs + MAX_WIN_ROWS) & (kc >= ws) & (kc < ws + WIN_COLS)
        row_off = np.clip(kr - qr + MAX_WIN_ROWS - 1, 0, 2 * MAX_WIN_ROWS - 2)
        col_off = np.clip(kc - qc + WIN_COLS - 1, 0, 2 * WIN_COLS - 2)
        shape = (ATT_Q_ROWS, GRID_W, ATT_BAND_ROWS, GRID_W)
        valid = np.broadcast_to(valid, shape).reshape(ATT_Q_ROWS * GRID_W, ATT_BAND_ROWS * GRID_W)
        row_off = np.broadcast_to(row_off, shape).reshape(valid.shape)
        col_off = np.broadcast_to(col_off, shape).reshape(valid.shape)
        bias = rel_bias[:, row_off, col_off]
        tabs.append(jnp.where(valid[None], bias, NEG_INF))
    return jnp.stack(tabs)


def _attention_kernel(q_ref, k_ref, v_ref, b_ref, o_ref, *, rows):
    rb = pl.program_id(2)
    band = jnp.clip(rb * ATT_Q_ROWS - MAX_WIN_ROWS // 2, 0, rows - ATT_BAND_ROWS)
    start = pl.multiple_of(band * GRID_W, GRID_W)
    n_keys = ATT_BAND_ROWS * GRID_W
    kb = k_ref[pl.ds(start, n_keys), :]
    vb = v_ref[pl.ds(start, n_keys), :]
    q = q_ref[...]
    lane = lax.broadcasted_iota(jnp.int32, q.shape, 1)
    scale = 1.0 / math.sqrt(HEAD_DIM)
    outs = []
    for hh in range(ATT_LANES // HEAD_DIM):
        in_head = (lane < HEAD_DIM) if hh == 0 else (lane >= HEAD_DIM)
        qm = jnp.where(in_head, q, jnp.zeros_like(q)) * jnp.asarray(scale, q.dtype)
        s = lax.dot_general(qm, kb, (((1,), (1,)), ((), ())), preferred_element_type=F32)
        s = s + b_ref[0, hh]
        m = jnp.max(s, axis=-1, keepdims=True)
        p = jnp.exp(s - m)
        l = jnp.sum(p, axis=-1, keepdims=True)
        o = jnp.dot(p.astype(BF16), vb, preferred_element_type=F32)
        outs.append(o / l)
    o_ref[...] = jnp.where(lane < HEAD_DIM, outs[0], outs[1]).astype(o_ref.dtype)


def neighbourhood_attention_core(qkv, bias_tabs, batch, seq):
    t = qkv.shape[0]
    d = qkv.shape[1] // 3
    rows = seq // GRID_W
    n_rb = rows // ATT_Q_ROWS
    n_hp = d // ATT_LANES
    q_tok = ATT_Q_ROWS * GRID_W

    def bias_map(b, hp, rb):
        return (jnp.where(rb == 0, 0, jnp.where(rb == n_rb - 1, 2, 1)), hp, 0, 0)

    return pl.pallas_call(
        functools.partial(_attention_kernel, rows=rows),
        grid=(batch, n_hp, n_rb),
        in_specs=[pl.BlockSpec((q_tok, ATT_LANES), lambda b, hp, rb: (b * n_rb + rb, hp)),
                  pl.BlockSpec((seq, ATT_LANES), lambda b, hp, rb: (b, n_hp + hp)),
                  pl.BlockSpec((seq, ATT_LANES), lambda b, hp, rb: (b, 2 * n_hp + hp)),
                  pl.BlockSpec((1, 2, q_tok, ATT_BAND_ROWS * GRID_W), bias_map)],
        out_specs=pl.BlockSpec((q_tok, ATT_LANES), lambda b, hp, rb: (b * n_rb + rb, hp)),
        out_shape=jax.ShapeDtypeStruct((t, d), BF16),
        compiler_params=_cparams(("parallel", "parallel", "arbitrary")),
        name="natten_core",
    )(qkv, qkv, qkv, bias_tabs)


def _short_conv_kernel(b_ref, c_ref, v_ref, cp_ref, vp_ref, cn_ref, vn_ref, w_ref, o_ref, *, seq):
    i = pl.program_id(0)
    tm = b_ref.shape[0]
    u = c_ref[...] * v_ref[...]
    u_before = cp_ref[7:8, :] * vp_ref[7:8, :]
    u_after = cn_ref[0:1, :] * vn_ref[0:1, :]
    row = lax.broadcasted_iota(jnp.int32, u.shape, 0)
    pos = (i * tm + row) % seq
    u_prev = jnp.where(row == 0, u_before, pltpu.roll(u, 1, 0))
    u_prev = jnp.where(pos == 0, 0.0, u_prev)
    u_next = jnp.where(row == tm - 1, u_after, pltpu.roll(u, tm - 1, 0))
    u_next = jnp.where(pos == seq - 1, 0.0, u_next)
    conv = u_prev * w_ref[0:1, :] + u * w_ref[1:2, :] + u_next * w_ref[2:3, :]
    o_ref[...] = (b_ref[...] * conv).astype(o_ref.dtype)


def short_conv_gate(y, w_conv, seq):
    t = y.shape[0]
    d = y.shape[1] // 3
    cw = 512
    nj = d // cw
    tm = TOK_TILE
    n8 = t // 8

    def prev_map(off):
        return lambda i, j: (jnp.maximum(i * (tm // 8) - 1, 0), off * nj + j)

    def next_map(off):
        return lambda i, j: (jnp.minimum((i + 1) * (tm // 8), n8 - 1), off * nj + j)

    return pl.pallas_call(
        functools.partial(_short_conv_kernel, seq=seq),
        grid=(t // tm, nj),
        in_specs=[pl.BlockSpec((tm, cw), lambda i, j: (i, j)),
                  pl.BlockSpec((tm, cw), lambda i, j: (i, nj + j)),
                  pl.BlockSpec((tm, cw), lambda i, j: (i, 2 * nj + j)),
                  pl.BlockSpec((8, cw), prev_map(1)),
                  pl.BlockSpec((8, cw), prev_map(2)),
                  pl.BlockSpec((8, cw), next_map(1)),
                  pl.BlockSpec((8, cw), next_map(2)),
                  pl.BlockSpec((3, cw), lambda i, j: (0, j))],
        out_specs=pl.BlockSpec((tm, cw), lambda i, j: (i, j)),
        out_shape=jax.ShapeDtypeStruct((t, d), BF16),
        compiler_params=_cparams(("parallel", "parallel")),
        name="short_conv_gate",
    )(y, y, y, y, y, y, y, w_conv)


def _top16(sc, want_rank):
    n, l = sc.shape
    iota_n = lax.broadcasted_iota(jnp.int32, (n, l), 0).astype(F32)
    iota_k = lax.broadcasted_iota(jnp.int32, (PEER_TOPK, l), 0).astype(F32)

    def body(r, carry):
        sc, rank, sv, si = carry
        rf = r.astype(F32)
        m = jnp.max(sc, axis=0, keepdims=True)
        idx = jnp.min(jnp.where(sc == m, iota_n, float(n)), axis=0, keepdims=True)
        sel = iota_n == idx
        if want_rank:
            rank = jnp.where(sel, rf, rank)
        sc = jnp.where(sel, -jnp.inf, sc)
        sv = jnp.where(iota_k == rf, m, sv)
        si = jnp.where(iota_k == rf, idx, si)
        return sc, rank, sv, si

    init = (sc, jnp.full((n, l), float(PEER_TOPK), F32),
            jnp.zeros((PEER_TOPK, l), F32), jnp.zeros((PEER_TOPK, l), F32))
    _, rank, sv, si = lax.fori_loop(0, PEER_TOPK, body, init)
    return sv, si, rank


def _merge_top16(sv1, sv2):
    k, l = sv1.shape
    iota_k = lax.broadcasted_iota(jnp.int32, (k, l), 0).astype(F32)
    top = sv1[0:1, :] + sv2[0:1, :]

    def body(_, carry):
        front, cnt, z = carry
        m = jnp.max(front, axis=0, keepdims=True)
        a_star = jnp.min(jnp.where(front == m, iota_k, float(k)), axis=0, keepdims=True)
        sel = iota_k == a_star
        z = z + jnp.exp(m - top)
        cnt = jnp.where(sel, cnt + 1.0, cnt)
        c_sel = jnp.sum(jnp.where(sel, cnt, 0.0), axis=0, keepdims=True)
        nxt = jnp.sum(jnp.where(iota_k == c_sel, sv2, 0.0), axis=0, keepdims=True)
        nxt = jnp.where(c_sel >= float(k), -jnp.inf, nxt)
        front = jnp.where(sel, sv1 + nxt, front)
        return front, cnt, z

    init = (sv1 + sv2[0:1, :], jnp.zeros((k, l), F32), jnp.zeros((1, l), F32))
    _, cnt, z = lax.fori_loop(0, PEER_TOPK, body, init)
    return cnt, z


def _peer_weights_kernel(xt_ref, wq_ref, keys_ref, r2_ref, e2_ref, n_ref, e1_ref, q_scr):
    q_scr[...] = jnp.dot(wq_ref[...], xt_ref[...], preferred_element_type=F32).astype(BF16)
    n_keys, l = PEER_N_KEYS, xt_ref.shape[1]
    iota_n = lax.broadcasted_iota(jnp.int32, (n_keys, l), 0).astype(F32)

    def head(h, _):
        r1 = pl.multiple_of(h * 2 * PEER_HALF_DIM, PEER_HALF_DIM)
        r2 = pl.multiple_of(r1 + PEER_HALF_DIM, PEER_HALF_DIM)
        s1 = jnp.dot(keys_ref[2 * h], q_scr[pl.ds(r1, PEER_HALF_DIM), :], preferred_element_type=F32)
        s2 = jnp.dot(keys_ref[2 * h + 1], q_scr[pl.ds(r2, PEER_HALF_DIM), :], preferred_element_type=F32)
        sv1, si1, _ = _top16(s1, want_rank=False)
        sv2, _, rank2 = _top16(s2, want_rank=True)
        cnt, z = _merge_top16(sv1, sv2)
        n_sel = jnp.zeros((n_keys, l), F32)
        for a in range(PEER_TOPK):
            n_sel = jnp.where(iota_n == si1[a:a + 1, :], cnt[a:a + 1, :], n_sel)
        r2_ref[h] = rank2
        e2_ref[h] = jnp.exp(s2 - sv2[0:1, :])
        n_ref[h] = n_sel
        e1_ref[h] = jnp.exp(s1 - sv1[0:1, :]) / z
        return 0

    lax.fori_loop(0, PEER_HEADS, head, 0)


def peer_weights(xt, wq_t, keys):
    d, t = xt.shape
    tm = PEER_PRE_TILE
    out = jax.ShapeDtypeStruct((PEER_HEADS, PEER_N_KEYS, t), F32)
    ospec = pl.BlockSpec((PEER_HEADS, PEER_N_KEYS, tm), lambda i: (0, 0, i))
    return pl.pallas_call(
        _peer_weights_kernel,
        grid=(t // tm,),
        in_specs=[pl.BlockSpec((d, tm), lambda i: (0, i)),
                  pl.BlockSpec(wq_t.shape, lambda i: (0, 0)),
                  pl.BlockSpec(keys.shape, lambda i: (0, 0, 0))],
        out_specs=[ospec, ospec, ospec, ospec],
        out_shape=[out, out, out, out],
        scratch_shapes=[pltpu.VMEM((wq_t.shape[0], tm), BF16)],
        compiler_params=_cparams(("parallel",)),
        name="peer_weights",
    )(xt, wq_t, keys)


def _peer_dense_kernel(xt_ref, r2_ref, e2_ref, n_ref, e1_ref, wup_ref, wdn_ref, h_ref, o_ref,
                       acc_ref, a_ref, hid_ref):
    e = pl.program_id(1)
    rows_per_step = wup_ref.shape[0] // PEER_N_KEYS

    @pl.when(e == 0)
    def _():
        acc_ref[...] = jnp.zeros_like(acc_ref)

    a_ref[...] = jnp.dot(wup_ref[...], xt_ref[...], preferred_element_type=F32)

    def key_row(il, _):
        i = e * rows_per_step + il
        g = jnp.zeros((PEER_N_KEYS, xt_ref.shape[1]), F32)
        for h in range(PEER_HEADS):
            n_row = n_ref[h, pl.ds(i, 1), :]
            e1_row = e1_ref[h, pl.ds(i, 1), :]
            g = g + jnp.where(r2_ref[h] < n_row, e2_ref[h] * e1_row, 0.0)
        r0 = pl.multiple_of(il * PEER_N_KEYS, PEER_N_KEYS)
        a = a_ref[pl.ds(r0, PEER_N_KEYS), :]
        act = 0.5 * a * (1.0 + lax.erf(a * math.sqrt(0.5)))
        hid_ref[pl.ds(r0, PEER_N_KEYS), :] = (act * g).astype(hid_ref.dtype)
        return 0

    lax.fori_loop(0, rows_per_step, key_row, 0)
    acc_ref[...] += lax.dot_general(hid_ref[...], wdn_ref[...], (((0,), (0,)), ((), ())),
                                    preferred_element_type=F32)

    @pl.when(e == pl.num_programs(1) - 1)
    def _():
        o_ref[...] = h_ref[...] + acc_ref[...]


def peer_dense(xt, r2, e2, n_sel, e1n, w_up, w_down, h):
    d, t = xt.shape
    n_exp = w_up.shape[0]
    tm, te = PEER_TOK_TILE, PEER_EXP_TILE
    wspec = pl.BlockSpec((PEER_HEADS, PEER_N_KEYS, tm), lambda i, e: (0, 0, i))
    return pl.pallas_call(
        _peer_dense_kernel,
        grid=(t // tm, n_exp // te),
        in_specs=[pl.BlockSpec((d, tm), lambda i, e: (0, i)),
                  wspec, wspec, wspec, wspec,
                  pl.BlockSpec((te, d), lambda i, e: (e, 0)),
                  pl.BlockSpec((te, d), lambda i, e: (e, 0)),
                  pl.BlockSpec((tm, d), lambda i, e: (i, 0))],
        out_specs=pl.BlockSpec((tm, d), lambda i, e: (i, 0)),
        out_shape=jax.ShapeDtypeStruct((t, d), F32),
        scratch_shapes=[pltpu.VMEM((tm, d), F32),
                        pltpu.VMEM((te, tm), F32),
                        pltpu.VMEM((te, tm), BF16)],
        compiler_params=_cparams(("parallel", "arbitrary")),
        name="peer_dense",
    )(xt, r2, e2, n_sel, e1n, w_up, w_down, h)


def kernel(x, norm_mix, norm_ffn, norm_final, attn_w_qkv, attn_w_o, attn_rel_bias,
           conv_w_in, conv_w_conv, conv_w_out, peer_w_query, peer_sub_keys,
           peer_w_up, peer_w_down):
    batch, seq, d = x.shape
    depth = norm_mix.shape[0]
    rows = seq // GRID_W
    h = x.reshape(batch * seq, d)
    for i in range(depth):
        j = i // 2
        if i % 2 == 0:
            qkv = norm_matmul(h, norm_mix[i], attn_w_qkv[j].astype(BF16), BF16)
            bias_tabs = _attention_bias_tables(attn_rel_bias[j], rows)
            mixed = neighbourhood_attention_core(qkv, bias_tabs, batch, seq)
            h = matmul_residual(mixed, attn_w_o[j].astype(BF16), h)
        else:
            y = norm_matmul(h, norm_mix[i], conv_w_in[j].astype(BF16), F32)
            mixed = short_conv_gate(y, conv_w_conv[j], seq)
            h = matmul_residual(mixed, conv_w_out[j].astype(BF16), h)
        xt = norm_transpose(h, norm_ffn[i])
        keys = peer_sub_keys[i].reshape(PEER_HEADS * 2, PEER_N_KEYS, PEER_HALF_DIM).astype(BF16)
        r2, e2, n_sel, e1n = peer_weights(xt, peer_w_query[i].T.astype(BF16), keys)
        h = peer_dense(xt, r2, e2, n_sel, e1n, peer_w_up[i].astype(BF16),
                       peer_w_down[i].astype(BF16), h)
    return final_norm(h, norm_final).reshape(batch, seq, d)
```

```python
import functools
import math

import numpy as np
import jax
import jax.numpy as jnp
from jax import lax
from jax.experimental import pallas as pl
from jax.experimental.pallas import tpu as pltpu

F32 = jnp.float32
BF16 = jnp.bfloat16

RMS_EPS = 1e-6
NEG_INF = -1e30

GRID_W = 64
N_HEADS = 16
HEAD_DIM = 64
MAX_WIN_ROWS = 8
WIN_COLS = 16
ATT_Q_ROWS = 4
ATT_BAND_ROWS = ATT_Q_ROWS + MAX_WIN_ROWS - 1
ATT_LANES = 128

PEER_HEADS = 8
PEER_N_KEYS = 128
PEER_TOPK = 16
PEER_HALF_DIM = 128

TOK_TILE = 512
PEER_PRE_TILE = 256
PEER_TOK_TILE = 512
PEER_EXP_TILE = 2048
PEER_STAGE_ROWS = 256
VMEM_LIMIT = 56 * 1024 * 1024


def _cparams(sem, flags=None):
    return pltpu.CompilerParams(dimension_semantics=sem, vmem_limit_bytes=VMEM_LIMIT, flags=flags)


def _rms(x, g):
    return x * lax.rsqrt(jnp.mean(x * x, axis=-1, keepdims=True) + RMS_EPS) * g


def _norm_matmul_kernel(x_ref, g_ref, w_ref, o_ref):
    y = _rms(x_ref[...], g_ref[...])
    o_ref[...] = jnp.dot(y.astype(BF16), w_ref[...],
                         preferred_element_type=F32).astype(o_ref.dtype)


def norm_matmul(x, gain, w, out_dtype):
    t, d = x.shape
    n = w.shape[1]
    return pl.pallas_call(
        _norm_matmul_kernel,
        grid=(t // TOK_TILE,),
        in_specs=[pl.BlockSpec((TOK_TILE, d), lambda i: (i, 0)),
                  pl.BlockSpec((1, d), lambda i: (0, 0)),
                  pl.BlockSpec((d, n), lambda i: (0, 0))],
        out_specs=pl.BlockSpec((TOK_TILE, n), lambda i: (i, 0)),
        out_shape=jax.ShapeDtypeStruct((t, n), out_dtype),
        compiler_params=_cparams(("parallel",)),
        name="norm_matmul",
    )(x, gain.reshape(1, d), w)


def _matmul_residual_kernel(a_ref, w_ref, h_ref, o_ref):
    o_ref[...] = h_ref[...] + jnp.dot(a_ref[...], w_ref[...], preferred_element_type=F32)


def matmul_residual(a, w, h):
    t, k = a.shape
    d = w.shape[1]
    return pl.pallas_call(
        _matmul_residual_kernel,
        grid=(t // TOK_TILE,),
        in_specs=[pl.BlockSpec((TOK_TILE, k), lambda i: (i, 0)),
                  pl.BlockSpec((k, d), lambda i: (0, 0)),
                  pl.BlockSpec((TOK_TILE, d), lambda i: (i, 0))],
        out_specs=pl.BlockSpec((TOK_TILE, d), lambda i: (i, 0)),
        out_shape=jax.ShapeDtypeStruct((t, d), F32),
        compiler_params=_cparams(("parallel",)),
        name="matmul_residual",
    )(a, w, h)


def _norm_transpose_kernel(x_ref, g_ref, o_ref):
    o_ref[...] = _rms(x_ref[...], g_ref[...]).T.astype(o_ref.dtype)


def norm_transpose(x, gain):
    t, d = x.shape
    return pl.pallas_call(
        _norm_transpose_kernel,
        grid=(t // TOK_TILE,),
        in_specs=[pl.BlockSpec((TOK_TILE, d), lambda i: (i, 0)),
                  pl.BlockSpec((1, d), lambda i: (0, 0))],
        out_specs=pl.BlockSpec((d, TOK_TILE), lambda i: (0, i)),
        out_shape=jax.ShapeDtypeStruct((d, t), BF16),
        compiler_params=_cparams(("parallel",)),
        name="norm_transpose",
    )(x, gain.reshape(1, d))


def _final_norm_kernel(x_ref, g_ref, o_ref):
    o_ref[...] = _rms(x_ref[...], g_ref[...])


def final_norm(x, gain):
    t, d = x.shape
    return pl.pallas_call(
        _final_norm_kernel,
        grid=(t // TOK_TILE,),
        in_specs=[pl.BlockSpec((TOK_TILE, d), lambda i: (i, 0)),
                  pl.BlockSpec((1, d), lambda i: (0, 0))],
        out_specs=pl.BlockSpec((TOK_TILE, d), lambda i: (i, 0)),
        out_shape=jax.ShapeDtypeStruct((t, d), F32),
        compiler_params=_cparams(("parallel",)),
        name="final_norm",
    )(x, gain.reshape(1, d))


def _attention_bias_tables(rel_bias, rows):
    n_h = rel_bias.shape[0]
    n_dr = 2 * MAX_WIN_ROWS - 1
    w = GRID_W
    edge = w - WIN_COLS
    line = jnp.pad(rel_bias, ((0, 0), (0, 0), (edge, edge + 1)), mode="edge")
    flat = jnp.tile(line, (1, 1, w))[:, :, w - 1:w - 1 + w * (2 * w - 1)]
    toep = flat.reshape(n_h, n_dr, w, 2 * w - 1)[:, :, :, :w]
    qc = np.arange(w)[:, None]
    kc = np.arange(w)[None, :]
    ws = np.clip(qc - WIN_COLS // 2, 0, w - WIN_COLS)
    toep = jnp.where(((kc >= ws) & (kc < ws + WIN_COLS))[None, None], toep, NEG_INF)
    pad = ATT_Q_ROWS - 1
    toep = jnp.pad(toep, ((0, 0), (pad, pad), (0, 0), (0, 0)), constant_values=NEG_INF)
    tabs = []
    for r0 in (0, ATT_Q_ROWS, rows - ATT_Q_ROWS):
        band = int(np.clip(r0 - MAX_WIN_ROWS // 2, 0, rows - ATT_BAND_ROWS))
        blocks = []
        for ql in range(ATT_Q_ROWS):
            lo = band - (r0 + ql) + MAX_WIN_ROWS - 1 + pad
            blk = toep[:, lo:lo + ATT_BAND_ROWS]
            blocks.append(blk.transpose(0, 2, 1, 3).reshape(n_h, w, ATT_BAND_ROWS * w))
        tab = jnp.stack(blocks, axis=1).reshape(n_h, ATT_Q_ROWS * w, ATT_BAND_ROWS * w)
        qr = r0 + np.arange(ATT_Q_ROWS)[:, None]
        kr = band + np.arange(ATT_BAND_ROWS)[None, :]
        rs = np.clip(qr - MAX_WIN_ROWS // 2, 0, rows - MAX_WIN_ROWS)
        row_ok = (kr >= rs) & (kr < rs + MAX_WIN_ROWS)
        row_ok = np.repeat(np.repeat(row_ok, w, axis=0), w, axis=1)
        tabs.append(jnp.where(row_ok[None], tab, NEG_INF))
    return jnp.stack(tabs)


def _attention_kernel(q_ref, k_ref, v_ref, b_ref, o_ref, *, rows):
    rb = pl.program_id(2)
    band = jnp.clip(rb * ATT_Q_ROWS - MAX_WIN_ROWS // 2, 0, rows - ATT_BAND_ROWS)
    start = pl.multiple_of(band * GRID_W, GRID_W)
    n_keys = ATT_BAND_ROWS * GRID_W
    kb = k_ref[pl.ds(start, n_keys), :]
    vb = v_ref[pl.ds(start, n_keys), :]
    q = q_ref[...]
    lane = lax.broadcasted_iota(jnp.int32, q.shape, 1)
    scale = 1.0 / math.sqrt(HEAD_DIM)
    outs = []
    for hh in range(ATT_LANES // HEAD_DIM):
        in_head = (lane < HEAD_DIM) if hh == 0 else (lane >= HEAD_DIM)
        qm = jnp.where(in_head, q, jnp.zeros_like(q)) * jnp.asarray(scale, q.dtype)
        s = lax.dot_general(qm, kb, (((1,), (1,)), ((), ())), preferred_element_type=F32)
        s = s + b_ref[0, hh]
        m = jnp.max(s, axis=-1, keepdims=True)
        p = jnp.exp(s - m)
        l = jnp.sum(p, axis=-1, keepdims=True)
        o = jnp.dot(p.astype(BF16), vb, preferred_element_type=F32)
        outs.append(o / l)
    o_ref[...] = jnp.where(lane < HEAD_DIM, outs[0], outs[1]).astype(o_ref.dtype)


def neighbourhood_attention_core(qkv, bias_tabs, batch, seq):
    t = qkv.shape[0]
    d = qkv.shape[1] // 3
    rows = seq // GRID_W
    n_rb = rows // ATT_Q_ROWS
    n_hp = d // ATT_LANES
    q_tok = ATT_Q_ROWS * GRID_W

    def bias_map(b, hp, rb):
        return (jnp.where(rb == 0, 0, jnp.where(rb == n_rb - 1, 2, 1)), hp, 0, 0)

    return pl.pallas_call(
        functools.partial(_attention_kernel, rows=rows),
        grid=(batch, n_hp, n_rb),
        in_specs=[pl.BlockSpec((q_tok, ATT_LANES), lambda b, hp, rb: (b * n_rb + rb, hp)),
                  pl.BlockSpec((seq, ATT_LANES), lambda b, hp, rb: (b, n_hp + hp)),
                  pl.BlockSpec((seq, ATT_LANES), lambda b, hp, rb: (b, 2 * n_hp + hp)),
                  pl.BlockSpec((1, 2, q_tok, ATT_BAND_ROWS * GRID_W), bias_map)],
        out_specs=pl.BlockSpec((q_tok, ATT_LANES), lambda b, hp, rb: (b * n_rb + rb, hp)),
        out_shape=jax.ShapeDtypeStruct((t, d), BF16),
        compiler_params=_cparams(("parallel", "parallel", "arbitrary")),
        name="natten_core",
    )(qkv, qkv, qkv, bias_tabs)


def _short_conv_kernel(b_ref, c_ref, v_ref, cp_ref, vp_ref, cn_ref, vn_ref, w_ref, o_ref, *, seq):
    i = pl.program_id(0)
    tm = b_ref.shape[0]
    u = c_ref[...] * v_ref[...]
    u_before = cp_ref[7:8, :] * vp_ref[7:8, :]
    u_after = cn_ref[0:1, :] * vn_ref[0:1, :]
    row = lax.broadcasted_iota(jnp.int32, u.shape, 0)
    pos = (i * tm + row) % seq
    u_prev = jnp.where(row == 0, u_before, pltpu.roll(u, 1, 0))
    u_prev = jnp.where(pos == 0, 0.0, u_prev)
    u_next = jnp.where(row == tm - 1, u_after, pltpu.roll(u, tm - 1, 0))
    u_next = jnp.where(pos == seq - 1, 0.0, u_next)
    conv = u_prev * w_ref[0:1, :] + u * w_ref[1:2, :] + u_next * w_ref[2:3, :]
    o_ref[...] = (b_ref[...] * conv).astype(o_ref.dtype)


def short_conv_gate(y, w_conv, seq):
    t = y.shape[0]
    d = y.shape[1] // 3
    cw = 512
    nj = d // cw
    tm = TOK_TILE
    n8 = t // 8

    def prev_map(off):
        return lambda i, j: (jnp.maximum(i * (tm // 8) - 1, 0), off * nj + j)

    def next_map(off):
        return lambda i, j: (jnp.minimum((i + 1) * (tm // 8), n8 - 1), off * nj + j)

    return pl.pallas_call(
        functools.partial(_short_conv_kernel, seq=seq),
        grid=(t // tm, nj),
        in_specs=[pl.BlockSpec((tm, cw), lambda i, j: (i, j)),
                  pl.BlockSpec((tm, cw), lambda i, j: (i, nj + j)),
                  pl.BlockSpec((tm, cw), lambda i, j: (i, 2 * nj + j)),
                  pl.BlockSpec((8, cw), prev_map(1)),
                  pl.BlockSpec((8, cw), prev_map(2)),
                  pl.BlockSpec((8, cw), next_map(1)),
                  pl.BlockSpec((8, cw), next_map(2)),
                  pl.BlockSpec((3, cw), lambda i, j: (0, j))],
        out_specs=pl.BlockSpec((tm, cw), lambda i, j: (i, j)),
        out_shape=jax.ShapeDtypeStruct((t, d), BF16),
        compiler_params=_cparams(("parallel", "parallel")),
        name="short_conv_gate",
    )(y, y, y, y, y, y, y, w_conv)


def _top16(sc, want_rank):
    n, l = sc.shape
    iota_n = lax.broadcasted_iota(jnp.int32, (n, l), 0).astype(F32)
    iota_k = lax.broadcasted_iota(jnp.int32, (PEER_TOPK, l), 0).astype(F32)

    def body(r, carry):
        sc, rank, sv, si = carry
        rf = r.astype(F32)
        m = jnp.max(sc, axis=0, keepdims=True)
        idx = jnp.min(jnp.where(sc == m, iota_n, float(n)), axis=0, keepdims=True)
        sel = iota_n == idx
        if want_rank:
            rank = jnp.where(sel, rf, rank)
        sc = jnp.where(sel, -jnp.inf, sc)
        sv = jnp.where(iota_k == rf, m, sv)
        si = jnp.where(iota_k == rf, idx, si)
        return sc, rank, sv, si

    init = (sc, jnp.full((n, l), float(PEER_TOPK), F32),
            jnp.zeros((PEER_TOPK, l), F32), jnp.zeros((PEER_TOPK, l), F32))
    _, rank, sv, si = lax.fori_loop(0, PEER_TOPK, body, init)
    return sv, si, rank


def _merge_top16(sv1, sv2):
    k, l = sv1.shape
    iota_k = lax.broadcasted_iota(jnp.int32, (k, l), 0).astype(F32)
    top = sv1[0:1, :] + sv2[0:1, :]

    def body(_, carry):
        front, cnt, z = carry
        m = jnp.max(front, axis=0, keepdims=True)
        a_star = jnp.min(jnp.where(front == m, iota_k, float(k)), axis=0, keepdims=True)
        sel = iota_k == a_star
        z = z + jnp.exp(m - top)
        cnt = jnp.where(sel, cnt + 1.0, cnt)
        c_sel = jnp.sum(jnp.where(sel, cnt, 0.0), axis=0, keepdims=True)
        nxt = jnp.sum(jnp.where(iota_k == c_sel, sv2, 0.0), axis=0, keepdims=True)
        nxt = jnp.where(c_sel >= float(k), -jnp.inf, nxt)
        front = jnp.where(sel, sv1 + nxt, front)
        return front, cnt, z

    init = (sv1 + sv2[0:1, :], jnp.zeros((k, l), F32), jnp.zeros((1, l), F32))
    _, cnt, z = lax.fori_loop(0, PEER_TOPK, body, init)
    return cnt, z


def _peer_weights_kernel(xt_ref, wq_ref, keys_ref, r2_ref, e2_ref, n_ref, e1_ref, q_scr, s_scr):
    q_scr[...] = jnp.dot(wq_ref[...], xt_ref[...], preferred_element_type=F32).astype(BF16)
    n_keys, lanes = PEER_N_KEYS, 128
    iota_n = lax.broadcasted_iota(jnp.int32, (n_keys, lanes), 0).astype(F32)

    def head(h, _):
        r1 = pl.multiple_of(h * 2 * PEER_HALF_DIM, PEER_HALF_DIM)
        r2 = pl.multiple_of(r1 + PEER_HALF_DIM, PEER_HALF_DIM)
        s_scr[0] = jnp.dot(keys_ref[2 * h], q_scr[pl.ds(r1, PEER_HALF_DIM), :],
                           preferred_element_type=F32)
        s_scr[1] = jnp.dot(keys_ref[2 * h + 1], q_scr[pl.ds(r2, PEER_HALF_DIM), :],
                           preferred_element_type=F32)
        for lt in range(xt_ref.shape[1] // lanes):
            ls = slice(lt * lanes, (lt + 1) * lanes)
            s1 = s_scr[0, :, ls]
            s2 = s_scr[1, :, ls]
            sv1, si1, _ = _top16(s1, want_rank=False)
            sv2, _, rank2 = _top16(s2, want_rank=True)
            cnt, z = _merge_top16(sv1, sv2)
            n_sel = jnp.zeros((n_keys, lanes), F32)
            for a in range(PEER_TOPK):
                n_sel = jnp.where(iota_n == si1[a:a + 1, :], cnt[a:a + 1, :], n_sel)
            r2_ref[h, :, ls] = rank2.astype(r2_ref.dtype)
            e2_ref[h, :, ls] = jnp.exp(s2 - sv2[0:1, :]).astype(e2_ref.dtype)
            n_ref[h, :, ls] = n_sel
            e1_ref[h, :, ls] = jnp.exp(s1 - sv1[0:1, :]) / z
        return 0

    lax.fori_loop(0, PEER_HEADS, head, 0)


def peer_weights(xt, wq_t, keys):
    d, t = xt.shape
    tm = PEER_PRE_TILE
    out16 = jax.ShapeDtypeStruct((PEER_HEADS, PEER_N_KEYS, t), BF16)
    out32 = jax.ShapeDtypeStruct((PEER_HEADS, PEER_N_KEYS, t), F32)
    ospec = pl.BlockSpec((PEER_HEADS, PEER_N_KEYS, tm), lambda i: (0, 0, i))
    return pl.pallas_call(
        _peer_weights_kernel,
        grid=(t // tm,),
        in_specs=[pl.BlockSpec((d, tm), lambda i: (0, i)),
                  pl.BlockSpec(wq_t.shape, lambda i: (0, 0)),
                  pl.BlockSpec(keys.shape, lambda i: (0, 0, 0))],
        out_specs=[ospec, ospec, ospec, ospec],
        out_shape=[out16, out16, out32, out32],
        scratch_shapes=[pltpu.VMEM((wq_t.shape[0], tm), BF16),
                        pltpu.VMEM((2, PEER_N_KEYS, tm), F32)],
        compiler_params=_cparams(("parallel",)),
        name="peer_weights",
    )(xt, wq_t, keys)


def _peer_hidden_rows(i, a_ref, hid_ref, row0, r2_ref, e2_ref, n_ref, e1_ref):
    nk, lanes, pack = PEER_N_KEYS, 128, 16
    n_rows = [n_ref[h, pl.ds(i, 1), :] for h in range(PEER_HEADS)]
    e1_rows = [e1_ref[h, pl.ds(i, 1), :] for h in range(PEER_HEADS)]
    zero = jnp.zeros((pack, lanes), BF16)
    for lt in range(a_ref.shape[1] // lanes):
        ls = slice(lt * lanes, (lt + 1) * lanes)
        n_b = [jnp.broadcast_to(r[:, ls], (pack, lanes)).astype(BF16) for r in n_rows]
        e1_b = [jnp.broadcast_to(r[:, ls], (pack, lanes)).astype(BF16) for r in e1_rows]
        for jb in range(nk // pack):
            js = slice(jb * pack, (jb + 1) * pack)
            g = None
            for h in range(PEER_HEADS):
                term = jnp.where(r2_ref[h, js, ls] < n_b[h], e2_ref[h, js, ls] * e1_b[h], zero)
                g = term if g is None else g + term
            rs = slice(row0 + jb * pack, row0 + (jb + 1) * pack)
            a = a_ref[rs, ls]
            act = 0.5 * a * (1.0 + lax.erf(a * math.sqrt(0.5)))
            hid_ref[rs, ls] = act.astype(BF16) * g


def _peer_dense_kernel(xt_ref, r2_ref, e2_ref, n_ref, e1_ref, wup_ref, wdn_ref, h_ref, o_ref,
                       acc_ref, a0_ref, a1_ref, hid0_ref, hid1_ref, r2_scr, e2_scr):
    e = pl.program_id(1)
    nk = PEER_N_KEYS
    grp = a0_ref.shape[0]
    n_grp = wup_ref.shape[0] // grp
    a_refs = (a0_ref, a1_ref)
    hid_refs = (hid0_ref, hid1_ref)

    @pl.when(e == 0)
    def _():
        acc_ref[...] = jnp.zeros_like(acc_ref)
        r2_scr[...] = r2_ref[...]
        e2_scr[...] = e2_ref[...]

    for p in range(n_grp + 2):
        if p < n_grp:
            a_refs[p % 2][...] = jnp.dot(wup_ref[p * grp:(p + 1) * grp, :], xt_ref[...],
                                         preferred_element_type=F32)
        if 1 <= p <= n_grp:
            q = p - 1
            for il in range(grp // nk):
                i = (e * n_grp + q) * (grp // nk) + il
                _peer_hidden_rows(i, a_refs[q % 2], hid_refs[q % 2], il * nk,
                                  r2_scr, e2_scr, n_ref, e1_ref)
        if p >= 2:
            q = p - 2
            acc_ref[...] += lax.dot_general(hid_refs[q % 2][...], wdn_ref[q * grp:(q + 1) * grp, :],
                                            (((0,), (0,)), ((), ())), preferred_element_type=F32)

    @pl.when(e == pl.num_programs(1) - 1)
    def _():
        o_ref[...] = h_ref[...] + acc_ref[...]


def peer_dense(xt, r2, e2, n_sel, e1n, w_up, w_down, h):
    d, t = xt.shape
    n_exp = w_up.shape[0]
    tm, te = PEER_TOK_TILE, PEER_EXP_TILE
    wspec = pl.BlockSpec((PEER_HEADS, PEER_N_KEYS, tm), lambda i, e: (0, 0, i))
    return pl.pallas_call(
        _peer_dense_kernel,
        grid=(t // tm, n_exp // te),
        in_specs=[pl.BlockSpec((d, tm), lambda i, e: (0, i)),
                  wspec, wspec, wspec, wspec,
                  pl.BlockSpec((te, d), lambda i, e: (e, 0)),
                  pl.BlockSpec((te, d), lambda i, e: (e, 0)),
                  pl.BlockSpec((tm, d), lambda i, e: (i, 0))],
        out_specs=pl.BlockSpec((tm, d), lambda i, e: (i, 0)),
        out_shape=jax.ShapeDtypeStruct((t, d), F32),
        scratch_shapes=[pltpu.VMEM((tm, d), F32),
                        pltpu.VMEM((PEER_STAGE_ROWS, tm), F32),
                        pltpu.VMEM((PEER_STAGE_ROWS, tm), F32),
                        pltpu.VMEM((PEER_STAGE_ROWS, tm), BF16),
                        pltpu.VMEM((PEER_STAGE_ROWS, tm), BF16),
                        pltpu.VMEM((PEER_HEADS, PEER_N_KEYS, tm), BF16),
                        pltpu.VMEM((PEER_HEADS, PEER_N_KEYS, tm), BF16)],
        compiler_params=_cparams(("parallel", "arbitrary")),
        name="peer_dense",
    )(xt, r2, e2, n_sel, e1n, w_up, w_down, h)


def kernel(x, norm_mix, norm_ffn, norm_final, attn_w_qkv, attn_w_o, attn_rel_bias,
           conv_w_in, conv_w_conv, conv_w_out, peer_w_query, peer_sub_keys,
           peer_w_up, peer_w_down):
    batch, seq, d = x.shape
    depth = norm_mix.shape[0]
    rows = seq // GRID_W
    h = x.reshape(batch * seq, d)
    for i in range(depth):
        j = i // 2
        if i % 2 == 0:
            qkv = norm_matmul(h, norm_mix[i], attn_w_qkv[j].astype(BF16), BF16)
            bias_tabs = _attention_bias_tables(attn_rel_bias[j], rows)
            mixed = neighbourhood_attention_core(qkv, bias_tabs, batch, seq)
            h = matmul_residual(mixed, attn_w_o[j].astype(BF16), h)
        else:
            y = norm_matmul(h, norm_mix[i], conv_w_in[j].astype(BF16), F32)
            mixed = short_conv_gate(y, conv_w_conv[j], seq)
            h = matmul_residual(mixed, conv_w_out[j].astype(BF16), h)
        xt = norm_transpose(h, norm_ffn[i])
        keys = peer_sub_keys[i].reshape(PEER_HEADS * 2, PEER_N_KEYS, PEER_HALF_DIM).astype(BF16)
        r2, e2, n_sel, e1n = peer_weights(xt, peer_w_query[i].T.astype(BF16), keys)
        h = peer_dense(xt, r2, e2, n_sel, e1n, peer_w_up[i].astype(BF16),
                       peer_w_down[i].astype(BF16), h)
    return final_norm(h, norm_final).reshape(batch, seq, d)
```

```python
import functools
import math

import numpy as np
import jax
import jax.numpy as jnp
from jax import lax
from jax.experimental import pallas as pl
from jax.experimental.pallas import tpu as pltpu

F32 = jnp.float32
BF16 = jnp.bfloat16

RMS_EPS = 1e-6
NEG_INF = -1e30

GRID_W = 64
N_HEADS = 16
HEAD_DIM = 64
MAX_WIN_ROWS = 8
WIN_COLS = 16
ATT_Q_ROWS = 4
ATT_BAND_ROWS = ATT_Q_ROWS + MAX_WIN_ROWS - 1
ATT_LANES = 128

PEER_HEADS = 8
PEER_N_KEYS = 128
PEER_TOPK = 16
PEER_HALF_DIM = 128

TOK_TILE = 512
PEER_PRE_TILE = 512
PEER_TOK_TILE = 512
PEER_EXP_TILE = 2048
PEER_STAGE_ROWS = 256
VMEM_LIMIT = 56 * 1024 * 1024


def _cparams(sem, flags=None):
    return pltpu.CompilerParams(dimension_semantics=sem, vmem_limit_bytes=VMEM_LIMIT, flags=flags)


def _rms(x, g):
    return x * lax.rsqrt(jnp.mean(x * x, axis=-1, keepdims=True) + RMS_EPS) * g


def _norm_matmul_kernel(x_ref, g_ref, w_ref, o_ref):
    y = _rms(x_ref[...], g_ref[...])
    o_ref[...] = jnp.dot(y.astype(BF16), w_ref[...],
                         preferred_element_type=F32).astype(o_ref.dtype)


def norm_matmul(x, gain, w, out_dtype):
    t, d = x.shape
    n = w.shape[1]
    return pl.pallas_call(
        _norm_matmul_kernel,
        grid=(t // TOK_TILE,),
        in_specs=[pl.BlockSpec((TOK_TILE, d), lambda i: (i, 0)),
                  pl.BlockSpec((1, d), lambda i: (0, 0)),
                  pl.BlockSpec((d, n), lambda i: (0, 0))],
        out_specs=pl.BlockSpec((TOK_TILE, n), lambda i: (i, 0)),
        out_shape=jax.ShapeDtypeStruct((t, n), out_dtype),
        compiler_params=_cparams(("parallel",)),
        name="norm_matmul",
    )(x, gain.reshape(1, d), w)


def _matmul_residual_kernel(a_ref, w_ref, h_ref, o_ref):
    o_ref[...] = h_ref[...] + jnp.dot(a_ref[...], w_ref[...], preferred_element_type=F32)


def matmul_residual(a, w, h):
    t, k = a.shape
    d = w.shape[1]
    return pl.pallas_call(
        _matmul_residual_kernel,
        grid=(t // TOK_TILE,),
        in_specs=[pl.BlockSpec((TOK_TILE, k), lambda i: (i, 0)),
                  pl.BlockSpec((k, d), lambda i: (0, 0)),
                  pl.BlockSpec((TOK_TILE, d), lambda i: (i, 0))],
        out_specs=pl.BlockSpec((TOK_TILE, d), lambda i: (i, 0)),
        out_shape=jax.ShapeDtypeStruct((t, d), F32),
        compiler_params=_cparams(("parallel",)),
        name="matmul_residual",
    )(a, w, h)


def _norm_transpose_kernel(x_ref, g_ref, o_ref):
    o_ref[...] = _rms(x_ref[...], g_ref[...]).T.astype(o_ref.dtype)


def norm_transpose(x, gain):
    t, d = x.shape
    return pl.pallas_call(
        _norm_transpose_kernel,
        grid=(t // TOK_TILE,),
        in_specs=[pl.BlockSpec((TOK_TILE, d), lambda i: (i, 0)),
                  pl.BlockSpec((1, d), lambda i: (0, 0))],
        out_specs=pl.BlockSpec((d, TOK_TILE), lambda i: (0, i)),
        out_shape=jax.ShapeDtypeStruct((d, t), BF16),
        compiler_params=_cparams(("parallel",)),
        name="norm_transpose",
    )(x, gain.reshape(1, d))


def _final_norm_kernel(x_ref, g_ref, o_ref):
    o_ref[...] = _rms(x_ref[...], g_ref[...])


def final_norm(x, gain):
    t, d = x.shape
    return pl.pallas_call(
        _final_norm_kernel,
        grid=(t // TOK_TILE,),
        in_specs=[pl.BlockSpec((TOK_TILE, d), lambda i: (i, 0)),
                  pl.BlockSpec((1, d), lambda i: (0, 0))],
        out_specs=pl.BlockSpec((TOK_TILE, d), lambda i: (i, 0)),
        out_shape=jax.ShapeDtypeStruct((t, d), F32),
        compiler_params=_cparams(("parallel",)),
        name="final_norm",
    )(x, gain.reshape(1, d))


def _attention_bias_tables(rel_bias, rows):
    n_h = rel_bias.shape[0]
    n_dr = 2 * MAX_WIN_ROWS - 1
    w = GRID_W
    edge = w - WIN_COLS
    line = jnp.pad(rel_bias, ((0, 0), (0, 0), (edge, edge + 1)), mode="edge")
    flat = jnp.tile(line, (1, 1, w))[:, :, w - 1:w - 1 + w * (2 * w - 1)]
    toep = flat.reshape(n_h, n_dr, w, 2 * w - 1)[:, :, :, :w]
    qc = np.arange(w)[:, None]
    kc = np.arange(w)[None, :]
    ws = np.clip(qc - WIN_COLS // 2, 0, w - WIN_COLS)
    toep = jnp.where(((kc >= ws) & (kc < ws + WIN_COLS))[None, None], toep, NEG_INF)
    pad = ATT_Q_ROWS - 1
    toep = jnp.pad(toep, ((0, 0), (pad, pad), (0, 0), (0, 0)), constant_values=NEG_INF)
    tabs = []
    for r0 in (0, ATT_Q_ROWS, rows - ATT_Q_ROWS):
        band = int(np.clip(r0 - MAX_WIN_ROWS // 2, 0, rows - ATT_BAND_ROWS))
        blocks = []
        for ql in range(ATT_Q_ROWS):
            lo = band - (r0 + ql) + MAX_WIN_ROWS - 1 + pad
            blk = toep[:, lo:lo + ATT_BAND_ROWS]
            blocks.append(blk.transpose(0, 2, 1, 3).reshape(n_h, w, ATT_BAND_ROWS * w))
        tab = jnp.stack(blocks, axis=1).reshape(n_h, ATT_Q_ROWS * w, ATT_BAND_ROWS * w)
        qr = r0 + np.arange(ATT_Q_ROWS)[:, None]
        kr = band + np.arange(ATT_BAND_ROWS)[None, :]
        rs = np.clip(qr - MAX_WIN_ROWS // 2, 0, rows - MAX_WIN_ROWS)
        row_ok = (kr >= rs) & (kr < rs + MAX_WIN_ROWS)
        row_ok = np.repeat(np.repeat(row_ok, w, axis=0), w, axis=1)
        tabs.append(jnp.where(row_ok[None], tab, NEG_INF))
    return jnp.stack(tabs)


def _attention_kernel(q_ref, k_ref, v_ref, b_ref, o_ref, *, rows):
    rb = pl.program_id(2)
    band = jnp.clip(rb * ATT_Q_ROWS - MAX_WIN_ROWS // 2, 0, rows - ATT_BAND_ROWS)
    start = pl.multiple_of(band * GRID_W, GRID_W)
    n_keys = ATT_BAND_ROWS * GRID_W
    kb = k_ref[pl.ds(start, n_keys), :]
    vb = v_ref[pl.ds(start, n_keys), :]
    q = q_ref[...]
    lane = lax.broadcasted_iota(jnp.int32, q.shape, 1)
    scale = 1.0 / math.sqrt(HEAD_DIM)
    outs = []
    for hh in range(ATT_LANES // HEAD_DIM):
        in_head = (lane < HEAD_DIM) if hh == 0 else (lane >= HEAD_DIM)
        qm = jnp.where(in_head, q, jnp.zeros_like(q)) * jnp.asarray(scale, q.dtype)
        s = lax.dot_general(qm, kb, (((1,), (1,)), ((), ())), preferred_element_type=F32)
        s = s + b_ref[0, hh]
        m = jnp.max(s, axis=-1, keepdims=True)
        p = jnp.exp(s - m)
        l = jnp.sum(p, axis=-1, keepdims=True)
        o = jnp.dot(p.astype(BF16), vb, preferred_element_type=F32)
        outs.append(o / l)
    o_ref[...] = jnp.where(lane < HEAD_DIM, outs[0], outs[1]).astype(o_ref.dtype)


def neighbourhood_attention_core(qkv, bias_tabs, batch, seq):
    t = qkv.shape[0]
    d = qkv.shape[1] // 3
    rows = seq // GRID_W
    n_rb = rows // ATT_Q_ROWS
    n_hp = d // ATT_LANES
    q_tok = ATT_Q_ROWS * GRID_W

    def bias_map(b, hp, rb):
        return (jnp.where(rb == 0, 0, jnp.where(rb == n_rb - 1, 2, 1)), hp, 0, 0)

    return pl.pallas_call(
        functools.partial(_attention_kernel, rows=rows),
        grid=(batch, n_hp, n_rb),
        in_specs=[pl.BlockSpec((q_tok, ATT_LANES), lambda b, hp, rb: (b * n_rb + rb, hp)),
                  pl.BlockSpec((seq, ATT_LANES), lambda b, hp, rb: (b, n_hp + hp)),
                  pl.BlockSpec((seq, ATT_LANES), lambda b, hp, rb: (b, 2 * n_hp + hp)),
                  pl.BlockSpec((1, 2, q_tok, ATT_BAND_ROWS * GRID_W), bias_map)],
        out_specs=pl.BlockSpec((q_tok, ATT_LANES), lambda b, hp, rb: (b * n_rb + rb, hp)),
        out_shape=jax.ShapeDtypeStruct((t, d), BF16),
        compiler_params=_cparams(("parallel", "parallel", "arbitrary")),
        name="natten_core",
    )(qkv, qkv, qkv, bias_tabs)


def _short_conv_kernel(b_ref, c_ref, v_ref, cp_ref, vp_ref, cn_ref, vn_ref, w_ref, o_ref, *, seq):
    i = pl.program_id(0)
    tm = b_ref.shape[0]
    u = c_ref[...] * v_ref[...]
    u_before = cp_ref[7:8, :] * vp_ref[7:8, :]
    u_after = cn_ref[0:1, :] * vn_ref[0:1, :]
    row = lax.broadcasted_iota(jnp.int32, u.shape, 0)
    pos = (i * tm + row) % seq
    u_prev = jnp.where(row == 0, u_before, pltpu.roll(u, 1, 0))
    u_prev = jnp.where(pos == 0, 0.0, u_prev)
    u_next = jnp.where(row == tm - 1, u_after, pltpu.roll(u, tm - 1, 0))
    u_next = jnp.where(pos == seq - 1, 0.0, u_next)
    conv = u_prev * w_ref[0:1, :] + u * w_ref[1:2, :] + u_next * w_ref[2:3, :]
    o_ref[...] = (b_ref[...] * conv).astype(o_ref.dtype)


def short_conv_gate(y, w_conv, seq):
    t = y.shape[0]
    d = y.shape[1] // 3
    cw = 512
    nj = d // cw
    tm = TOK_TILE
    n8 = t // 8

    def prev_map(off):
        return lambda i, j: (jnp.maximum(i * (tm // 8) - 1, 0), off * nj + j)

    def next_map(off):
        return lambda i, j: (jnp.minimum((i + 1) * (tm // 8), n8 - 1), off * nj + j)

    return pl.pallas_call(
        functools.partial(_short_conv_kernel, seq=seq),
        grid=(t // tm, nj),
        in_specs=[pl.BlockSpec((tm, cw), lambda i, j: (i, j)),
                  pl.BlockSpec((tm, cw), lambda i, j: (i, nj + j)),
                  pl.BlockSpec((tm, cw), lambda i, j: (i, 2 * nj + j)),
                  pl.BlockSpec((8, cw), prev_map(1)),
                  pl.BlockSpec((8, cw), prev_map(2)),
                  pl.BlockSpec((8, cw), next_map(1)),
                  pl.BlockSpec((8, cw), next_map(2)),
                  pl.BlockSpec((3, cw), lambda i, j: (0, j))],
        out_specs=pl.BlockSpec((tm, cw), lambda i, j: (i, j)),
        out_shape=jax.ShapeDtypeStruct((t, d), BF16),
        compiler_params=_cparams(("parallel", "parallel")),
        name="short_conv_gate",
    )(y, y, y, y, y, y, y, w_conv)


def _top16(sc, want_rank):
    n, l = sc.shape
    iota_n = lax.broadcasted_iota(jnp.int32, (n, l), 0).astype(F32)
    iota_k = lax.broadcasted_iota(jnp.int32, (PEER_TOPK, l), 0).astype(F32)

    def body(r, carry):
        sc, rank, sv, si = carry
        rf = jnp.asarray(r, jnp.int32).astype(F32)
        m = jnp.max(sc, axis=0, keepdims=True)
        idx = jnp.min(jnp.where(sc == m, iota_n, float(n)), axis=0, keepdims=True)
        sel = iota_n == idx
        if want_rank:
            rank = jnp.where(sel, rf, rank)
        sc = jnp.where(sel, -jnp.inf, sc)
        sv = jnp.where(iota_k == rf, m, sv)
        si = jnp.where(iota_k == rf, idx, si)
        return sc, rank, sv, si

    init = (sc, jnp.full((n, l), float(PEER_TOPK), F32),
            jnp.zeros((PEER_TOPK, l), F32), jnp.zeros((PEER_TOPK, l), F32))
    _, rank, sv, si = lax.fori_loop(0, PEER_TOPK, body, init)
    return sv, si, rank


def _merge_top16(sv1, sv2):
    k, l = sv1.shape
    iota_k = lax.broadcasted_iota(jnp.int32, (k, l), 0).astype(F32)
    top = sv1[0:1, :] + sv2[0:1, :]

    def body(_, carry):
        front, cnt, z = carry
        m = jnp.max(front, axis=0, keepdims=True)
        a_star = jnp.min(jnp.where(front == m, iota_k, float(k)), axis=0, keepdims=True)
        sel = iota_k == a_star
        z = z + jnp.exp(m - top)
        cnt = jnp.where(sel, cnt + 1.0, cnt)
        c_sel = jnp.sum(jnp.where(sel, cnt, 0.0), axis=0, keepdims=True)
        nxt = jnp.sum(jnp.where(iota_k == c_sel, sv2, 0.0), axis=0, keepdims=True)
        nxt = jnp.where(c_sel >= float(k), -jnp.inf, nxt)
        front = jnp.where(sel, sv1 + nxt, front)
        return front, cnt, z

    init = (sv1 + sv2[0:1, :], jnp.zeros((k, l), F32), jnp.zeros((1, l), F32))
    _, cnt, z = lax.fori_loop(0, PEER_TOPK, body, init)
    return cnt, z


def _tree(op, xs):
    while len(xs) > 1:
        xs = [op(xs[k], xs[k + 1]) for k in range(0, len(xs) - 1, 2)] + (xs[-1:] if len(xs) % 2 else [])
    return xs[0]


def _all_sublanes(op, x):
    for shift in (4, 2, 1):
        x = op(x, pltpu.roll(x, shift, 0))
    return x


def _peer_scores(h, keys_ref, q_scr, s_scr):
    for p in range(2):
        r0 = pl.multiple_of((2 * h + p) * PEER_HALF_DIM, PEER_HALF_DIM)
        s_scr[p] = jnp.dot(keys_ref[2 * h + p], q_scr[pl.ds(r0, PEER_HALF_DIM), :],
                           preferred_element_type=F32)


def _peer_head_exact(h, s_scr, r2_ref, e2_ref, n_ref, e1_ref):
    n_keys, lanes = PEER_N_KEYS, 128
    iota_n = lax.broadcasted_iota(jnp.int32, (n_keys, lanes), 0).astype(F32)
    for lt in range(s_scr.shape[2] // lanes):
        ls = slice(lt * lanes, (lt + 1) * lanes)
        s1 = s_scr[0, :, ls]
        s2 = s_scr[1, :, ls]
        sv1, si1, _ = _top16(s1, want_rank=False)
        sv2, _, rank2 = _top16(s2, want_rank=True)
        cnt, z = _merge_top16(sv1, sv2)
        n_sel = jnp.zeros((n_keys, lanes), F32)
        for a in range(PEER_TOPK):
            n_sel = jnp.where(iota_n == si1[a:a + 1, :], cnt[a:a + 1, :], n_sel)
        r2_ref[h, :, ls] = rank2.astype(r2_ref.dtype)
        e2_ref[h, :, ls] = jnp.exp(s2 - sv2[0:1, :]).astype(e2_ref.dtype)
        n_ref[h, :, ls] = n_sel
        e1_ref[h, :, ls] = jnp.exp(s1 - sv1[0:1, :]) * (1.0 / z)


def _peer_head_distinct(h, s_scr, t_scr, sv_scr, r2_ref, e2_ref, n_ref, e1_ref):
    n_keys, lanes, sub = PEER_N_KEYS, 128, 8
    n_vr = n_keys // sub
    tm = s_scr.shape[2]
    for p in range(2):
        prev = jnp.full((sub, tm), jnp.inf, F32)
        for r in range(PEER_TOPK):
            below = []
            for v in range(n_vr):
                x = s_scr[p, v * sub:(v + 1) * sub, :]
                below.append(jnp.where(x < prev, x, -jnp.inf))
            prev = _all_sublanes(jnp.maximum, _tree(jnp.maximum, below))
            t_scr[p, r] = prev
            sv_scr[p, r:r + 1, :] = prev[0:1, :]
    cnt, z = _merge_top16(sv_scr[0], sv_scr[1])
    inv_z = 1.0 / z
    one = jnp.ones((sub, lanes), F32)
    zero = jnp.zeros((sub, lanes), F32)
    bad = zero
    for lt in range(tm // lanes):
        ls = slice(lt * lanes, (lt + 1) * lanes)
        t2 = [t_scr[1, b, :, ls] for b in range(PEER_TOPK)]
        n_top = zero
        for v in range(n_vr):
            vs = slice(v * sub, (v + 1) * sub)
            x = s_scr[1, vs, ls]
            rank = _tree(jnp.add, [jnp.where(t > x, one, zero) for t in t2])
            r2_ref[h, vs, ls] = rank.astype(r2_ref.dtype)
            e2_ref[h, vs, ls] = jnp.exp(x - t2[0]).astype(e2_ref.dtype)
            n_top = n_top + jnp.where(rank < float(PEER_TOPK), one, zero)
        bad = jnp.maximum(bad, jnp.where(_all_sublanes(jnp.add, n_top) != float(PEER_TOPK), one, zero))
        t1 = [t_scr[0, a, :, ls] for a in range(PEER_TOPK)]
        cnt_b = [jnp.broadcast_to(cnt[a:a + 1, ls], (sub, lanes)) for a in range(PEER_TOPK)]
        inv_b = jnp.broadcast_to(inv_z[:, ls], (sub, lanes))
        n_top = zero
        for v in range(n_vr):
            vs = slice(v * sub, (v + 1) * sub)
            x = s_scr[0, vs, ls]
            n_sel = zero
            for a in range(PEER_TOPK):
                n_sel = jnp.where(x == t1[a], cnt_b[a], n_sel)
            n_ref[h, vs, ls] = n_sel
            e1_ref[h, vs, ls] = jnp.exp(x - t1[0]) * inv_b
            n_top = n_top + jnp.where(x >= t1[PEER_TOPK - 1], one, zero)
        bad = jnp.maximum(bad, jnp.where(_all_sublanes(jnp.add, n_top) != float(PEER_TOPK), one, zero))
    return bad


def _peer_weights_kernel(xt_ref, wq_ref, keys_ref, r2_ref, e2_ref, n_ref, e1_ref,
                         q_scr, s_scr, t_scr, sv_scr):
    q_scr[...] = jnp.dot(wq_ref[...], xt_ref[...], preferred_element_type=F32).astype(BF16)

    def distinct_head(h, bad):
        _peer_scores(h, keys_ref, q_scr, s_scr)
        return jnp.maximum(bad, _peer_head_distinct(h, s_scr, t_scr, sv_scr,
                                                    r2_ref, e2_ref, n_ref, e1_ref))

    bad = lax.fori_loop(0, PEER_HEADS, distinct_head, jnp.zeros((8, 128), F32))

    @pl.when(jnp.max(bad) > 0.0)
    def _():
        def exact_head(h, _):
            _peer_scores(h, keys_ref, q_scr, s_scr)
            _peer_head_exact(h, s_scr, r2_ref, e2_ref, n_ref, e1_ref)
            return 0

        lax.fori_loop(0, PEER_HEADS, exact_head, 0)


def peer_weights(xt, wq_t, keys):
    d, t = xt.shape
    tm = PEER_PRE_TILE
    out16 = jax.ShapeDtypeStruct((PEER_HEADS, PEER_N_KEYS, t), BF16)
    out32 = jax.ShapeDtypeStruct((PEER_HEADS, PEER_N_KEYS, t), F32)
    ospec = pl.BlockSpec((PEER_HEADS, PEER_N_KEYS, tm), lambda i: (0, 0, i))
    return pl.pallas_call(
        _peer_weights_kernel,
        grid=(t // tm,),
        in_specs=[pl.BlockSpec((d, tm), lambda i: (0, i)),
                  pl.BlockSpec(wq_t.shape, lambda i: (0, 0)),
                  pl.BlockSpec(keys.shape, lambda i: (0, 0, 0))],
        out_specs=[ospec, ospec, ospec, ospec],
        out_shape=[out16, out16, out32, out32],
        scratch_shapes=[pltpu.VMEM((wq_t.shape[0], tm), BF16),
                        pltpu.VMEM((2, PEER_N_KEYS, tm), F32),
                        pltpu.VMEM((2, PEER_TOPK, 8, tm), F32),
                        pltpu.VMEM((2, PEER_TOPK, tm), F32)],
        compiler_params=_cparams(("parallel",)),
        name="peer_weights",
    )(xt, wq_t, keys)


def _peer_hidden_rows(i, a_ref, hid_ref, row0, r2_ref, e2_ref, n_ref, e1_ref):
    nk, lanes, pack = PEER_N_KEYS, 128, 16
    n_rows = [n_ref[h, pl.ds(i, 1), :] for h in range(PEER_HEADS)]
    e1_rows = [e1_ref[h, pl.ds(i, 1), :] for h in range(PEER_HEADS)]
    zero = jnp.zeros((pack, lanes), BF16)
    for lt in range(a_ref.shape[1] // lanes):
        ls = slice(lt * lanes, (lt + 1) * lanes)
        n_b = [jnp.broadcast_to(r[:, ls], (pack, lanes)).astype(BF16) for r in n_rows]
        e1_b = [jnp.broadcast_to(r[:, ls], (pack, lanes)).astype(BF16) for r in e1_rows]
        for jb in range(nk // pack):
            js = slice(jb * pack, (jb + 1) * pack)
            g = None
            for h in range(PEER_HEADS):
                term = jnp.where(r2_ref[h, js, ls] < n_b[h], e2_ref[h, js, ls] * e1_b[h], zero)
                g = term if g is None else g + term
            rs = slice(row0 + jb * pack, row0 + (jb + 1) * pack)
            a = a_ref[rs, ls]
            act = 0.5 * a * (1.0 + lax.erf(a * math.sqrt(0.5)))
            hid_ref[rs, ls] = act.astype(BF16) * g


def _peer_dense_kernel(xt_ref, r2_ref, e2_ref, n_ref, e1_ref, wup_ref, wdn_ref, h_ref, o_ref,
                       acc_ref, a0_ref, a1_ref, hid0_ref, hid1_ref, r2_scr, e2_scr):
    e = pl.program_id(1)
    nk = PEER_N_KEYS
    grp = a0_ref.shape[0]
    n_grp = wup_ref.shape[0] // grp
    a_refs = (a0_ref, a1_ref)
    hid_refs = (hid0_ref, hid1_ref)

    @pl.when(e == 0)
    def _():
        acc_ref[...] = jnp.zeros_like(acc_ref)
        r2_scr[...] = r2_ref[...]
        e2_scr[...] = e2_ref[...]

    for p in range(n_grp + 2):
        if p < n_grp:
            a_refs[p % 2][...] = jnp.dot(wup_ref[p * grp:(p + 1) * grp, :], xt_ref[...],
                                         preferred_element_type=F32)
        if 1 <= p <= n_grp:
            q = p - 1
            for il in range(grp // nk):
                i = (e * n_grp + q) * (grp // nk) + il
                _peer_hidden_rows(i, a_refs[q % 2], hid_refs[q % 2], il * nk,
                                  r2_scr, e2_scr, n_ref, e1_ref)
        if p >= 2:
            q = p - 2
            acc_ref[...] += lax.dot_general(hid_refs[q % 2][...], wdn_ref[q * grp:(q + 1) * grp, :],
                                            (((0,), (0,)), ((), ())), preferred_element_type=F32)

    @pl.when(e == pl.num_programs(1) - 1)
    def _():
        o_ref[...] = h_ref[...] + acc_ref[...]


def peer_dense(xt, r2, e2, n_sel, e1n, w_up, w_down, h):
    d, t = xt.shape
    n_exp = w_up.shape[0]
    tm, te = PEER_TOK_TILE, PEER_EXP_TILE
    wspec = pl.BlockSpec((PEER_HEADS, PEER_N_KEYS, tm), lambda i, e: (0, 0, i))
    return pl.pallas_call(
        _peer_dense_kernel,
        grid=(t // tm, n_exp // te),
        in_specs=[pl.BlockSpec((d, tm), lambda i, e: (0, i)),
                  wspec, wspec, wspec, wspec,
                  pl.BlockSpec((te, d), lambda i, e: (e, 0)),
                  pl.BlockSpec((te, d), lambda i, e: (e, 0)),
                  pl.BlockSpec((tm, d), lambda i, e: (i, 0))],
        out_specs=pl.BlockSpec((tm, d), lambda i, e: (i, 0)),
        out_shape=jax.ShapeDtypeStruct((t, d), F32),
        scratch_shapes=[pltpu.VMEM((tm, d), F32),
                        pltpu.VMEM((PEER_STAGE_ROWS, tm), F32),
                        pltpu.VMEM((PEER_STAGE_ROWS, tm), F32),
                        pltpu.VMEM((PEER_STAGE_ROWS, tm), BF16),
                        pltpu.VMEM((PEER_STAGE_ROWS, tm), BF16),
                        pltpu.VMEM((PEER_HEADS, PEER_N_KEYS, tm), BF16),
                        pltpu.VMEM((PEER_HEADS, PEER_N_KEYS, tm), BF16)],
        compiler_params=_cparams(("parallel", "arbitrary")),
        name="peer_dense",
    )(xt, r2, e2, n_sel, e1n, w_up, w_down, h)


def kernel(x, norm_mix, norm_ffn, norm_final, attn_w_qkv, attn_w_o, attn_rel_bias,
           conv_w_in, conv_w_conv, conv_w_out, peer_w_query, peer_sub_keys,
           peer_w_up, peer_w_down):
    batch, seq, d = x.shape
    depth = norm_mix.shape[0]
    rows = seq // GRID_W
    h = x.reshape(batch * seq, d)
    for i in range(depth):
        j = i // 2
        if i % 2 == 0:
            qkv = norm_matmul(h, norm_mix[i], attn_w_qkv[j].astype(BF16), BF16)
            bias_tabs = _attention_bias_tables(attn_rel_bias[j], rows)
            mixed = neighbourhood_attention_core(qkv, bias_tabs, batch, seq)
            h = matmul_residual(mixed, attn_w_o[j].astype(BF16), h)
        else:
            y = norm_matmul(h, norm_mix[i], conv_w_in[j].astype(BF16), F32)
            mixed = short_conv_gate(y, conv_w_conv[j], seq)
            h = matmul_residual(mixed, conv_w_out[j].astype(BF16), h)
        xt = norm_transpose(h, norm_ffn[i])
        keys = peer_sub_keys[i].reshape(PEER_HEADS * 2, PEER_N_KEYS, PEER_HALF_DIM).astype(BF16)
        r2, e2, n_sel, e1n = peer_weights(xt, peer_w_query[i].T.astype(BF16), keys)
        h = peer_dense(xt, r2, e2, n_sel, e1n, peer_w_up[i].astype(BF16),
                       peer_w_down[i].astype(BF16), h)
    return final_norm(h, norm_final).reshape(batch, seq, d)
```

```python
import functools
import math

import numpy as np
import jax
import jax.numpy as jnp
from jax import lax
from jax.experimental import pallas as pl
from jax.experimental.pallas import tpu as pltpu

F32 = jnp.float32
BF16 = jnp.bfloat16

RMS_EPS = 1e-6
NEG_INF = -1e30

GRID_W = 64
N_HEADS = 16
HEAD_DIM = 64
MAX_WIN_ROWS = 8
WIN_COLS = 16
ATT_Q_ROWS = 4
ATT_BAND_ROWS = ATT_Q_ROWS + MAX_WIN_ROWS - 1
ATT_LANES = 128

PEER_HEADS = 8
PEER_N_KEYS = 128
PEER_TOPK = 16
PEER_HALF_DIM = 128

TOK_TILE = 512
PEER_PRE_TILE = 512
PEER_TOK_TILE = 512
PEER_EXP_TILE = 2048
VMEM_LIMIT = 56 * 1024 * 1024


def _cparams(sem, flags=None):
    return pltpu.CompilerParams(dimension_semantics=sem, vmem_limit_bytes=VMEM_LIMIT, flags=flags)


def _rms(x, g):
    return x * lax.rsqrt(jnp.mean(x * x, axis=-1, keepdims=True) + RMS_EPS) * g


def _norm_matmul_kernel(x_ref, g_ref, w_ref, o_ref):
    y = _rms(x_ref[...], g_ref[...])
    o_ref[...] = jnp.dot(y.astype(BF16), w_ref[...],
                         preferred_element_type=F32).astype(o_ref.dtype)


def norm_matmul(x, gain, w, out_dtype):
    t, d = x.shape
    n = w.shape[1]
    return pl.pallas_call(
        _norm_matmul_kernel,
        grid=(t // TOK_TILE,),
        in_specs=[pl.BlockSpec((TOK_TILE, d), lambda i: (i, 0)),
                  pl.BlockSpec((1, d), lambda i: (0, 0)),
                  pl.BlockSpec((d, n), lambda i: (0, 0))],
        out_specs=pl.BlockSpec((TOK_TILE, n), lambda i: (i, 0)),
        out_shape=jax.ShapeDtypeStruct((t, n), out_dtype),
        compiler_params=_cparams(("parallel",)),
        name="norm_matmul",
    )(x, gain.reshape(1, d), w)


def _matmul_residual_kernel(a_ref, w_ref, h_ref, o_ref):
    o_ref[...] = h_ref[...] + jnp.dot(a_ref[...], w_ref[...], preferred_element_type=F32)


def matmul_residual(a, w, h):
    t, k = a.shape
    d = w.shape[1]
    return pl.pallas_call(
        _matmul_residual_kernel,
        grid=(t // TOK_TILE,),
        in_specs=[pl.BlockSpec((TOK_TILE, k), lambda i: (i, 0)),
                  pl.BlockSpec((k, d), lambda i: (0, 0)),
                  pl.BlockSpec((TOK_TILE, d), lambda i: (i, 0))],
        out_specs=pl.BlockSpec((TOK_TILE, d), lambda i: (i, 0)),
        out_shape=jax.ShapeDtypeStruct((t, d), F32),
        compiler_params=_cparams(("parallel",)),
        name="matmul_residual",
    )(a, w, h)


def _norm_transpose_kernel(x_ref, g_ref, o_ref):
    o_ref[...] = _rms(x_ref[...], g_ref[...]).T.astype(o_ref.dtype)


def norm_transpose(x, gain):
    t, d = x.shape
    return pl.pallas_call(
        _norm_transpose_kernel,
        grid=(t // TOK_TILE,),
        in_specs=[pl.BlockSpec((TOK_TILE, d), lambda i: (i, 0)),
                  pl.BlockSpec((1, d), lambda i: (0, 0))],
        out_specs=pl.BlockSpec((d, TOK_TILE), lambda i: (0, i)),
        out_shape=jax.ShapeDtypeStruct((d, t), BF16),
        compiler_params=_cparams(("parallel",)),
        name="norm_transpose",
    )(x, gain.reshape(1, d))


def _final_norm_kernel(x_ref, g_ref, o_ref):
    o_ref[...] = _rms(x_ref[...], g_ref[...])


def final_norm(x, gain):
    t, d = x.shape
    return pl.pallas_call(
        _final_norm_kernel,
        grid=(t // TOK_TILE,),
        in_specs=[pl.BlockSpec((TOK_TILE, d), lambda i: (i, 0)),
                  pl.BlockSpec((1, d), lambda i: (0, 0))],
        out_specs=pl.BlockSpec((TOK_TILE, d), lambda i: (i, 0)),
        out_shape=jax.ShapeDtypeStruct((t, d), F32),
        compiler_params=_cparams(("parallel",)),
        name="final_norm",
    )(x, gain.reshape(1, d))


def _attention_bias_tables(rel_bias, rows):
    n_h = rel_bias.shape[0]
    n_dr = 2 * MAX_WIN_ROWS - 1
    w = GRID_W
    edge = w - WIN_COLS
    line = jnp.pad(rel_bias, ((0, 0), (0, 0), (edge, edge + 1)), mode="edge")
    flat = jnp.tile(line, (1, 1, w))[:, :, w - 1:w - 1 + w * (2 * w - 1)]
    toep = flat.reshape(n_h, n_dr, w, 2 * w - 1)[:, :, :, :w]
    qc = np.arange(w)[:, None]
    kc = np.arange(w)[None, :]
    ws = np.clip(qc - WIN_COLS // 2, 0, w - WIN_COLS)
    toep = jnp.where(((kc >= ws) & (kc < ws + WIN_COLS))[None, None], toep, NEG_INF)
    pad = ATT_Q_ROWS - 1
    toep = jnp.pad(toep, ((0, 0), (pad, pad), (0, 0), (0, 0)), constant_values=NEG_INF)
    tabs = []
    for r0 in (0, ATT_Q_ROWS, rows - ATT_Q_ROWS):
        band = int(np.clip(r0 - MAX_WIN_ROWS // 2, 0, rows - ATT_BAND_ROWS))
        blocks = []
        for ql in range(ATT_Q_ROWS):
            lo = band - (r0 + ql) + MAX_WIN_ROWS - 1 + pad
            blk = toep[:, lo:lo + ATT_BAND_ROWS]
            blocks.append(blk.transpose(0, 2, 1, 3).reshape(n_h, w, ATT_BAND_ROWS * w))
        tab = jnp.stack(blocks, axis=1).reshape(n_h, ATT_Q_ROWS * w, ATT_BAND_ROWS * w)
        qr = r0 + np.arange(ATT_Q_ROWS)[:, None]
        kr = band + np.arange(ATT_BAND_ROWS)[None, :]
        rs = np.clip(qr - MAX_WIN_ROWS // 2, 0, rows - MAX_WIN_ROWS)
        row_ok = (kr >= rs) & (kr < rs + MAX_WIN_ROWS)
        row_ok = np.repeat(np.repeat(row_ok, w, axis=0), w, axis=1)
        tabs.append(jnp.where(row_ok[None], tab, NEG_INF))
    return jnp.stack(tabs)


def _attention_kernel(q_ref, k_ref, v_ref, b_ref, o_ref, *, rows):
    rb = pl.program_id(2)
    band = jnp.clip(rb * ATT_Q_ROWS - MAX_WIN_ROWS // 2, 0, rows - ATT_BAND_ROWS)
    start = pl.multiple_of(band * GRID_W, GRID_W)
    n_keys = ATT_BAND_ROWS * GRID_W
    kb = k_ref[pl.ds(start, n_keys), :]
    vb = v_ref[pl.ds(start, n_keys), :]
    q = q_ref[...]
    lane = lax.broadcasted_iota(jnp.int32, q.shape, 1)
    scale = 1.0 / math.sqrt(HEAD_DIM)
    outs = []
    for hh in range(ATT_LANES // HEAD_DIM):
        in_head = (lane < HEAD_DIM) if hh == 0 else (lane >= HEAD_DIM)
        qm = jnp.where(in_head, q, jnp.zeros_like(q)) * jnp.asarray(scale, q.dtype)
        s = lax.dot_general(qm, kb, (((1,), (1,)), ((), ())), preferred_element_type=F32)
        s = s + b_ref[0, hh]
        m = jnp.max(s, axis=-1, keepdims=True)
        p = jnp.exp(s - m)
        l = jnp.sum(p, axis=-1, keepdims=True)
        o = jnp.dot(p.astype(BF16), vb, preferred_element_type=F32)
        outs.append(o / l)
    o_ref[...] = jnp.where(lane < HEAD_DIM, outs[0], outs[1]).astype(o_ref.dtype)


def neighbourhood_attention_core(qkv, bias_tabs, batch, seq):
    t = qkv.shape[0]
    d = qkv.shape[1] // 3
    rows = seq // GRID_W
    n_rb = rows // ATT_Q_ROWS
    n_hp = d // ATT_LANES
    q_tok = ATT_Q_ROWS * GRID_W

    def bias_map(b, hp, rb):
        return (jnp.where(rb == 0, 0, jnp.where(rb == n_rb - 1, 2, 1)), hp, 0, 0)

    return pl.pallas_call(
        functools.partial(_attention_kernel, rows=rows),
        grid=(batch, n_hp, n_rb),
        in_specs=[pl.BlockSpec((q_tok, ATT_LANES), lambda b, hp, rb: (b * n_rb + rb, hp)),
                  pl.BlockSpec((seq, ATT_LANES), lambda b, hp, rb: (b, n_hp + hp)),
                  pl.BlockSpec((seq, ATT_LANES), lambda b, hp, rb: (b, 2 * n_hp + hp)),
                  pl.BlockSpec((1, 2, q_tok, ATT_BAND_ROWS * GRID_W), bias_map)],
        out_specs=pl.BlockSpec((q_tok, ATT_LANES), lambda b, hp, rb: (b * n_rb + rb, hp)),
        out_shape=jax.ShapeDtypeStruct((t, d), BF16),
        compiler_params=_cparams(("parallel", "parallel", "arbitrary")),
        name="natten_core",
    )(qkv, qkv, qkv, bias_tabs)


def _short_conv_kernel(b_ref, c_ref, v_ref, cp_ref, vp_ref, cn_ref, vn_ref, w_ref, o_ref, *, seq):
    i = pl.program_id(0)
    tm = b_ref.shape[0]
    u = c_ref[...] * v_ref[...]
    u_before = cp_ref[7:8, :] * vp_ref[7:8, :]
    u_after = cn_ref[0:1, :] * vn_ref[0:1, :]
    row = lax.broadcasted_iota(jnp.int32, u.shape, 0)
    pos = (i * tm + row) % seq
    u_prev = jnp.where(row == 0, u_before, pltpu.roll(u, 1, 0))
    u_prev = jnp.where(pos == 0, 0.0, u_prev)
    u_next = jnp.where(row == tm - 1, u_after, pltpu.roll(u, tm - 1, 0))
    u_next = jnp.where(pos == seq - 1, 0.0, u_next)
    conv = u_prev * w_ref[0:1, :] + u * w_ref[1:2, :] + u_next * w_ref[2:3, :]
    o_ref[...] = (b_ref[...] * conv).astype(o_ref.dtype)


def short_conv_gate(y, w_conv, seq):
    t = y.shape[0]
    d = y.shape[1] // 3
    cw = 512
    nj = d // cw
    tm = TOK_TILE
    n8 = t // 8

    def prev_map(off):
        return lambda i, j: (jnp.maximum(i * (tm // 8) - 1, 0), off * nj + j)

    def next_map(off):
        return lambda i, j: (jnp.minimum((i + 1) * (tm // 8), n8 - 1), off * nj + j)

    return pl.pallas_call(
        functools.partial(_short_conv_kernel, seq=seq),
        grid=(t // tm, nj),
        in_specs=[pl.BlockSpec((tm, cw), lambda i, j: (i, j)),
                  pl.BlockSpec((tm, cw), lambda i, j: (i, nj + j)),
                  pl.BlockSpec((tm, cw), lambda i, j: (i, 2 * nj + j)),
                  pl.BlockSpec((8, cw), prev_map(1)),
                  pl.BlockSpec((8, cw), prev_map(2)),
                  pl.BlockSpec((8, cw), next_map(1)),
                  pl.BlockSpec((8, cw), next_map(2)),
                  pl.BlockSpec((3, cw), lambda i, j: (0, j))],
        out_specs=pl.BlockSpec((tm, cw), lambda i, j: (i, j)),
        out_shape=jax.ShapeDtypeStruct((t, d), BF16),
        compiler_params=_cparams(("parallel", "parallel")),
        name="short_conv_gate",
    )(y, y, y, y, y, y, y, w_conv)


def _top16(sc, want_rank):
    n, l = sc.shape
    iota_n = lax.broadcasted_iota(jnp.int32, (n, l), 0).astype(F32)
    iota_k = lax.broadcasted_iota(jnp.int32, (PEER_TOPK, l), 0).astype(F32)

    def body(r, carry):
        sc, rank, sv, si = carry
        rf = jnp.asarray(r, jnp.int32).astype(F32)
        m = jnp.max(sc, axis=0, keepdims=True)
        idx = jnp.min(jnp.where(sc == m, iota_n, float(n)), axis=0, keepdims=True)
        sel = iota_n == idx
        if want_rank:
            rank = jnp.where(sel, rf, rank)
        sc = jnp.where(sel, -jnp.inf, sc)
        sv = jnp.where(iota_k == rf, m, sv)
        si = jnp.where(iota_k == rf, idx, si)
        return sc, rank, sv, si

    init = (sc, jnp.full((n, l), float(PEER_TOPK), F32),
            jnp.zeros((PEER_TOPK, l), F32), jnp.zeros((PEER_TOPK, l), F32))
    _, rank, sv, si = lax.fori_loop(0, PEER_TOPK, body, init)
    return sv, si, rank


def _merge_top16(sv1, sv2):
    k, l = sv1.shape
    iota_k = lax.broadcasted_iota(jnp.int32, (k, l), 0).astype(F32)
    top = sv1[0:1, :] + sv2[0:1, :]

    def body(_, carry):
        front, cnt, z = carry
        m = jnp.max(front, axis=0, keepdims=True)
        a_star = jnp.min(jnp.where(front == m, iota_k, float(k)), axis=0, keepdims=True)
        sel = iota_k == a_star
        z = z + jnp.exp(m - top)
        cnt = jnp.where(sel, cnt + 1.0, cnt)
        c_sel = jnp.sum(jnp.where(sel, cnt, 0.0), axis=0, keepdims=True)
        nxt = jnp.sum(jnp.where(iota_k == c_sel, sv2, 0.0), axis=0, keepdims=True)
        nxt = jnp.where(c_sel >= float(k), -jnp.inf, nxt)
        front = jnp.where(sel, sv1 + nxt, front)
        return front, cnt, z

    init = (sv1 + sv2[0:1, :], jnp.zeros((k, l), F32), jnp.zeros((1, l), F32))
    _, cnt, z = lax.fori_loop(0, PEER_TOPK, body, init)
    return cnt, z


def _tree(op, xs):
    while len(xs) > 1:
        xs = [op(xs[k], xs[k + 1]) for k in range(0, len(xs) - 1, 2)] + (xs[-1:] if len(xs) % 2 else [])
    return xs[0]


def _all_sublanes(op, x):
    for shift in (4, 2, 1):
        x = op(x, pltpu.roll(x, shift, 0))
    return x


def _peer_scores(h, keys_ref, q_scr, s_scr):
    for p in range(2):
        r0 = pl.multiple_of((2 * h + p) * PEER_HALF_DIM, PEER_HALF_DIM)
        s_scr[p] = jnp.dot(keys_ref[2 * h + p], q_scr[pl.ds(r0, PEER_HALF_DIM), :],
                           preferred_element_type=F32)


def _peer_head_exact(h, s_scr, r2_ref, e2_ref, n_ref, e1_ref):
    n_keys, lanes = PEER_N_KEYS, 128
    iota_n = lax.broadcasted_iota(jnp.int32, (n_keys, lanes), 0).astype(F32)
    for lt in range(s_scr.shape[2] // lanes):
        ls = slice(lt * lanes, (lt + 1) * lanes)
        s1 = s_scr[0, :, ls]
        s2 = s_scr[1, :, ls]
        sv1, si1, _ = _top16(s1, want_rank=False)
        sv2, _, rank2 = _top16(s2, want_rank=True)
        cnt, z = _merge_top16(sv1, sv2)
        n_sel = jnp.zeros((n_keys, lanes), F32)
        for a in range(PEER_TOPK):
            n_sel = jnp.where(iota_n == si1[a:a + 1, :], cnt[a:a + 1, :], n_sel)
        r2_ref[h, :, ls] = rank2.astype(r2_ref.dtype)
        e2_ref[h, :, ls] = jnp.exp(s2 - sv2[0:1, :]).astype(e2_ref.dtype)
        n_ref[h, :, ls] = n_sel
        e1_ref[h, :, ls] = jnp.exp(s1 - sv1[0:1, :]) * (1.0 / z)


def _peer_head_distinct(h, s_scr, t_scr, sv_scr, r2_ref, e2_ref, n_ref, e1_ref):
    n_keys, lanes, sub = PEER_N_KEYS, 128, 8
    n_vr = n_keys // sub
    tm = s_scr.shape[2]
    for p in range(2):
        prev = jnp.full((sub, tm), jnp.inf, F32)
        for r in range(PEER_TOPK):
            below = []
            for v in range(n_vr):
                x = s_scr[p, v * sub:(v + 1) * sub, :]
                below.append(jnp.where(x < prev, x, -jnp.inf))
            prev = _all_sublanes(jnp.maximum, _tree(jnp.maximum, below))
            t_scr[p, r] = prev
            sv_scr[p, r:r + 1, :] = prev[0:1, :]
    cnt, z = _merge_top16(sv_scr[0], sv_scr[1])
    inv_z = 1.0 / z
    one = jnp.ones((sub, lanes), F32)
    zero = jnp.zeros((sub, lanes), F32)
    bad = zero
    for lt in range(tm // lanes):
        ls = slice(lt * lanes, (lt + 1) * lanes)
        t2 = [t_scr[1, b, :, ls] for b in range(PEER_TOPK)]
        n_top = zero
        for v in range(n_vr):
            vs = slice(v * sub, (v + 1) * sub)
            x = s_scr[1, vs, ls]
            rank = _tree(jnp.add, [jnp.where(t > x, one, zero) for t in t2])
            r2_ref[h, vs, ls] = rank.astype(r2_ref.dtype)
            e2_ref[h, vs, ls] = jnp.exp(x - t2[0]).astype(e2_ref.dtype)
            n_top = n_top + jnp.where(rank < float(PEER_TOPK), one, zero)
        bad = jnp.maximum(bad, jnp.where(_all_sublanes(jnp.add, n_top) != float(PEER_TOPK), one, zero))
        t1 = [t_scr[0, a, :, ls] for a in range(PEER_TOPK)]
        cnt_b = [jnp.broadcast_to(cnt[a:a + 1, ls], (sub, lanes)) for a in range(PEER_TOPK)]
        inv_b = jnp.broadcast_to(inv_z[:, ls], (sub, lanes))
        n_top = zero
        for v in range(n_vr):
            vs = slice(v * sub, (v + 1) * sub)
            x = s_scr[0, vs, ls]
            n_sel = zero
            for a in range(PEER_TOPK):
                n_sel = jnp.where(x == t1[a], cnt_b[a], n_sel)
            n_ref[h, vs, ls] = n_sel
            e1_ref[h, vs, ls] = jnp.exp(x - t1[0]) * inv_b
            n_top = n_top + jnp.where(x >= t1[PEER_TOPK - 1], one, zero)
        bad = jnp.maximum(bad, jnp.where(_all_sublanes(jnp.add, n_top) != float(PEER_TOPK), one, zero))
    return bad


def _peer_weights_kernel(xt_ref, wq_ref, keys_ref, r2_ref, e2_ref, n_ref, e1_ref,
                         q_scr, s_scr, t_scr, sv_scr):
    q_scr[...] = jnp.dot(wq_ref[...], xt_ref[...], preferred_element_type=F32).astype(BF16)

    def head(h, _):
        _peer_scores(h, keys_ref, q_scr, s_scr)
        bad = _peer_head_distinct(h, s_scr, t_scr, sv_scr, r2_ref, e2_ref, n_ref, e1_ref)

        @pl.when(jnp.max(bad) > 0.0)
        def _():
            _peer_head_exact(h, s_scr, r2_ref, e2_ref, n_ref, e1_ref)

        return 0

    lax.fori_loop(0, PEER_HEADS, head, 0)


def peer_weights(xt, wq_t, keys):
    d, t = xt.shape
    tm = PEER_PRE_TILE
    out16 = jax.ShapeDtypeStruct((PEER_HEADS, PEER_N_KEYS, t), BF16)
    out32 = jax.ShapeDtypeStruct((PEER_HEADS, PEER_N_KEYS, t), F32)
    ospec = pl.BlockSpec((PEER_HEADS, PEER_N_KEYS, tm), lambda i: (0, 0, i))
    return pl.pallas_call(
        _peer_weights_kernel,
        grid=(t // tm,),
        in_specs=[pl.BlockSpec((d, tm), lambda i: (0, i)),
                  pl.BlockSpec(wq_t.shape, lambda i: (0, 0)),
                  pl.BlockSpec(keys.shape, lambda i: (0, 0, 0))],
        out_specs=[ospec, ospec, ospec, ospec],
        out_shape=[out16, out16, out32, out32],
        scratch_shapes=[pltpu.VMEM((wq_t.shape[0], tm), BF16),
                        pltpu.VMEM((2, PEER_N_KEYS, tm), F32),
                        pltpu.VMEM((2, PEER_TOPK, 8, tm), F32),
                        pltpu.VMEM((2, PEER_TOPK, tm), F32)],
        compiler_params=_cparams(("parallel",)),
        name="peer_weights",
    )(xt, wq_t, keys)


def _peer_hidden_rows(i, il, a_ref, hid_ref, r2_ref, e2_ref, n_ref, e1_ref):
    nk, lanes, pack = PEER_N_KEYS, 128, 16
    row0 = pl.multiple_of(il * nk, nk)
    n_rows = [n_ref[h, pl.ds(i, 1), :] for h in range(PEER_HEADS)]
    e1_rows = [e1_ref[h, pl.ds(i, 1), :] for h in range(PEER_HEADS)]
    zero = jnp.zeros((pack, lanes), BF16)
    for lt in range(a_ref.shape[1] // lanes):
        ls = slice(lt * lanes, (lt + 1) * lanes)
        n_b = [jnp.broadcast_to(r[:, ls], (pack, lanes)).astype(BF16) for r in n_rows]
        e1_b = [jnp.broadcast_to(r[:, ls], (pack, lanes)).astype(BF16) for r in e1_rows]
        for jb in range(nk // pack):
            js = slice(jb * pack, (jb + 1) * pack)
            g = None
            for h in range(PEER_HEADS):
                term = jnp.where(r2_ref[h, js, ls] < n_b[h], e2_ref[h, js, ls] * e1_b[h], zero)
                g = term if g is None else g + term
            rs = pl.ds(row0 + jb * pack, pack)
            a = a_ref[rs, ls]
            act = 0.5 * a * (1.0 + lax.erf(a * math.sqrt(0.5)))
            hid_ref[rs, ls] = act.astype(BF16) * g


def _peer_dense_kernel(xt_ref, r2_ref, e2_ref, n_ref, e1_ref, wup_ref, wdn_ref, h_ref, o_ref,
                       acc_ref, a_ref, hid_ref, r2_scr, e2_scr):
    e = pl.program_id(1)
    key_rows = wup_ref.shape[0] // PEER_N_KEYS

    @pl.when(e == 0)
    def _():
        acc_ref[...] = jnp.zeros_like(acc_ref)
        r2_scr[...] = r2_ref[...]
        e2_scr[...] = e2_ref[...]

    a_ref[...] = jnp.dot(wup_ref[...], xt_ref[...], preferred_element_type=F32)

    def key_row(il, _):
        _peer_hidden_rows(e * key_rows + il, il, a_ref, hid_ref, r2_scr, e2_scr, n_ref, e1_ref)
        return 0

    lax.fori_loop(0, key_rows, key_row, 0)
    acc_ref[...] += lax.dot_general(hid_ref[...], wdn_ref[...], (((0,), (0,)), ((), ())),
                                    preferred_element_type=F32)

    @pl.when(e == pl.num_programs(1) - 1)
    def _():
        o_ref[...] = h_ref[...] + acc_ref[...]


def peer_dense(xt, r2, e2, n_sel, e1n, w_up, w_down, h):
    d, t = xt.shape
    n_exp = w_up.shape[0]
    tm, te = PEER_TOK_TILE, PEER_EXP_TILE
    wspec = pl.BlockSpec((PEER_HEADS, PEER_N_KEYS, tm), lambda i, e: (0, 0, i))
    return pl.pallas_call(
        _peer_dense_kernel,
        grid=(t // tm, n_exp // te),
        in_specs=[pl.BlockSpec((d, tm), lambda i, e: (0, i)),
                  wspec, wspec, wspec, wspec,
                  pl.BlockSpec((te, d), lambda i, e: (e, 0)),
                  pl.BlockSpec((te, d), lambda i, e: (e, 0)),
                  pl.BlockSpec((tm, d), lambda i, e: (i, 0))],
        out_specs=pl.BlockSpec((tm, d), lambda i, e: (i, 0)),
        out_shape=jax.ShapeDtypeStruct((t, d), F32),
        scratch_shapes=[pltpu.VMEM((tm, d), F32),
                        pltpu.VMEM((te, tm), F32),
                        pltpu.VMEM((te, tm), BF16),
                        pltpu.VMEM((PEER_HEADS, PEER_N_KEYS, tm), BF16),
                        pltpu.VMEM((PEER_HEADS, PEER_N_KEYS, tm), BF16)],
        compiler_params=_cparams(("parallel", "arbitrary")),
        name="peer_dense",
    )(xt, r2, e2, n_sel, e1n, w_up, w_down, h)


def kernel(x, norm_mix, norm_ffn, norm_final, attn_w_qkv, attn_w_o, attn_rel_bias,
           conv_w_in, conv_w_conv, conv_w_out, peer_w_query, peer_sub_keys,
           peer_w_up, peer_w_down):
    batch, seq, d = x.shape
    depth = norm_mix.shape[0]
    rows = seq // GRID_W
    h = x.reshape(batch * seq, d)
    for i in range(depth):
        j = i // 2
        if i % 2 == 0:
            qkv = norm_matmul(h, norm_mix[i], attn_w_qkv[j].astype(BF16), BF16)
            bias_tabs = _attention_bias_tables(attn_rel_bias[j], rows)
            mixed = neighbourhood_attention_core(qkv, bias_tabs, batch, seq)
            h = matmul_residual(mixed, attn_w_o[j].astype(BF16), h)
        else:
            y = norm_matmul(h, norm_mix[i], conv_w_in[j].astype(BF16), F32)
            mixed = short_conv_gate(y, conv_w_conv[j], seq)
            h = matmul_residual(mixed, conv_w_out[j].astype(BF16), h)
        xt = norm_transpose(h, norm_ffn[i])
        keys = peer_sub_keys[i].reshape(PEER_HEADS * 2, PEER_N_KEYS, PEER_HALF_DIM).astype(BF16)
        r2, e2, n_sel, e1n = peer_weights(xt, peer_w_query[i].T.astype(BF16), keys)
        h = peer_dense(xt, r2, e2, n_sel, e1n, peer_w_up[i].astype(BF16),
                       peer_w_down[i].astype(BF16), h)
    return final_norm(h, norm_final).reshape(batch, seq, d)
```

```python
import functools
import math

import numpy as np
import jax
import jax.numpy as jnp
from jax import lax
from jax.experimental import pallas as pl
from jax.experimental.pallas import tpu as pltpu

F32 = jnp.float32
BF16 = jnp.bfloat16

RMS_EPS = 1e-6
NEG_INF = -1e30

GRID_W = 64
N_HEADS = 16
HEAD_DIM = 64
MAX_WIN_ROWS = 8
WIN_COLS = 16
ATT_Q_ROWS = 4
ATT_BAND_ROWS = ATT_Q_ROWS + MAX_WIN_ROWS - 1
ATT_LANES = 128
ATT_BLOCKS_PER_STEP = 2

PEER_HEADS = 8
PEER_N_KEYS = 128
PEER_TOPK = 16
PEER_HALF_DIM = 128

TOK_TILE = 512
PEER_PRE_TILE = 512
PEER_TOK_TILE = 512
PEER_EXP_TILE = 2048
PEER_STAGE_ROWS = 256
VMEM_LIMIT = 56 * 1024 * 1024


def _cparams(sem, flags=None):
    return pltpu.CompilerParams(dimension_semantics=sem, vmem_limit_bytes=VMEM_LIMIT, flags=flags)


def _rms(x, g):
    return x * lax.rsqrt(jnp.mean(x * x, axis=-1, keepdims=True) + RMS_EPS) * g


def _norm_matmul_kernel(x_ref, g_ref, w_ref, o_ref):
    y = _rms(x_ref[...], g_ref[...])
    o_ref[...] = jnp.dot(y.astype(BF16), w_ref[...],
                         preferred_element_type=F32).astype(o_ref.dtype)


def norm_matmul(x, gain, w, out_dtype):
    t, d = x.shape
    n = w.shape[1]
    return pl.pallas_call(
        _norm_matmul_kernel,
        grid=(t // TOK_TILE,),
        in_specs=[pl.BlockSpec((TOK_TILE, d), lambda i: (i, 0)),
                  pl.BlockSpec((1, d), lambda i: (0, 0)),
                  pl.BlockSpec((d, n), lambda i: (0, 0))],
        out_specs=pl.BlockSpec((TOK_TILE, n), lambda i: (i, 0)),
        out_shape=jax.ShapeDtypeStruct((t, n), out_dtype),
        compiler_params=_cparams(("parallel",)),
        name="norm_matmul",
    )(x, gain.reshape(1, d), w)


def _matmul_residual_kernel(a_ref, w_ref, h_ref, o_ref):
    o_ref[...] = h_ref[...] + jnp.dot(a_ref[...], w_ref[...], preferred_element_type=F32)


def matmul_residual(a, w, h):
    t, k = a.shape
    d = w.shape[1]
    return pl.pallas_call(
        _matmul_residual_kernel,
        grid=(t // TOK_TILE,),
        in_specs=[pl.BlockSpec((TOK_TILE, k), lambda i: (i, 0)),
                  pl.BlockSpec((k, d), lambda i: (0, 0)),
                  pl.BlockSpec((TOK_TILE, d), lambda i: (i, 0))],
        out_specs=pl.BlockSpec((TOK_TILE, d), lambda i: (i, 0)),
        out_shape=jax.ShapeDtypeStruct((t, d), F32),
        compiler_params=_cparams(("parallel",)),
        name="matmul_residual",
    )(a, w, h)


def _norm_transpose_kernel(x_ref, g_ref, o_ref):
    o_ref[...] = _rms(x_ref[...], g_ref[...]).T.astype(o_ref.dtype)


def norm_transpose(x, gain):
    t, d = x.shape
    return pl.pallas_call(
        _norm_transpose_kernel,
        grid=(t // TOK_TILE,),
        in_specs=[pl.BlockSpec((TOK_TILE, d), lambda i: (i, 0)),
                  pl.BlockSpec((1, d), lambda i: (0, 0))],
        out_specs=pl.BlockSpec((d, TOK_TILE), lambda i: (0, i)),
        out_shape=jax.ShapeDtypeStruct((d, t), BF16),
        compiler_params=_cparams(("parallel",)),
        name="norm_transpose",
    )(x, gain.reshape(1, d))


def _final_norm_kernel(x_ref, g_ref, o_ref):
    o_ref[...] = _rms(x_ref[...], g_ref[...])


def final_norm(x, gain):
    t, d = x.shape
    return pl.pallas_call(
        _final_norm_kernel,
        grid=(t // TOK_TILE,),
        in_specs=[pl.BlockSpec((TOK_TILE, d), lambda i: (i, 0)),
                  pl.BlockSpec((1, d), lambda i: (0, 0))],
        out_specs=pl.BlockSpec((TOK_TILE, d), lambda i: (i, 0)),
        out_shape=jax.ShapeDtypeStruct((t, d), F32),
        compiler_params=_cparams(("parallel",)),
        name="final_norm",
    )(x, gain.reshape(1, d))


def _attention_bias_tables(rel_bias, rows):
    n_h = rel_bias.shape[0]
    n_dr = 2 * MAX_WIN_ROWS - 1
    w = GRID_W
    edge = w - WIN_COLS
    line = jnp.pad(rel_bias, ((0, 0), (0, 0), (edge, edge + 1)), mode="edge")
    flat = jnp.tile(line, (1, 1, w))[:, :, w - 1:w - 1 + w * (2 * w - 1)]
    toep = flat.reshape(n_h, n_dr, w, 2 * w - 1)[:, :, :, :w]
    qc = np.arange(w)[:, None]
    kc = np.arange(w)[None, :]
    ws = np.clip(qc - WIN_COLS // 2, 0, w - WIN_COLS)
    toep = jnp.where(((kc >= ws) & (kc < ws + WIN_COLS))[None, None], toep, NEG_INF)
    pad = ATT_Q_ROWS - 1
    toep = jnp.pad(toep, ((0, 0), (pad, pad), (0, 0), (0, 0)), constant_values=NEG_INF)
    tabs = []
    for r0 in (0, ATT_Q_ROWS, rows - ATT_Q_ROWS):
        band = int(np.clip(r0 - MAX_WIN_ROWS // 2, 0, rows - ATT_BAND_ROWS))
        blocks = []
        for ql in range(ATT_Q_ROWS):
            lo = band - (r0 + ql) + MAX_WIN_ROWS - 1 + pad
            blk = toep[:, lo:lo + ATT_BAND_ROWS]
            blocks.append(blk.transpose(0, 2, 1, 3).reshape(n_h, w, ATT_BAND_ROWS * w))
        tab = jnp.stack(blocks, axis=1).reshape(n_h, ATT_Q_ROWS * w, ATT_BAND_ROWS * w)
        qr = r0 + np.arange(ATT_Q_ROWS)[:, None]
        kr = band + np.arange(ATT_BAND_ROWS)[None, :]
        rs = np.clip(qr - MAX_WIN_ROWS // 2, 0, rows - MAX_WIN_ROWS)
        row_ok = (kr >= rs) & (kr < rs + MAX_WIN_ROWS)
        row_ok = np.repeat(np.repeat(row_ok, w, axis=0), w, axis=1)
        tabs.append(jnp.where(row_ok[None], tab, NEG_INF))
    return jnp.stack(tabs)


def _attention_kernel(q_ref, k_ref, v_ref, *rest, rows):
    bias_refs, o_ref = rest[:ATT_BLOCKS_PER_STEP], rest[ATT_BLOCKS_PER_STEP]
    q_tok = ATT_Q_ROWS * GRID_W
    n_keys = ATT_BAND_ROWS * GRID_W
    scale = 1.0 / math.sqrt(HEAD_DIM)
    lane = lax.broadcasted_iota(jnp.int32, (q_tok, ATT_LANES), 1)
    for sub in range(ATT_BLOCKS_PER_STEP):
        rb = pl.program_id(2) * ATT_BLOCKS_PER_STEP + sub
        band = jnp.clip(rb * ATT_Q_ROWS - MAX_WIN_ROWS // 2, 0, rows - ATT_BAND_ROWS)
        start = pl.multiple_of(band * GRID_W, GRID_W)
        kb = k_ref[pl.ds(start, n_keys), :]
        vb = v_ref[pl.ds(start, n_keys), :]
        q = q_ref[sub * q_tok:(sub + 1) * q_tok, :]
        outs = []
        for hh in range(ATT_LANES // HEAD_DIM):
            in_head = (lane < HEAD_DIM) if hh == 0 else (lane >= HEAD_DIM)
            qm = jnp.where(in_head, q, jnp.zeros_like(q)) * jnp.asarray(scale, q.dtype)
            s = lax.dot_general(qm, kb, (((1,), (1,)), ((), ())), preferred_element_type=F32)
            s = s + bias_refs[sub][0, hh]
            m = jnp.max(s, axis=-1, keepdims=True)
            p = jnp.exp(s - m)
            l = jnp.sum(p, axis=-1, keepdims=True)
            o = jnp.dot(p.astype(BF16), vb, preferred_element_type=F32)
            outs.append(o / l)
        o_ref[sub * q_tok:(sub + 1) * q_tok, :] = jnp.where(lane < HEAD_DIM, outs[0],
                                                             outs[1]).astype(o_ref.dtype)


def neighbourhood_attention_core(qkv, bias_tabs, batch, seq):
    t = qkv.shape[0]
    d = qkv.shape[1] // 3
    rows = seq // GRID_W
    n_rb = rows // ATT_Q_ROWS
    n_steps = n_rb // ATT_BLOCKS_PER_STEP
    n_hp = d // ATT_LANES
    q_tok = ATT_Q_ROWS * GRID_W * ATT_BLOCKS_PER_STEP

    def bias_spec(sub):
        def index_map(b, hp, st):
            rb = st * ATT_BLOCKS_PER_STEP + sub
            return (jnp.where(rb == 0, 0, jnp.where(rb == n_rb - 1, 2, 1)), hp, 0, 0)
        return pl.BlockSpec((1, 2, ATT_Q_ROWS * GRID_W, ATT_BAND_ROWS * GRID_W), index_map)

    return pl.pallas_call(
        functools.partial(_attention_kernel, rows=rows),
        grid=(batch, n_hp, n_steps),
        in_specs=[pl.BlockSpec((q_tok, ATT_LANES), lambda b, hp, st: (b * n_steps + st, hp)),
                  pl.BlockSpec((seq, ATT_LANES), lambda b, hp, st: (b, n_hp + hp)),
                  pl.BlockSpec((seq, ATT_LANES), lambda b, hp, st: (b, 2 * n_hp + hp))]
                 + [bias_spec(sub) for sub in range(ATT_BLOCKS_PER_STEP)],
        out_specs=pl.BlockSpec((q_tok, ATT_LANES), lambda b, hp, st: (b * n_steps + st, hp)),
        out_shape=jax.ShapeDtypeStruct((t, d), BF16),
        compiler_params=_cparams(("parallel", "parallel", "arbitrary")),
        name="natten_core",
    )(qkv, qkv, qkv, *([bias_tabs] * ATT_BLOCKS_PER_STEP))


def _short_conv_kernel(b_ref, c_ref, v_ref, cp_ref, vp_ref, cn_ref, vn_ref, w_ref, o_ref, *, seq):
    i = pl.program_id(0)
    tm = b_ref.shape[0]
    u = c_ref[...] * v_ref[...]
    u_before = cp_ref[7:8, :] * vp_ref[7:8, :]
    u_after = cn_ref[0:1, :] * vn_ref[0:1, :]
    row = lax.broadcasted_iota(jnp.int32, u.shape, 0)
    pos = (i * tm + row) % seq
    u_prev = jnp.where(row == 0, u_before, pltpu.roll(u, 1, 0))
    u_prev = jnp.where(pos == 0, 0.0, u_prev)
    u_next = jnp.where(row == tm - 1, u_after, pltpu.roll(u, tm - 1, 0))
    u_next = jnp.where(pos == seq - 1, 0.0, u_next)
    conv = u_prev * w_ref[0:1, :] + u * w_ref[1:2, :] + u_next * w_ref[2:3, :]
    o_ref[...] = (b_ref[...] * conv).astype(o_ref.dtype)


def short_conv_gate(y, w_conv, seq):
    t = y.shape[0]
    d = y.shape[1] // 3
    cw = 512
    nj = d // cw
    tm = TOK_TILE
    n8 = t // 8

    def prev_map(off):
        return lambda i, j: (jnp.maximum(i * (tm // 8) - 1, 0), off * nj + j)

    def next_map(off):
        return lambda i, j: (jnp.minimum((i + 1) * (tm // 8), n8 - 1), off * nj + j)

    return pl.pallas_call(
        functools.partial(_short_conv_kernel, seq=seq),
        grid=(t // tm, nj),
        in_specs=[pl.BlockSpec((tm, cw), lambda i, j: (i, j)),
                  pl.BlockSpec((tm, cw), lambda i, j: (i, nj + j)),
                  pl.BlockSpec((tm, cw), lambda i, j: (i, 2 * nj + j)),
                  pl.BlockSpec((8, cw), prev_map(1)),
                  pl.BlockSpec((8, cw), prev_map(2)),
                  pl.BlockSpec((8, cw), next_map(1)),
                  pl.BlockSpec((8, cw), next_map(2)),
                  pl.BlockSpec((3, cw), lambda i, j: (0, j))],
        out_specs=pl.BlockSpec((tm, cw), lambda i, j: (i, j)),
        out_shape=jax.ShapeDtypeStruct((t, d), BF16),
        compiler_params=_cparams(("parallel", "parallel")),
        name="short_conv_gate",
    )(y, y, y, y, y, y, y, w_conv)


def _top16(sc, want_rank):
    n, l = sc.shape
    iota_n = lax.broadcasted_iota(jnp.int32, (n, l), 0).astype(F32)
    iota_k = lax.broadcasted_iota(jnp.int32, (PEER_TOPK, l), 0).astype(F32)

    def body(r, carry):
        sc, rank, sv, si = carry
        rf = jnp.asarray(r, jnp.int32).astype(F32)
        m = jnp.max(sc, axis=0, keepdims=True)
        idx = jnp.min(jnp.where(sc == m, iota_n, float(n)), axis=0, keepdims=True)
        sel = iota_n == idx
        if want_rank:
            rank = jnp.where(sel, rf, rank)
        sc = jnp.where(sel, -jnp.inf, sc)
        sv = jnp.where(iota_k == rf, m, sv)
        si = jnp.where(iota_k == rf, idx, si)
        return sc, rank, sv, si

    init = (sc, jnp.full((n, l), float(PEER_TOPK), F32),
            jnp.zeros((PEER_TOPK, l), F32), jnp.zeros((PEER_TOPK, l), F32))
    _, rank, sv, si = lax.fori_loop(0, PEER_TOPK, body, init)
    return sv, si, rank


def _merge_top16(sv1, sv2):
    k, l = sv1.shape
    iota_k = lax.broadcasted_iota(jnp.int32, (k, l), 0).astype(F32)
    top = sv1[0:1, :] + sv2[0:1, :]

    def body(_, carry):
        front, cnt, z = carry
        m = jnp.max(front, axis=0, keepdims=True)
        a_star = jnp.min(jnp.where(front == m, iota_k, float(k)), axis=0, keepdims=True)
        sel = iota_k == a_star
        z = z + jnp.exp(m - top)
        cnt = jnp.where(sel, cnt + 1.0, cnt)
        c_sel = jnp.sum(jnp.where(sel, cnt, 0.0), axis=0, keepdims=True)
        nxt = jnp.sum(jnp.where(iota_k == c_sel, sv2, 0.0), axis=0, keepdims=True)
        nxt = jnp.where(c_sel >= float(k), -jnp.inf, nxt)
        front = jnp.where(sel, sv1 + nxt, front)
        return front, cnt, z

    init = (sv1 + sv2[0:1, :], jnp.zeros((k, l), F32), jnp.zeros((1, l), F32))
    _, cnt, z = lax.fori_loop(0, PEER_TOPK, body, init)
    return cnt, z


def _tree(op, xs):
    while len(xs) > 1:
        xs = [op(xs[k], xs[k + 1]) for k in range(0, len(xs) - 1, 2)] + (xs[-1:] if len(xs) % 2 else [])
    return xs[0]


def _all_sublanes(op, x):
    for shift in (4, 2, 1):
        x = op(x, pltpu.roll(x, shift, 0))
    return x


def _peer_scores(h, keys_ref, q_scr, s_scr):
    for p in range(2):
        r0 = pl.multiple_of((2 * h + p) * PEER_HALF_DIM, PEER_HALF_DIM)
        s_scr[p] = jnp.dot(keys_ref[2 * h + p], q_scr[pl.ds(r0, PEER_HALF_DIM), :],
                           preferred_element_type=F32)


def _peer_head_exact(h, s_scr, r2_ref, e2_ref, n_ref, e1_ref):
    n_keys, lanes = PEER_N_KEYS, 128
    iota_n = lax.broadcasted_iota(jnp.int32, (n_keys, lanes), 0).astype(F32)
    for lt in range(s_scr.shape[2] // lanes):
        ls = slice(lt * lanes, (lt + 1) * lanes)
        s1 = s_scr[0, :, ls]
        s2 = s_scr[1, :, ls]
        sv1, si1, _ = _top16(s1, want_rank=False)
        sv2, _, rank2 = _top16(s2, want_rank=True)
        cnt, z = _merge_top16(sv1, sv2)
        n_sel = jnp.zeros((n_keys, lanes), F32)
        for a in range(PEER_TOPK):
            n_sel = jnp.where(iota_n == si1[a:a + 1, :], cnt[a:a + 1, :], n_sel)
        r2_ref[h, :, ls] = rank2.astype(r2_ref.dtype)
        e2_ref[h, :, ls] = jnp.exp(s2 - sv2[0:1, :]).astype(e2_ref.dtype)
        n_ref[h, :, ls] = n_sel
        e1_ref[h, :, ls] = jnp.exp(s1 - sv1[0:1, :]) * (1.0 / z)


def _peer_head_distinct(h, s_scr, t_scr, sv_scr, r2_ref, e2_ref, n_ref, e1_ref):
    n_keys, lanes, sub = PEER_N_KEYS, 128, 8
    n_vr = n_keys // sub
    tm = s_scr.shape[2]
    for p in range(2):
        prev = jnp.full((sub, tm), jnp.inf, F32)
        for r in range(PEER_TOPK):
            below = []
            for v in range(n_vr):
                x = s_scr[p, v * sub:(v + 1) * sub, :]
                below.append(jnp.where(x < prev, x, -jnp.inf))
            prev = _all_sublanes(jnp.maximum, _tree(jnp.maximum, below))
            t_scr[p, r] = prev
            sv_scr[p, r:r + 1, :] = prev[0:1, :]
    cnt, z = _merge_top16(sv_scr[0], sv_scr[1])
    inv_z = 1.0 / z
    one = jnp.ones((sub, lanes), F32)
    zero = jnp.zeros((sub, lanes), F32)
    bad = zero
    for lt in range(tm // lanes):
        ls = slice(lt * lanes, (lt + 1) * lanes)
        t2 = [t_scr[1, b, :, ls] for b in range(PEER_TOPK)]
        n_top = zero
        for v in range(n_vr):
            vs = slice(v * sub, (v + 1) * sub)
            x = s_scr[1, vs, ls]
            rank = _tree(jnp.add, [jnp.where(t > x, one, zero) for t in t2])
            r2_ref[h, vs, ls] = rank.astype(r2_ref.dtype)
            e2_ref[h, vs, ls] = jnp.exp(x - t2[0]).astype(e2_ref.dtype)
            n_top = n_top + jnp.where(rank < float(PEER_TOPK), one, zero)
        bad = jnp.maximum(bad, jnp.where(_all_sublanes(jnp.add, n_top) != float(PEER_TOPK), one, zero))
        t1 = [t_scr[0, a, :, ls] for a in range(PEER_TOPK)]
        cnt_b = [jnp.broadcast_to(cnt[a:a + 1, ls], (sub, lanes)) for a in range(PEER_TOPK)]
        inv_b = jnp.broadcast_to(inv_z[:, ls], (sub, lanes))
        n_top = zero
        for v in range(n_vr):
            vs = slice(v * sub, (v + 1) * sub)
            x = s_scr[0, vs, ls]
            n_sel = zero
            for a in range(PEER_TOPK):
                n_sel = jnp.where(x == t1[a], cnt_b[a], n_sel)
            n_ref[h, vs, ls] = n_sel
            e1_ref[h, vs, ls] = jnp.exp(x - t1[0]) * inv_b
            n_top = n_top + jnp.where(x >= t1[PEER_TOPK - 1], one, zero)
        bad = jnp.maximum(bad, jnp.where(_all_sublanes(jnp.add, n_top) != float(PEER_TOPK), one, zero))
    return bad


def _peer_weights_kernel(xt_ref, wq_ref, keys_ref, r2_ref, e2_ref, n_ref, e1_ref,
                         q_scr, s_scr, t_scr, sv_scr):
    q_scr[...] = jnp.dot(wq_ref[...], xt_ref[...], preferred_element_type=F32).astype(BF16)

    def head(h, _):
        _peer_scores(h, keys_ref, q_scr, s_scr)
        bad = _peer_head_distinct(h, s_scr, t_scr, sv_scr, r2_ref, e2_ref, n_ref, e1_ref)

        @pl.when(jnp.max(bad) > 0.0)
        def _():
            _peer_head_exact(h, s_scr, r2_ref, e2_ref, n_ref, e1_ref)

        return 0

    lax.fori_loop(0, PEER_HEADS, head, 0)


def peer_weights(xt, wq_t, keys):
    d, t = xt.shape
    tm = PEER_PRE_TILE
    out16 = jax.ShapeDtypeStruct((PEER_HEADS, PEER_N_KEYS, t), BF16)
    out32 = jax.ShapeDtypeStruct((PEER_HEADS, PEER_N_KEYS, t), F32)
    ospec = pl.BlockSpec((PEER_HEADS, PEER_N_KEYS, tm), lambda i: (0, 0, i))
    return pl.pallas_call(
        _peer_weights_kernel,
        grid=(t // tm,),
        in_specs=[pl.BlockSpec((d, tm), lambda i: (0, i)),
                  pl.BlockSpec(wq_t.shape, lambda i: (0, 0)),
                  pl.BlockSpec(keys.shape, lambda i: (0, 0, 0))],
        out_specs=[ospec, ospec, ospec, ospec],
        out_shape=[out16, out16, out32, out32],
        scratch_shapes=[pltpu.VMEM((wq_t.shape[0], tm), BF16),
                        pltpu.VMEM((2, PEER_N_KEYS, tm), F32),
                        pltpu.VMEM((2, PEER_TOPK, 8, tm), F32),
                        pltpu.VMEM((2, PEER_TOPK, tm), F32)],
        compiler_params=_cparams(("parallel",)),
        name="peer_weights",
    )(xt, wq_t, keys)


def _peer_hidden_rows(i, il, a_ref, hid_ref, r2_ref, e2_ref, n_ref, e1_ref):
    nk, lanes, pack = PEER_N_KEYS, 128, 16
    row0 = il * nk
    n_rows = [n_ref[h, pl.ds(i, 1), :] for h in range(PEER_HEADS)]
    e1_rows = [e1_ref[h, pl.ds(i, 1), :] for h in range(PEER_HEADS)]
    zero = jnp.zeros((pack, lanes), BF16)
    for lt in range(a_ref.shape[1] // lanes):
        ls = slice(lt * lanes, (lt + 1) * lanes)
        n_b = [jnp.broadcast_to(r[:, ls], (pack, lanes)).astype(BF16) for r in n_rows]
        e1_b = [jnp.broadcast_to(r[:, ls], (pack, lanes)).astype(BF16) for r in e1_rows]
        for jb in range(nk // pack):
            js = slice(jb * pack, (jb + 1) * pack)
            g = None
            for h in range(PEER_HEADS):
                term = jnp.where(r2_ref[h, js, ls] < n_b[h], e2_ref[h, js, ls] * e1_b[h], zero)
                g = term if g is None else g + term
            rs = slice(row0 + jb * pack, row0 + (jb + 1) * pack)
            a = a_ref[rs, ls]
            act = 0.5 * a * (1.0 + lax.erf(a * math.sqrt(0.5)))
            hid_ref[rs, ls] = act.astype(BF16) * g


def _peer_dense_kernel(xt_ref, r2_ref, e2_ref, n_ref, e1_ref, wup_ref, wdn_ref, h_ref, o_ref,
                       acc_ref, a0_ref, a1_ref, hid0_ref, hid1_ref, r2_scr, e2_scr):
    e = pl.program_id(1)
    grp = a0_ref.shape[0]
    rows_per_grp = grp // PEER_N_KEYS
    n_grp = wup_ref.shape[0] // grp

    @pl.when(e == 0)
    def _():
        acc_ref[...] = jnp.zeros_like(acc_ref)
        r2_scr[...] = r2_ref[...]
        e2_scr[...] = e2_ref[...]

    def up(g, a_ref):
        r0 = pl.multiple_of(g * grp, grp)
        a_ref[...] = jnp.dot(wup_ref[pl.ds(r0, grp), :], xt_ref[...], preferred_element_type=F32)

    def hidden(g, a_ref, hid_ref):
        for il in range(rows_per_grp):
            _peer_hidden_rows((e * n_grp + g) * rows_per_grp + il, il, a_ref, hid_ref,
                              r2_scr, e2_scr, n_ref, e1_ref)

    def down(g, hid_ref):
        r0 = pl.multiple_of(g * grp, grp)
        acc_ref[...] += lax.dot_general(hid_ref[...], wdn_ref[pl.ds(r0, grp), :],
                                        (((0,), (0,)), ((), ())), preferred_element_type=F32)

    up(0, a0_ref)
    up(1, a1_ref)
    hidden(0, a0_ref, hid0_ref)

    def two_groups(k, _):
        g = 2 * k
        down(g, hid0_ref)
        hidden(g + 1, a1_ref, hid1_ref)
        up(g + 2, a0_ref)
        down(g + 1, hid1_ref)
        hidden(g + 2, a0_ref, hid0_ref)
        up(g + 3, a1_ref)
        return 0

    lax.fori_loop(0, n_grp // 2 - 1, two_groups, 0)
    down(n_grp - 2, hid0_ref)
    hidden(n_grp - 1, a1_ref, hid1_ref)
    down(n_grp - 1, hid1_ref)

    @pl.when(e == pl.num_programs(1) - 1)
    def _():
        o_ref[...] = h_ref[...] + acc_ref[...]


def peer_dense(xt, r2, e2, n_sel, e1n, w_up, w_down, h):
    d, t = xt.shape
    n_exp = w_up.shape[0]
    tm, te = PEER_TOK_TILE, PEER_EXP_TILE
    wspec = pl.BlockSpec((PEER_HEADS, PEER_N_KEYS, tm), lambda i, e: (0, 0, i))
    return pl.pallas_call(
        _peer_dense_kernel,
        grid=(t // tm, n_exp // te),
        in_specs=[pl.BlockSpec((d, tm), lambda i, e: (0, i)),
                  wspec, wspec, wspec, wspec,
                  pl.BlockSpec((te, d), lambda i, e: (e, 0)),
                  pl.BlockSpec((te, d), lambda i, e: (e, 0)),
                  pl.BlockSpec((tm, d), lambda i, e: (i, 0))],
        out_specs=pl.BlockSpec((tm, d), lambda i, e: (i, 0)),
        out_shape=jax.ShapeDtypeStruct((t, d), F32),
        scratch_shapes=[pltpu.VMEM((tm, d), F32),
                        pltpu.VMEM((PEER_STAGE_ROWS, tm), F32),
                        pltpu.VMEM((PEER_STAGE_ROWS, tm), F32),
                        pltpu.VMEM((PEER_STAGE_ROWS, tm), BF16),
                        pltpu.VMEM((PEER_STAGE_ROWS, tm), BF16),
                        pltpu.VMEM((PEER_HEADS, PEER_N_KEYS, tm), BF16),
                        pltpu.VMEM((PEER_HEADS, PEER_N_KEYS, tm), BF16)],
        compiler_params=_cparams(("parallel", "arbitrary")),
        name="peer_dense",
    )(xt, r2, e2, n_sel, e1n, w_up, w_down, h)


def kernel(x, norm_mix, norm_ffn, norm_final, attn_w_qkv, attn_w_o, attn_rel_bias,
           conv_w_in, conv_w_conv, conv_w_out, peer_w_query, peer_sub_keys,
           peer_w_up, peer_w_down):
    batch, seq, d = x.shape
    depth = norm_mix.shape[0]
    rows = seq // GRID_W
    h = x.reshape(batch * seq, d)
    for i in range(depth):
        j = i // 2
        if i % 2 == 0:
            qkv = norm_matmul(h, norm_mix[i], attn_w_qkv[j].astype(BF16), BF16)
            bias_tabs = _attention_bias_tables(attn_rel_bias[j], rows)
            mixed = neighbourhood_attention_core(qkv, bias_tabs, batch, seq)
            h = matmul_residual(mixed, attn_w_o[j].astype(BF16), h)
        else:
            y = norm_matmul(h, norm_mix[i], conv_w_in[j].astype(BF16), F32)
            mixed = short_conv_gate(y, conv_w_conv[j], seq)
            h = matmul_residual(mixed, conv_w_out[j].astype(BF16), h)
        xt = norm_transpose(h, norm_ffn[i])
        keys = peer_sub_keys[i].reshape(PEER_HEADS * 2, PEER_N_KEYS, PEER_HALF_DIM).astype(BF16)
        r2, e2, n_sel, e1n = peer_weights(xt, peer_w_query[i].T.astype(BF16), keys)
        h = peer_dense(xt, r2, e2, n_sel, e1n, peer_w_up[i].astype(BF16),
                       peer_w_down[i].astype(BF16), h)
    return final_norm(h, norm_final).reshape(batch, seq, d)
```

```python
import functools
import math

import numpy as np
import jax
import jax.numpy as jnp
from jax import lax
from jax.experimental import pallas as pl
from jax.experimental.pallas import tpu as pltpu

F32 = jnp.float32
BF16 = jnp.bfloat16

RMS_EPS = 1e-6
NEG_INF = -1e30

GRID_W = 64
N_HEADS = 16
HEAD_DIM = 64
MAX_WIN_ROWS = 8
WIN_COLS = 16
ATT_Q_ROWS = 4
ATT_BAND_ROWS = ATT_Q_ROWS + MAX_WIN_ROWS - 1
ATT_LANES = 128
ATT_BLOCKS_PER_STEP = 4

PEER_HEADS = 8
PEER_N_KEYS = 128
PEER_TOPK = 16
PEER_HALF_DIM = 128

TOK_TILE = 512
PEER_PRE_TILE = 512
PEER_TOK_TILE = 512
PEER_EXP_TILE = 2048
PEER_STAGE_ROWS = 256
VMEM_LIMIT = 56 * 1024 * 1024


def _cparams(sem, flags=None):
    return pltpu.CompilerParams(dimension_semantics=sem, vmem_limit_bytes=VMEM_LIMIT, flags=flags)


def _rms(x, g):
    return x * lax.rsqrt(jnp.mean(x * x, axis=-1, keepdims=True) + RMS_EPS) * g


def _norm_matmul_kernel(x_ref, g_ref, w_ref, o_ref):
    y = _rms(x_ref[...], g_ref[...])
    o_ref[...] = jnp.dot(y.astype(BF16), w_ref[...],
                         preferred_element_type=F32).astype(o_ref.dtype)


def norm_matmul(x, gain, w, out_dtype):
    t, d = x.shape
    n = w.shape[1]
    return pl.pallas_call(
        _norm_matmul_kernel,
        grid=(t // TOK_TILE,),
        in_specs=[pl.BlockSpec((TOK_TILE, d), lambda i: (i, 0)),
                  pl.BlockSpec((1, d), lambda i: (0, 0)),
                  pl.BlockSpec((d, n), lambda i: (0, 0))],
        out_specs=pl.BlockSpec((TOK_TILE, n), lambda i: (i, 0)),
        out_shape=jax.ShapeDtypeStruct((t, n), out_dtype),
        compiler_params=_cparams(("parallel",)),
        name="norm_matmul",
    )(x, gain.reshape(1, d), w)


def _matmul_residual_norm_kernel(a_ref, w_ref, h_ref, g_ref, o_ref, xt_ref):
    h_new = h_ref[...] + jnp.dot(a_ref[...], w_ref[...], preferred_element_type=F32)
    o_ref[...] = h_new
    xt_ref[...] = _rms(h_new, g_ref[...]).T.astype(xt_ref.dtype)


def matmul_residual_norm(a, w, h, gain):
    t, k = a.shape
    d = w.shape[1]
    return pl.pallas_call(
        _matmul_residual_norm_kernel,
        grid=(t // TOK_TILE,),
        in_specs=[pl.BlockSpec((TOK_TILE, k), lambda i: (i, 0)),
                  pl.BlockSpec((k, d), lambda i: (0, 0)),
                  pl.BlockSpec((TOK_TILE, d), lambda i: (i, 0)),
                  pl.BlockSpec((1, d), lambda i: (0, 0))],
        out_specs=[pl.BlockSpec((TOK_TILE, d), lambda i: (i, 0)),
                   pl.BlockSpec((d, TOK_TILE), lambda i: (0, i))],
        out_shape=[jax.ShapeDtypeStruct((t, d), F32), jax.ShapeDtypeStruct((d, t), BF16)],
        compiler_params=_cparams(("parallel",)),
        name="matmul_residual_norm",
    )(a, w, h, gain.reshape(1, d))


def _final_norm_kernel(x_ref, g_ref, o_ref):
    o_ref[...] = _rms(x_ref[...], g_ref[...])


def final_norm(x, gain):
    t, d = x.shape
    return pl.pallas_call(
        _final_norm_kernel,
        grid=(t // TOK_TILE,),
        in_specs=[pl.BlockSpec((TOK_TILE, d), lambda i: (i, 0)),
                  pl.BlockSpec((1, d), lambda i: (0, 0))],
        out_specs=pl.BlockSpec((TOK_TILE, d), lambda i: (i, 0)),
        out_shape=jax.ShapeDtypeStruct((t, d), F32),
        compiler_params=_cparams(("parallel",)),
        name="final_norm",
    )(x, gain.reshape(1, d))


def _attention_bias_tables(rel_bias, rows):
    n_h = rel_bias.shape[0]
    n_dr = 2 * MAX_WIN_ROWS - 1
    w = GRID_W
    edge = w - WIN_COLS
    line = jnp.pad(rel_bias, ((0, 0), (0, 0), (edge, edge + 1)), mode="edge")
    flat = jnp.tile(line, (1, 1, w))[:, :, w - 1:w - 1 + w * (2 * w - 1)]
    toep = flat.reshape(n_h, n_dr, w, 2 * w - 1)[:, :, :, :w]
    qc = np.arange(w)[:, None]
    kc = np.arange(w)[None, :]
    ws = np.clip(qc - WIN_COLS // 2, 0, w - WIN_COLS)
    toep = jnp.where(((kc >= ws) & (kc < ws + WIN_COLS))[None, None], toep, NEG_INF)
    pad = ATT_Q_ROWS - 1
    toep = jnp.pad(toep, ((0, 0), (pad, pad), (0, 0), (0, 0)), constant_values=NEG_INF)
    tabs = []
    for r0 in (0, ATT_Q_ROWS, rows - ATT_Q_ROWS):
        band = int(np.clip(r0 - MAX_WIN_ROWS // 2, 0, rows - ATT_BAND_ROWS))
        blocks = []
        for ql in range(ATT_Q_ROWS):
            lo = band - (r0 + ql) + MAX_WIN_ROWS - 1 + pad
            blk = toep[:, lo:lo + ATT_BAND_ROWS]
            blocks.append(blk.transpose(0, 2, 1, 3).reshape(n_h, w, ATT_BAND_ROWS * w))
        tab = jnp.stack(blocks, axis=1).reshape(n_h, ATT_Q_ROWS * w, ATT_BAND_ROWS * w)
        qr = r0 + np.arange(ATT_Q_ROWS)[:, None]
        kr = band + np.arange(ATT_BAND_ROWS)[None, :]
        rs = np.clip(qr - MAX_WIN_ROWS // 2, 0, rows - MAX_WIN_ROWS)
        row_ok = (kr >= rs) & (kr < rs + MAX_WIN_ROWS)
        row_ok = np.repeat(np.repeat(row_ok, w, axis=0), w, axis=1)
        tabs.append(jnp.where(row_ok[None], tab, NEG_INF))
    return jnp.stack(tabs)


def _attention_kernel(q_ref, k_ref, v_ref, *rest, rows):
    bias_refs, o_ref = rest[:ATT_BLOCKS_PER_STEP], rest[ATT_BLOCKS_PER_STEP]
    q_tok = ATT_Q_ROWS * GRID_W
    n_keys = ATT_BAND_ROWS * GRID_W
    scale = 1.0 / math.sqrt(HEAD_DIM)
    lane = lax.broadcasted_iota(jnp.int32, (q_tok, ATT_LANES), 1)
    for sub in range(ATT_BLOCKS_PER_STEP):
        rb = pl.program_id(2) * ATT_BLOCKS_PER_STEP + sub
        band = jnp.clip(rb * ATT_Q_ROWS - MAX_WIN_ROWS // 2, 0, rows - ATT_BAND_ROWS)
        start = pl.multiple_of(band * GRID_W, GRID_W)
        kb = k_ref[pl.ds(start, n_keys), :]
        vb = v_ref[pl.ds(start, n_keys), :]
        q = q_ref[sub * q_tok:(sub + 1) * q_tok, :]
        outs = []
        for hh in range(ATT_LANES // HEAD_DIM):
            in_head = (lane < HEAD_DIM) if hh == 0 else (lane >= HEAD_DIM)
            qm = jnp.where(in_head, q, jnp.zeros_like(q)) * jnp.asarray(scale, q.dtype)
            s = lax.dot_general(qm, kb, (((1,), (1,)), ((), ())), preferred_element_type=F32)
            s = s + bias_refs[sub][0, hh]
            m = jnp.max(s, axis=-1, keepdims=True)
            p = jnp.exp(s - m)
            l = jnp.sum(p, axis=-1, keepdims=True)
            o = jnp.dot(p.astype(BF16), vb, preferred_element_type=F32)
            outs.append(o / l)
        o_ref[sub * q_tok:(sub + 1) * q_tok, :] = jnp.where(lane < HEAD_DIM, outs[0],
                                                             outs[1]).astype(o_ref.dtype)


def neighbourhood_attention_core(qkv, bias_tabs, batch, seq):
    t = qkv.shape[0]
    d = qkv.shape[1] // 3
    rows = seq // GRID_W
    n_rb = rows // ATT_Q_ROWS
    n_steps = n_rb // ATT_BLOCKS_PER_STEP
    n_hp = d // ATT_LANES
    q_tok = ATT_Q_ROWS * GRID_W * ATT_BLOCKS_PER_STEP

    def bias_spec(sub):
        def index_map(b, hp, st):
            rb = st * ATT_BLOCKS_PER_STEP + sub
            return (jnp.where(rb == 0, 0, jnp.where(rb == n_rb - 1, 2, 1)), hp, 0, 0)
        return pl.BlockSpec((1, 2, ATT_Q_ROWS * GRID_W, ATT_BAND_ROWS * GRID_W), index_map)

    return pl.pallas_call(
        functools.partial(_attention_kernel, rows=rows),
        grid=(batch, n_hp, n_steps),
        in_specs=[pl.BlockSpec((q_tok, ATT_LANES), lambda b, hp, st: (b * n_steps + st, hp)),
                  pl.BlockSpec((seq, ATT_LANES), lambda b, hp, st: (b, n_hp + hp)),
                  pl.BlockSpec((seq, ATT_LANES), lambda b, hp, st: (b, 2 * n_hp + hp))]
                 + [bias_spec(sub) for sub in range(ATT_BLOCKS_PER_STEP)],
        out_specs=pl.BlockSpec((q_tok, ATT_LANES), lambda b, hp, st: (b * n_steps + st, hp)),
        out_shape=jax.ShapeDtypeStruct((t, d), BF16),
        compiler_params=_cparams(("parallel", "parallel", "arbitrary")),
        name="natten_core",
    )(qkv, qkv, qkv, *([bias_tabs] * ATT_BLOCKS_PER_STEP))


def _short_conv_kernel(b_ref, c_ref, v_ref, cp_ref, vp_ref, cn_ref, vn_ref, w_ref, o_ref, *, seq):
    i = pl.program_id(0)
    tm = b_ref.shape[0]
    u = c_ref[...] * v_ref[...]
    u_before = cp_ref[7:8, :] * vp_ref[7:8, :]
    u_after = cn_ref[0:1, :] * vn_ref[0:1, :]
    row = lax.broadcasted_iota(jnp.int32, u.shape, 0)
    pos = (i * tm + row) % seq
    u_prev = jnp.where(row == 0, u_before, pltpu.roll(u, 1, 0))
    u_prev = jnp.where(pos == 0, 0.0, u_prev)
    u_next = jnp.where(row == tm - 1, u_after, pltpu.roll(u, tm - 1, 0))
    u_next = jnp.where(pos == seq - 1, 0.0, u_next)
    conv = u_prev * w_ref[0:1, :] + u * w_ref[1:2, :] + u_next * w_ref[2:3, :]
    o_ref[...] = (b_ref[...] * conv).astype(o_ref.dtype)


def short_conv_gate(y, w_conv, seq):
    t = y.shape[0]
    d = y.shape[1] // 3
    cw = 512
    nj = d // cw
    tm = TOK_TILE
    n8 = t // 8

    def prev_map(off):
        return lambda i, j: (jnp.maximum(i * (tm // 8) - 1, 0), off * nj + j)

    def next_map(off):
        return lambda i, j: (jnp.minimum((i + 1) * (tm // 8), n8 - 1), off * nj + j)

    return pl.pallas_call(
        functools.partial(_short_conv_kernel, seq=seq),
        grid=(t // tm, nj),
        in_specs=[pl.BlockSpec((tm, cw), lambda i, j: (i, j)),
                  pl.BlockSpec((tm, cw), lambda i, j: (i, nj + j)),
                  pl.BlockSpec((tm, cw), lambda i, j: (i, 2 * nj + j)),
                  pl.BlockSpec((8, cw), prev_map(1)),
                  pl.BlockSpec((8, cw), prev_map(2)),
                  pl.BlockSpec((8, cw), next_map(1)),
                  pl.BlockSpec((8, cw), next_map(2)),
                  pl.BlockSpec((3, cw), lambda i, j: (0, j))],
        out_specs=pl.BlockSpec((tm, cw), lambda i, j: (i, j)),
        out_shape=jax.ShapeDtypeStruct((t, d), BF16),
        compiler_params=_cparams(("parallel", "parallel")),
        name="short_conv_gate",
    )(y, y, y, y, y, y, y, w_conv)


def _top16(sc, want_rank):
    n, l = sc.shape
    iota_n = lax.broadcasted_iota(jnp.int32, (n, l), 0).astype(F32)
    iota_k = lax.broadcasted_iota(jnp.int32, (PEER_TOPK, l), 0).astype(F32)

    def body(r, carry):
        sc, rank, sv, si = carry
        rf = jnp.asarray(r, jnp.int32).astype(F32)
        m = jnp.max(sc, axis=0, keepdims=True)
        idx = jnp.min(jnp.where(sc == m, iota_n, float(n)), axis=0, keepdims=True)
        sel = iota_n == idx
        if want_rank:
            rank = jnp.where(sel, rf, rank)
        sc = jnp.where(sel, -jnp.inf, sc)
        sv = jnp.where(iota_k == rf, m, sv)
        si = jnp.where(iota_k == rf, idx, si)
        return sc, rank, sv, si

    init = (sc, jnp.full((n, l), float(PEER_TOPK), F32),
            jnp.zeros((PEER_TOPK, l), F32), jnp.zeros((PEER_TOPK, l), F32))
    _, rank, sv, si = lax.fori_loop(0, PEER_TOPK, body, init)
    return sv, si, rank


def _merge_top16(sv1, sv2):
    k, l = sv1.shape
    iota_k = lax.broadcasted_iota(jnp.int32, (k, l), 0).astype(F32)
    top = sv1[0:1, :] + sv2[0:1, :]

    def body(_, carry):
        front, cnt, z = carry
        m = jnp.max(front, axis=0, keepdims=True)
        a_star = jnp.min(jnp.where(front == m, iota_k, float(k)), axis=0, keepdims=True)
        sel = iota_k == a_star
        z = z + jnp.exp(m - top)
        cnt = jnp.where(sel, cnt + 1.0, cnt)
        c_sel = jnp.sum(jnp.where(sel, cnt, 0.0), axis=0, keepdims=True)
        nxt = jnp.sum(jnp.where(iota_k == c_sel, sv2, 0.0), axis=0, keepdims=True)
        nxt = jnp.where(c_sel >= float(k), -jnp.inf, nxt)
        front = jnp.where(sel, sv1 + nxt, front)
        return front, cnt, z

    init = (sv1 + sv2[0:1, :], jnp.zeros((k, l), F32), jnp.zeros((1, l), F32))
    _, cnt, z = lax.fori_loop(0, PEER_TOPK, body, init)
    return cnt, z


def _tree(op, xs):
    while len(xs) > 1:
        xs = [op(xs[k], xs[k + 1]) for k in range(0, len(xs) - 1, 2)] + (xs[-1:] if len(xs) % 2 else [])
    return xs[0]


def _all_sublanes(op, x):
    for shift in (4, 2, 1):
        x = op(x, pltpu.roll(x, shift, 0))
    return x


def _count_greater(top, x):
    sel = jnp.where
    c8 = top[7] > x
    c4 = sel(c8, top[11], top[3]) > x
    c2 = sel(c8, sel(c4, top[13], top[9]), sel(c4, top[5], top[1])) > x
    c1 = sel(c8, sel(c4, sel(c2, top[14], top[12]), sel(c2, top[10], top[8])),
             sel(c4, sel(c2, top[6], top[4]), sel(c2, top[2], top[0]))) > x
    c16 = top[15] > x
    zero = jnp.zeros_like(x)
    return ((sel(c8, 8.0, zero) + sel(c4, 4.0, zero)) + (sel(c2, 2.0, zero) + sel(c1, 1.0, zero))
            + sel(c16, 1.0, zero))


def _peer_scores(h, keys_ref, q_scr, s_scr):
    for p in range(2):
        r0 = pl.multiple_of((2 * h + p) * PEER_HALF_DIM, PEER_HALF_DIM)
        s_scr[p] = jnp.dot(keys_ref[2 * h + p], q_scr[pl.ds(r0, PEER_HALF_DIM), :],
                           preferred_element_type=F32)


def _peer_head_exact(h, s_scr, r2_ref, e2_ref, n_ref, e1_ref):
    n_keys, lanes = PEER_N_KEYS, 128
    iota_n = lax.broadcasted_iota(jnp.int32, (n_keys, lanes), 0).astype(F32)
    for lt in range(s_scr.shape[2] // lanes):
        ls = slice(lt * lanes, (lt + 1) * lanes)
        s1 = s_scr[0, :, ls]
        s2 = s_scr[1, :, ls]
        sv1, si1, _ = _top16(s1, want_rank=False)
        sv2, _, rank2 = _top16(s2, want_rank=True)
        cnt, z = _merge_top16(sv1, sv2)
        n_sel = jnp.zeros((n_keys, lanes), F32)
        for a in range(PEER_TOPK):
            n_sel = jnp.where(iota_n == si1[a:a + 1, :], cnt[a:a + 1, :], n_sel)
        r2_ref[h, :, ls] = rank2.astype(r2_ref.dtype)
        e2_ref[h, :, ls] = jnp.exp(s2 - sv2[0:1, :]).astype(e2_ref.dtype)
        n_ref[h, :, ls] = n_sel
        e1_ref[h, :, ls] = jnp.exp(s1 - sv1[0:1, :]) * (1.0 / z)


def _peer_head_distinct(h, s_scr, t_scr, sv_scr, r2_ref, e2_ref, n_ref, e1_ref):
    n_keys, lanes, sub = PEER_N_KEYS, 128, 8
    n_vr = n_keys // sub
    tm = s_scr.shape[2]
    for p in range(2):
        prev = jnp.full((sub, tm), jnp.inf, F32)
        for r in range(PEER_TOPK):
            below = []
            for v in range(n_vr):
                x = s_scr[p, v * sub:(v + 1) * sub, :]
                below.append(jnp.where(x < prev, x, -jnp.inf))
            prev = _all_sublanes(jnp.maximum, _tree(jnp.maximum, below))
            t_scr[p, r] = prev
            sv_scr[p, r:r + 1, :] = prev[0:1, :]
    cnt, z = _merge_top16(sv_scr[0], sv_scr[1])
    inv_z = 1.0 / z
    one = jnp.ones((sub, lanes), F32)
    zero = jnp.zeros((sub, lanes), F32)
    bad = zero
    for lt in range(tm // lanes):
        ls = slice(lt * lanes, (lt + 1) * lanes)
        t2 = [t_scr[1, b, :, ls] for b in range(PEER_TOPK)]
        n_top = zero
        for v in range(n_vr):
            vs = slice(v * sub, (v + 1) * sub)
            x = s_scr[1, vs, ls]
            rank = _count_greater(t2, x)
            r2_ref[h, vs, ls] = rank.astype(r2_ref.dtype)
            e2_ref[h, vs, ls] = jnp.exp(x - t2[0]).astype(e2_ref.dtype)
            n_top = n_top + jnp.where(rank < float(PEER_TOPK), one, zero)
        bad = jnp.maximum(bad, jnp.where(_all_sublanes(jnp.add, n_top) != float(PEER_TOPK), one, zero))
        t1 = [t_scr[0, a, :, ls] for a in range(PEER_TOPK)]
        cnt_b = [jnp.broadcast_to(cnt[a:a + 1, ls], (sub, lanes)) for a in range(PEER_TOPK)]
        inv_b = jnp.broadcast_to(inv_z[:, ls], (sub, lanes))
        n_top = zero
        for v in range(n_vr):
            vs = slice(v * sub, (v + 1) * sub)
            x = s_scr[0, vs, ls]
            n_sel = zero
            for a in range(PEER_TOPK):
                n_sel = jnp.where(x == t1[a], cnt_b[a], n_sel)
            n_ref[h, vs, ls] = n_sel
            e1_ref[h, vs, ls] = jnp.exp(x - t1[0]) * inv_b
            n_top = n_top + jnp.where(x >= t1[PEER_TOPK - 1], one, zero)
        bad = jnp.maximum(bad, jnp.where(_all_sublanes(jnp.add, n_top) != float(PEER_TOPK), one, zero))
    return bad


def _peer_weights_kernel(xt_ref, wq_ref, keys_ref, r2_ref, e2_ref, n_ref, e1_ref,
                         q_scr, s_scr, t_scr, sv_scr):
    q_scr[...] = jnp.dot(wq_ref[...], xt_ref[...], preferred_element_type=F32).astype(BF16)

    def head(h, _):
        _peer_scores(h, keys_ref, q_scr, s_scr)
        bad = _peer_head_distinct(h, s_scr, t_scr, sv_scr, r2_ref, e2_ref, n_ref, e1_ref)

        @pl.when(jnp.max(bad) > 0.0)
        def _():
            _peer_head_exact(h, s_scr, r2_ref, e2_ref, n_ref, e1_ref)

        return 0

    lax.fori_loop(0, PEER_HEADS, head, 0)


def peer_weights(xt, wq_t, keys):
    d, t = xt.shape
    tm = PEER_PRE_TILE
    out16 = jax.ShapeDtypeStruct((PEER_HEADS, PEER_N_KEYS, t), BF16)
    out32 = jax.ShapeDtypeStruct((PEER_HEADS, PEER_N_KEYS, t), F32)
    ospec = pl.BlockSpec((PEER_HEADS, PEER_N_KEYS, tm), lambda i: (0, 0, i))
    return pl.pallas_call(
        _peer_weights_kernel,
        grid=(t // tm,),
        in_specs=[pl.BlockSpec((d, tm), lambda i: (0, i)),
                  pl.BlockSpec(wq_t.shape, lambda i: (0, 0)),
                  pl.BlockSpec(keys.shape, lambda i: (0, 0, 0))],
        out_specs=[ospec, ospec, ospec, ospec],
        out_shape=[out16, out16, out32, out32],
        scratch_shapes=[pltpu.VMEM((wq_t.shape[0], tm), BF16),
                        pltpu.VMEM((2, PEER_N_KEYS, tm), F32),
                        pltpu.VMEM((2, PEER_TOPK, 8, tm), F32),
                        pltpu.VMEM((2, PEER_TOPK, tm), F32)],
        compiler_params=_cparams(("parallel",)),
        name="peer_weights",
    )(xt, wq_t, keys)


def _peer_hidden_rows(i, il, a_ref, hid_ref, r2_ref, e2_ref, n_ref, e1_ref):
    nk, lanes, pack = PEER_N_KEYS, 128, 16
    row0 = il * nk
    n_rows = [n_ref[h, pl.ds(i, 1), :] for h in range(PEER_HEADS)]
    e1_rows = [e1_ref[h, pl.ds(i, 1), :] for h in range(PEER_HEADS)]
    zero = jnp.zeros((pack, lanes), BF16)
    for lt in range(a_ref.shape[1] // lanes):
        ls = slice(lt * lanes, (lt + 1) * lanes)
        n_b = [jnp.broadcast_to(r[:, ls], (pack, lanes)).astype(BF16) for r in n_rows]
        e1_b = [jnp.broadcast_to(r[:, ls], (pack, lanes)).astype(BF16) for r in e1_rows]
        for jb in range(nk // pack):
            js = slice(jb * pack, (jb + 1) * pack)
            g = None
            for h in range(PEER_HEADS):
                term = jnp.where(r2_ref[h, js, ls] < n_b[h], e2_ref[h, js, ls] * e1_b[h], zero)
                g = term if g is None else g + term
            rs = slice(row0 + jb * pack, row0 + (jb + 1) * pack)
            a = a_ref[rs, ls]
            act = 0.5 * a * (1.0 + lax.erf(a * math.sqrt(0.5)))
            hid_ref[rs, ls] = act.astype(BF16) * g


def _peer_dense_kernel(xt_ref, r2_ref, e2_ref, n_ref, e1_ref, wup_ref, wdn_ref, h_ref, o_ref,
                       acc_ref, a0_ref, a1_ref, hid0_ref, hid1_ref, r2_scr, e2_scr):
    e = pl.program_id(1)
    grp = a0_ref.shape[0]
    rows_per_grp = grp // PEER_N_KEYS
    n_grp = wup_ref.shape[0] // grp

    @pl.when(e == 0)
    def _():
        acc_ref[...] = jnp.zeros_like(acc_ref)
        r2_scr[...] = r2_ref[...]
        e2_scr[...] = e2_ref[...]

    a_refs = (a0_ref, a1_ref)
    hid_refs = (hid0_ref, hid1_ref)
    for p in range(n_grp + 2):
        if p < n_grp:
            a_refs[p % 2][...] = jnp.dot(wup_ref[p * grp:(p + 1) * grp, :], xt_ref[...],
                                         preferred_element_type=F32)
        if 1 <= p <= n_grp:
            g = p - 1
            for il in range(rows_per_grp):
                _peer_hidden_rows((e * n_grp + g) * rows_per_grp + il, il, a_refs[g % 2],
                                  hid_refs[g % 2], r2_scr, e2_scr, n_ref, e1_ref)
        if p >= 2:
            g = p - 2
            acc_ref[...] += lax.dot_general(hid_refs[g % 2][...], wdn_ref[g * grp:(g + 1) * grp, :],
                                            (((0,), (0,)), ((), ())), preferred_element_type=F32)

    @pl.when(e == pl.num_programs(1) - 1)
    def _():
        o_ref[...] = h_ref[...] + acc_ref[...]


def peer_dense(xt, r2, e2, n_sel, e1n, w_up, w_down, h):
    d, t = xt.shape
    n_exp = w_up.shape[0]
    tm, te = PEER_TOK_TILE, PEER_EXP_TILE
    wspec = pl.BlockSpec((PEER_HEADS, PEER_N_KEYS, tm), lambda i, e: (0, 0, i))
    return pl.pallas_call(
        _peer_dense_kernel,
        grid=(t // tm, n_exp // te),
        in_specs=[pl.BlockSpec((d, tm), lambda i, e: (0, i)),
                  wspec, wspec, wspec, wspec,
                  pl.BlockSpec((te, d), lambda i, e: (e, 0)),
                  pl.BlockSpec((te, d), lambda i, e: (e, 0)),
                  pl.BlockSpec((tm, d), lambda i, e: (i, 0))],
        out_specs=pl.BlockSpec((tm, d), lambda i, e: (i, 0)),
        out_shape=jax.ShapeDtypeStruct((t, d), F32),
        scratch_shapes=[pltpu.VMEM((tm, d), F32),
                        pltpu.VMEM((PEER_STAGE_ROWS, tm), F32),
                        pltpu.VMEM((PEER_STAGE_ROWS, tm), F32),
                        pltpu.VMEM((PEER_STAGE_ROWS, tm), BF16),
                        pltpu.VMEM((PEER_STAGE_ROWS, tm), BF16),
                        pltpu.VMEM((PEER_HEADS, PEER_N_KEYS, tm), BF16),
                        pltpu.VMEM((PEER_HEADS, PEER_N_KEYS, tm), BF16)],
        compiler_params=_cparams(("parallel", "arbitrary")),
        name="peer_dense",
    )(xt, r2, e2, n_sel, e1n, w_up, w_down, h)


def kernel(x, norm_mix, norm_ffn, norm_final, attn_w_qkv, attn_w_o, attn_rel_bias,
           conv_w_in, conv_w_conv, conv_w_out, peer_w_query, peer_sub_keys,
           peer_w_up, peer_w_down):
    batch, seq, d = x.shape
    depth = norm_mix.shape[0]
    rows = seq // GRID_W
    h = x.reshape(batch * seq, d)
    for i in range(depth):
        j = i // 2
        if i % 2 == 0:
            qkv = norm_matmul(h, norm_mix[i], attn_w_qkv[j].astype(BF16), BF16)
            bias_tabs = _attention_bias_tables(attn_rel_bias[j], rows)
            mixed = neighbourhood_attention_core(qkv, bias_tabs, batch, seq)
            w_out = attn_w_o[j]
        else:
            y = norm_matmul(h, norm_mix[i], conv_w_in[j].astype(BF16), F32)
            mixed = short_conv_gate(y, conv_w_conv[j], seq)
            w_out = conv_w_out[j]
        h, xt = matmul_residual_norm(mixed, w_out.astype(BF16), h, norm_ffn[i])
        keys = peer_sub_keys[i].reshape(PEER_HEADS * 2, PEER_N_KEYS, PEER_HALF_DIM).astype(BF16)
        r2, e2, n_sel, e1n = peer_weights(xt, peer_w_query[i].T.astype(BF16), keys)
        h = peer_dense(xt, r2, e2, n_sel, e1n, peer_w_up[i].astype(BF16),
                       peer_w_down[i].astype(BF16), h)
    return final_norm(h, norm_final).reshape(batch, seq, d)
```

```python
import functools
import math

import numpy as np
import jax
import jax.numpy as jnp
from jax import lax
from jax.experimental import pallas as pl
from jax.experimental.pallas import tpu as pltpu

F32 = jnp.float32
BF16 = jnp.bfloat16

RMS_EPS = 1e-6
NEG_INF = -1e30

GRID_W = 64
N_HEADS = 16
HEAD_DIM = 64
MAX_WIN_ROWS = 8
WIN_COLS = 16
ATT_Q_ROWS = 4
ATT_BAND_ROWS = ATT_Q_ROWS + MAX_WIN_ROWS - 1
ATT_LANES = 128
ATT_BLOCKS_PER_STEP = 4

PEER_HEADS = 8
PEER_N_KEYS = 128
PEER_TOPK = 16
PEER_HALF_DIM = 128
GELU_HALF = 0.5

TOK_TILE = 512
PEER_PRE_TILE = 512
PEER_TOK_TILE = 512
PEER_EXP_TILE = 2048
PEER_STAGE_ROWS = 256
VMEM_LIMIT = 56 * 1024 * 1024


def _cparams(sem):
    return pltpu.CompilerParams(dimension_semantics=sem, vmem_limit_bytes=VMEM_LIMIT)


def _rms(x, g):
    return x * lax.rsqrt(jnp.mean(x * x, axis=-1, keepdims=True) + RMS_EPS) * g


def _norm_matmul_kernel(x_ref, g_ref, w_ref, o_ref):
    y = _rms(x_ref[...], g_ref[...])
    o_ref[...] = jnp.dot(y.astype(BF16), w_ref[...],
                         preferred_element_type=F32).astype(o_ref.dtype)


def norm_matmul(x, gain, w, out_dtype):
    t, d = x.shape
    n = w.shape[1]
    return pl.pallas_call(
        _norm_matmul_kernel,
        grid=(t // TOK_TILE,),
        in_specs=[pl.BlockSpec((TOK_TILE, d), lambda i: (i, 0)),
                  pl.BlockSpec((1, d), lambda i: (0, 0)),
                  pl.BlockSpec((d, n), lambda i: (0, 0))],
        out_specs=pl.BlockSpec((TOK_TILE, n), lambda i: (i, 0)),
        out_shape=jax.ShapeDtypeStruct((t, n), out_dtype),
        compiler_params=_cparams(("parallel",)),
        name="norm_matmul",
    )(x, gain.reshape(1, d), w)


def _matmul_residual_norm_kernel(a_ref, w_ref, h_ref, g_ref, o_ref, xt_ref):
    h_new = h_ref[...] + jnp.dot(a_ref[...], w_ref[...], preferred_element_type=F32)
    o_ref[...] = h_new
    xt_ref[...] = _rms(h_new, g_ref[...]).T.astype(xt_ref.dtype)


def matmul_residual_norm(a, w, h, gain):
    t, k = a.shape
    d = w.shape[1]
    return pl.pallas_call(
        _matmul_residual_norm_kernel,
        grid=(t // TOK_TILE,),
        in_specs=[pl.BlockSpec((TOK_TILE, k), lambda i: (i, 0)),
                  pl.BlockSpec((k, d), lambda i: (0, 0)),
                  pl.BlockSpec((TOK_TILE, d), lambda i: (i, 0)),
                  pl.BlockSpec((1, d), lambda i: (0, 0))],
        out_specs=[pl.BlockSpec((TOK_TILE, d), lambda i: (i, 0)),
                   pl.BlockSpec((d, TOK_TILE), lambda i: (0, i))],
        out_shape=[jax.ShapeDtypeStruct((t, d), F32), jax.ShapeDtypeStruct((d, t), BF16)],
        compiler_params=_cparams(("parallel",)),
        name="matmul_residual_norm",
    )(a, w, h, gain.reshape(1, d))


def _final_norm_kernel(x_ref, g_ref, o_ref):
    o_ref[...] = _rms(x_ref[...], g_ref[...])


def final_norm(x, gain):
    t, d = x.shape
    return pl.pallas_call(
        _final_norm_kernel,
        grid=(t // TOK_TILE,),
        in_specs=[pl.BlockSpec((TOK_TILE, d), lambda i: (i, 0)),
                  pl.BlockSpec((1, d), lambda i: (0, 0))],
        out_specs=pl.BlockSpec((TOK_TILE, d), lambda i: (i, 0)),
        out_shape=jax.ShapeDtypeStruct((t, d), F32),
        compiler_params=_cparams(("parallel",)),
        name="final_norm",
    )(x, gain.reshape(1, d))


def _attention_bias_tables(rel_bias, rows):
    n_h = rel_bias.shape[0]
    n_dr = 2 * MAX_WIN_ROWS - 1
    w = GRID_W
    edge = w - WIN_COLS
    line = jnp.pad(rel_bias, ((0, 0), (0, 0), (edge, edge + 1)), mode="edge")
    flat = jnp.tile(line, (1, 1, w))[:, :, w - 1:w - 1 + w * (2 * w - 1)]
    toep = flat.reshape(n_h, n_dr, w, 2 * w - 1)[:, :, :, :w]
    qc = np.arange(w)[:, None]
    kc = np.arange(w)[None, :]
    ws = np.clip(qc - WIN_COLS // 2, 0, w - WIN_COLS)
    toep = jnp.where(((kc >= ws) & (kc < ws + WIN_COLS))[None, None], toep, NEG_INF)
    pad = ATT_Q_ROWS - 1
    toep = jnp.pad(toep, ((0, 0), (pad, pad), (0, 0), (0, 0)), constant_values=NEG_INF)
    tabs = []
    for r0 in (0, ATT_Q_ROWS, rows - ATT_Q_ROWS):
        band = int(np.clip(r0 - MAX_WIN_ROWS // 2, 0, rows - ATT_BAND_ROWS))
        blocks = []
        for ql in range(ATT_Q_ROWS):
            lo = band - (r0 + ql) + MAX_WIN_ROWS - 1 + pad
            blk = toep[:, lo:lo + ATT_BAND_ROWS]
            blocks.append(blk.transpose(0, 2, 1, 3).reshape(n_h, w, ATT_BAND_ROWS * w))
        tab = jnp.stack(blocks, axis=1).reshape(n_h, ATT_Q_ROWS * w, ATT_BAND_ROWS * w)
        qr = r0 + np.arange(ATT_Q_ROWS)[:, None]
        kr = band + np.arange(ATT_BAND_ROWS)[None, :]
        rs = np.clip(qr - MAX_WIN_ROWS // 2, 0, rows - MAX_WIN_ROWS)
        row_ok = (kr >= rs) & (kr < rs + MAX_WIN_ROWS)
        row_ok = np.repeat(np.repeat(row_ok, w, axis=0), w, axis=1)
        tabs.append(jnp.where(row_ok[None], tab, NEG_INF))
    return jnp.stack(tabs)


def _attention_kernel(q_ref, k_ref, v_ref, *rest, rows):
    bias_refs, o_ref = rest[:ATT_BLOCKS_PER_STEP], rest[ATT_BLOCKS_PER_STEP]
    q_tok = ATT_Q_ROWS * GRID_W
    n_keys = ATT_BAND_ROWS * GRID_W
    scale = 1.0 / math.sqrt(HEAD_DIM)
    lane = lax.broadcasted_iota(jnp.int32, (q_tok, ATT_LANES), 1)
    for sub in range(ATT_BLOCKS_PER_STEP):
        rb = pl.program_id(2) * ATT_BLOCKS_PER_STEP + sub
        band = jnp.clip(rb * ATT_Q_ROWS - MAX_WIN_ROWS // 2, 0, rows - ATT_BAND_ROWS)
        start = pl.multiple_of(band * GRID_W, GRID_W)
        kb = k_ref[pl.ds(start, n_keys), :]
        vb = v_ref[pl.ds(start, n_keys), :]
        q = q_ref[sub * q_tok:(sub + 1) * q_tok, :]
        outs = []
        for hh in range(ATT_LANES // HEAD_DIM):
            in_head = (lane < HEAD_DIM) if hh == 0 else (lane >= HEAD_DIM)
            qm = jnp.where(in_head, q, jnp.zeros_like(q)) * jnp.asarray(scale, q.dtype)
            s = lax.dot_general(qm, kb, (((1,), (1,)), ((), ())), preferred_element_type=F32)
            s = s + bias_refs[sub][0, hh]
            m = jnp.max(s, axis=-1, keepdims=True)
            p = jnp.exp(s - m)
            l = jnp.sum(p, axis=-1, keepdims=True)
            o = jnp.dot(p.astype(BF16), vb, preferred_element_type=F32)
            outs.append(o / l)
        o_ref[sub * q_tok:(sub + 1) * q_tok, :] = jnp.where(lane < HEAD_DIM, outs[0],
                                                             outs[1]).astype(o_ref.dtype)


def neighbourhood_attention_core(qkv, bias_tabs, batch, seq):
    t = qkv.shape[0]
    d = qkv.shape[1] // 3
    rows = seq // GRID_W
    n_rb = rows // ATT_Q_ROWS
    n_steps = n_rb // ATT_BLOCKS_PER_STEP
    n_hp = d // ATT_LANES
    q_tok = ATT_Q_ROWS * GRID_W * ATT_BLOCKS_PER_STEP

    def bias_spec(sub):
        def index_map(b, hp, st):
            rb = st * ATT_BLOCKS_PER_STEP + sub
            return (jnp.where(rb == 0, 0, jnp.where(rb == n_rb - 1, 2, 1)), hp, 0, 0)
        return pl.BlockSpec((1, 2, ATT_Q_ROWS * GRID_W, ATT_BAND_ROWS * GRID_W), index_map)

    return pl.pallas_call(
        functools.partial(_attention_kernel, rows=rows),
        grid=(batch, n_hp, n_steps),
        in_specs=[pl.BlockSpec((q_tok, ATT_LANES), lambda b, hp, st: (b * n_steps + st, hp)),
                  pl.BlockSpec((seq, ATT_LANES), lambda b, hp, st: (b, n_hp + hp)),
                  pl.BlockSpec((seq, ATT_LANES), lambda b, hp, st: (b, 2 * n_hp + hp))]
                 + [bias_spec(sub) for sub in range(ATT_BLOCKS_PER_STEP)],
        out_specs=pl.BlockSpec((q_tok, ATT_LANES), lambda b, hp, st: (b * n_steps + st, hp)),
        out_shape=jax.ShapeDtypeStruct((t, d), BF16),
        compiler_params=_cparams(("parallel", "parallel", "arbitrary")),
        name="natten_core",
    )(qkv, qkv, qkv, *([bias_tabs] * ATT_BLOCKS_PER_STEP))


def _short_conv_kernel(b_ref, c_ref, v_ref, cp_ref, vp_ref, cn_ref, vn_ref, w_ref, o_ref, *, seq):
    i = pl.program_id(0)
    tm = b_ref.shape[0]
    u = c_ref[...] * v_ref[...]
    u_before = cp_ref[7:8, :] * vp_ref[7:8, :]
    u_after = cn_ref[0:1, :] * vn_ref[0:1, :]
    row = lax.broadcasted_iota(jnp.int32, u.shape, 0)
    pos = (i * tm + row) % seq
    u_prev = jnp.where(row == 0, u_before, pltpu.roll(u, 1, 0))
    u_prev = jnp.where(pos == 0, 0.0, u_prev)
    u_next = jnp.where(row == tm - 1, u_after, pltpu.roll(u, tm - 1, 0))
    u_next = jnp.where(pos == seq - 1, 0.0, u_next)
    conv = u_prev * w_ref[0:1, :] + u * w_ref[1:2, :] + u_next * w_ref[2:3, :]
    o_ref[...] = (b_ref[...] * conv).astype(o_ref.dtype)


def short_conv_gate(y, w_conv, seq):
    t = y.shape[0]
    d = y.shape[1] // 3
    cw = 512
    nj = d // cw
    tm = TOK_TILE
    n8 = t // 8

    def prev_map(off):
        return lambda i, j: (jnp.maximum(i * (tm // 8) - 1, 0), off * nj + j)

    def next_map(off):
        return lambda i, j: (jnp.minimum((i + 1) * (tm // 8), n8 - 1), off * nj + j)

    return pl.pallas_call(
        functools.partial(_short_conv_kernel, seq=seq),
        grid=(t // tm, nj),
        in_specs=[pl.BlockSpec((tm, cw), lambda i, j: (i, j)),
                  pl.BlockSpec((tm, cw), lambda i, j: (i, nj + j)),
                  pl.BlockSpec((tm, cw), lambda i, j: (i, 2 * nj + j)),
                  pl.BlockSpec((8, cw), prev_map(1)),
                  pl.BlockSpec((8, cw), prev_map(2)),
                  pl.BlockSpec((8, cw), next_map(1)),
                  pl.BlockSpec((8, cw), next_map(2)),
                  pl.BlockSpec((3, cw), lambda i, j: (0, j))],
        out_specs=pl.BlockSpec((tm, cw), lambda i, j: (i, j)),
        out_shape=jax.ShapeDtypeStruct((t, d), BF16),
        compiler_params=_cparams(("parallel", "parallel")),
        name="short_conv_gate",
    )(y, y, y, y, y, y, y, w_conv)


def _top16(sc, want_rank):
    n, l = sc.shape
    iota_n = lax.broadcasted_iota(jnp.int32, (n, l), 0).astype(F32)
    iota_k = lax.broadcasted_iota(jnp.int32, (PEER_TOPK, l), 0).astype(F32)

    def body(r, carry):
        sc, rank, sv, si = carry
        rf = jnp.asarray(r, jnp.int32).astype(F32)
        m = jnp.max(sc, axis=0, keepdims=True)
        idx = jnp.min(jnp.where(sc == m, iota_n, float(n)), axis=0, keepdims=True)
        sel = iota_n == idx
        if want_rank:
            rank = jnp.where(sel, rf, rank)
        sc = jnp.where(sel, -jnp.inf, sc)
        sv = jnp.where(iota_k == rf, m, sv)
        si = jnp.where(iota_k == rf, idx, si)
        return sc, rank, sv, si

    init = (sc, jnp.full((n, l), float(PEER_TOPK), F32),
            jnp.zeros((PEER_TOPK, l), F32), jnp.zeros((PEER_TOPK, l), F32))
    _, rank, sv, si = lax.fori_loop(0, PEER_TOPK, body, init)
    return sv, si, rank


def _merge_top16(sv1, sv2):
    k = sv1.shape[0]
    iota_k = lax.broadcasted_iota(jnp.int32, sv1.shape, 0).astype(F32)
    top = sv1[0:1] + sv2[0:1]

    def body(_, carry):
        front, cnt, z = carry
        m = jnp.max(front, axis=0, keepdims=True)
        a_star = jnp.min(jnp.where(front == m, iota_k, float(k)), axis=0, keepdims=True)
        sel = iota_k == a_star
        z = z + jnp.exp(m - top)
        cnt = jnp.where(sel, cnt + 1.0, cnt)
        c_sel = jnp.sum(jnp.where(sel, cnt, 0.0), axis=0, keepdims=True)
        nxt = jnp.sum(jnp.where(iota_k == c_sel, sv2, 0.0), axis=0, keepdims=True)
        nxt = jnp.where(c_sel >= float(k), -jnp.inf, nxt)
        front = jnp.where(sel, sv1 + nxt, front)
        return front, cnt, z

    init = (sv1 + sv2[0:1], jnp.zeros(sv1.shape, F32), jnp.zeros(top.shape, F32))
    _, cnt, z = lax.fori_loop(0, PEER_TOPK, body, init)
    return cnt, z


def _tree(op, xs):
    while len(xs) > 1:
        xs = [op(xs[k], xs[k + 1]) for k in range(0, len(xs) - 1, 2)] + (xs[-1:] if len(xs) % 2 else [])
    return xs[0]


def _all_sublanes(op, x):
    for shift in (4, 2, 1):
        x = op(x, pltpu.roll(x, shift, 0))
    return x


def _count_greater(top, x):
    sel = jnp.where
    c8 = top[7] > x
    c4 = sel(c8, top[11], top[3]) > x
    c2 = sel(c8, sel(c4, top[13], top[9]), sel(c4, top[5], top[1])) > x
    c1 = sel(c8, sel(c4, sel(c2, top[14], top[12]), sel(c2, top[10], top[8])),
             sel(c4, sel(c2, top[6], top[4]), sel(c2, top[2], top[0]))) > x
    c16 = top[15] > x
    zero = jnp.zeros_like(x)
    return ((sel(c8, 8.0, zero) + sel(c4, 4.0, zero)) + (sel(c2, 2.0, zero) + sel(c1, 1.0, zero))
            + sel(c16, 1.0, zero))


def _sorting_network(n):
    def merge(lo, hi, r):
        step = r * 2
        if step < hi - lo:
            yield from merge(lo, hi, step)
            yield from merge(lo + r, hi, step)
            yield from [(k, k + r) for k in range(lo + r, hi - r, step)]
        else:
            yield (lo, lo + r)

    def sort(lo, hi):
        if hi - lo >= 1:
            mid = lo + (hi - lo) // 2
            yield from sort(lo, mid)
            yield from sort(mid + 1, hi)
            yield from merge(lo, hi, 1)

    return tuple(sort(0, n - 1))


_SORT16 = _sorting_network(PEER_TOPK)


def _compare_exchange(xs, i, j):
    xs[i], xs[j] = jnp.maximum(xs[i], xs[j]), jnp.minimum(xs[i], xs[j])


def _top16_sorted(rows):
    xs = list(rows)
    for i, j in _SORT16:
        _compare_exchange(xs, i, j)
    k = len(xs)
    for shift in (4, 2, 1):
        ys = [pltpu.roll(x, shift, 0) for x in xs]
        xs = [jnp.maximum(xs[v], ys[k - 1 - v]) for v in range(k)]
        stride = k // 2
        while stride:
            for v in range(k):
                if not v & stride:
                    _compare_exchange(xs, v, v + stride)
            stride //= 2
    return xs


def _peer_scores(h, keys_ref, q_scr, s_scr):
    for p in range(2):
        r0 = pl.multiple_of((2 * h + p) * PEER_HALF_DIM, PEER_HALF_DIM)
        s_scr[h, p] = jnp.dot(keys_ref[2 * h + p], q_scr[pl.ds(r0, PEER_HALF_DIM), :],
                              preferred_element_type=F32)


def _peer_head_exact(h, s_scr, r2_ref, e2_ref, n_ref, e1_ref):
    n_keys, lanes = PEER_N_KEYS, 128
    iota_n = lax.broadcasted_iota(jnp.int32, (n_keys, lanes), 0).astype(F32)
    for lt in range(s_scr.shape[3] // lanes):
        ls = slice(lt * lanes, (lt + 1) * lanes)
        s1 = s_scr[h, 0, :, ls]
        s2 = s_scr[h, 1, :, ls]
        sv1, si1, _ = _top16(s1, want_rank=False)
        sv2, _, rank2 = _top16(s2, want_rank=True)
        cnt, z = _merge_top16(sv1, sv2)
        n_sel = jnp.zeros((n_keys, lanes), F32)
        for a in range(PEER_TOPK):
            n_sel = jnp.where(iota_n == si1[a:a + 1, :], cnt[a:a + 1, :], n_sel)
        r2_ref[h, :, ls] = rank2.astype(r2_ref.dtype)
        e2_ref[h, :, ls] = jnp.exp(s2 - sv2[0:1, :]).astype(e2_ref.dtype)
        n_ref[h, :, ls] = n_sel
        e1_ref[h, :, ls] = jnp.exp(s1 - sv1[0:1, :]) * (GELU_HALF / z)


def _peer_head_top_values(h, s_scr, t_scr, sv_scr):
    n_keys, lanes, sub = PEER_N_KEYS, 128, 8

    head_row = lax.broadcasted_iota(jnp.int32, (sub, lanes), 0) == h

    def lane_tile(lt, _):
        ls = pl.ds(pl.multiple_of(lt * lanes, lanes), lanes)
        for p in range(2):
            rows = [s_scr[h, p, v * sub:(v + 1) * sub, ls] for v in range(n_keys // sub)]
            top = _top16_sorted(rows)
            for r in range(PEER_TOPK):
                t_scr[h, p, r, :, ls] = top[r]
                sv_scr[p, r, :, ls] = jnp.where(head_row, top[r], sv_scr[p, r, :, ls])
        return 0

    lax.fori_loop(0, s_scr.shape[3] // lanes, lane_tile, 0)


def _peer_head_distinct(h, s_scr, t_scr, cnt_scr, z_scr, r2_ref, e2_ref, n_ref, e1_ref):
    n_keys, lanes, sub = PEER_N_KEYS, 128, 8
    n_vr = n_keys // sub
    one = jnp.ones((sub, lanes), F32)
    zero = jnp.zeros((sub, lanes), F32)
    bad = zero
    cnt_rows = [cnt_scr[a, pl.ds(h, 1), :] for a in range(PEER_TOPK)]
    inv_row = GELU_HALF / z_scr[pl.ds(h, 1), :]
    for lt in range(s_scr.shape[3] // lanes):
        ls = slice(lt * lanes, (lt + 1) * lanes)
        for p in (1, 0):
            top = [t_scr[h, p, b, :, ls] for b in range(PEER_TOPK)]
            tied = _tree(jnp.maximum, [jnp.where(top[b] > top[b + 1], zero, one)
                                       for b in range(PEER_TOPK - 1)])
            n_top = zero
            if p == 1:
                for v in range(n_vr):
                    vs = slice(v * sub, (v + 1) * sub)
                    x = s_scr[h, 1, vs, ls]
                    rank = _count_greater(top, x)
                    r2_ref[h, vs, ls] = rank.astype(r2_ref.dtype)
                    e2_ref[h, vs, ls] = jnp.exp(x - top[0]).astype(e2_ref.dtype)
                    n_top = n_top + jnp.where(rank < float(PEER_TOPK), one, zero)
            else:
                cnt_b = [jnp.broadcast_to(r[:, ls], (sub, lanes)) for r in cnt_rows]
                inv_b = jnp.broadcast_to(inv_row[:, ls], (sub, lanes))
                for v in range(n_vr):
                    vs = slice(v * sub, (v + 1) * sub)
                    x = s_scr[h, 0, vs, ls]
                    n_sel = zero
                    for a in range(PEER_TOPK):
                        n_sel = jnp.where(x == top[a], cnt_b[a], n_sel)
                    n_ref[h, vs, ls] = n_sel
                    e1_ref[h, vs, ls] = jnp.exp(x - top[0]) * inv_b
                    n_top = n_top + jnp.where(x >= top[PEER_TOPK - 1], one, zero)
            miscount = jnp.where(_all_sublanes(jnp.add, n_top) != float(PEER_TOPK), one, zero)
            bad = jnp.maximum(bad, jnp.maximum(tied, miscount))
    return bad


def _peer_weights_kernel(xt_ref, wq_ref, keys_ref, r2_ref, e2_ref, n_ref, e1_ref,
                         q_scr, s_scr, t_scr, sv_scr, cnt_scr, z_scr):
    q_scr[...] = jnp.dot(wq_ref[...], xt_ref[...], preferred_element_type=F32).astype(BF16)
    sv_scr[...] = jnp.zeros_like(sv_scr)

    def top_values(h, _):
        _peer_scores(h, keys_ref, q_scr, s_scr)
        _peer_head_top_values(h, s_scr, t_scr, sv_scr)
        return 0

    lax.fori_loop(0, PEER_HEADS, top_values, 0)
    cnt, z = _merge_top16(sv_scr[0], sv_scr[1])
    cnt_scr[...] = cnt
    z_scr[...] = z[0]

    def weights(h, _):
        bad = _peer_head_distinct(h, s_scr, t_scr, cnt_scr, z_scr, r2_ref, e2_ref, n_ref, e1_ref)

        @pl.when(jnp.max(bad) > 0.0)
        def _():
            _peer_head_exact(h, s_scr, r2_ref, e2_ref, n_ref, e1_ref)

        return 0

    lax.fori_loop(0, PEER_HEADS, weights, 0)


def peer_weights(xt, wq_t, keys):
    d, t = xt.shape
    tm = PEER_PRE_TILE
    out16 = jax.ShapeDtypeStruct((PEER_HEADS, PEER_N_KEYS, t), BF16)
    out32 = jax.ShapeDtypeStruct((PEER_HEADS, PEER_N_KEYS, t), F32)
    ospec = pl.BlockSpec((PEER_HEADS, PEER_N_KEYS, tm), lambda i: (0, 0, i))
    return pl.pallas_call(
        _peer_weights_kernel,
        grid=(t // tm,),
        in_specs=[pl.BlockSpec((d, tm), lambda i: (0, i)),
                  pl.BlockSpec(wq_t.shape, lambda i: (0, 0)),
                  pl.BlockSpec(keys.shape, lambda i: (0, 0, 0))],
        out_specs=[ospec, ospec, ospec, ospec],
        out_shape=[out16, out16, out32, out32],
        scratch_shapes=[pltpu.VMEM((wq_t.shape[0], tm), BF16),
                        pltpu.VMEM((PEER_HEADS, 2, PEER_N_KEYS, tm), F32),
                        pltpu.VMEM((PEER_HEADS, 2, PEER_TOPK, 8, tm), F32),
                        pltpu.VMEM((2, PEER_TOPK, PEER_HEADS, tm), F32),
                        pltpu.VMEM((PEER_TOPK, PEER_HEADS, tm), F32),
                        pltpu.VMEM((PEER_HEADS, tm), F32)],
        compiler_params=_cparams(("parallel",)),
        name="peer_weights",
    )(xt, wq_t, keys)


def _peer_hidden_rows(i, il, a_ref, hid_ref, r2_ref, e2_ref, n_ref, e1_ref):
    nk, lanes, pack = PEER_N_KEYS, 128, 16
    row0 = il * nk
    n_rows = [n_ref[h, pl.ds(i, 1), :] for h in range(PEER_HEADS)]
    e1_rows = [e1_ref[h, pl.ds(i, 1), :] for h in range(PEER_HEADS)]
    zero = jnp.zeros((pack, lanes), BF16)
    for lt in range(a_ref.shape[1] // lanes):
        ls = slice(lt * lanes, (lt + 1) * lanes)
        n_b = [jnp.broadcast_to(r[:, ls], (pack, lanes)).astype(BF16) for r in n_rows]
        e1_b = [jnp.broadcast_to(r[:, ls], (pack, lanes)).astype(BF16) for r in e1_rows]
        for jb in range(nk // pack):
            js = slice(jb * pack, (jb + 1) * pack)
            g = None
            for h in range(PEER_HEADS):
                term = jnp.where(r2_ref[h, js, ls] < n_b[h], e2_ref[h, js, ls] * e1_b[h], zero)
                g = term if g is None else g + term
            rs = slice(row0 + jb * pack, row0 + (jb + 1) * pack)
            a = a_ref[rs, ls]
            act = a * (1.0 + lax.erf(a * math.sqrt(0.5)))
            hid_ref[rs, ls] = act.astype(BF16) * g


def _peer_dense_kernel(xt_ref, r2_ref, e2_ref, n_ref, e1_ref, wup_ref, wdn_ref, h_ref, o_ref,
                       acc_ref, a0_ref, a1_ref, hid0_ref, hid1_ref, r2_scr, e2_scr):
    e = pl.program_id(1)
    grp = a0_ref.shape[0]
    rows_per_grp = grp // PEER_N_KEYS
    n_grp = wup_ref.shape[0] // grp

    @pl.when(e == 0)
    def _():
        acc_ref[...] = jnp.zeros_like(acc_ref)
        r2_scr[...] = r2_ref[...]
        e2_scr[...] = e2_ref[...]

    a_refs = (a0_ref, a1_ref)
    hid_refs = (hid0_ref, hid1_ref)
    for p in range(n_grp + 2):
        if p < n_grp:
            a_refs[p % 2][...] = jnp.dot(wup_ref[p * grp:(p + 1) * grp, :], xt_ref[...],
                                         preferred_element_type=F32)
        if 1 <= p <= n_grp:
            g = p - 1
            for il in range(rows_per_grp):
                _peer_hidden_rows((e * n_grp + g) * rows_per_grp + il, il, a_refs[g % 2],
                                  hid_refs[g % 2], r2_scr, e2_scr, n_ref, e1_ref)
        if p >= 2:
            g = p - 2
            acc_ref[...] += lax.dot_general(hid_refs[g % 2][...], wdn_ref[g * grp:(g + 1) * grp, :],
                                            (((0,), (0,)), ((), ())), preferred_element_type=F32)

    @pl.when(e == pl.num_programs(1) - 1)
    def _():
        o_ref[...] = h_ref[...] + acc_ref[...]


def peer_dense(xt, r2, e2, n_sel, e1n, w_up, w_down, h):
    d, t = xt.shape
    n_exp = w_up.shape[0]
    tm, te = PEER_TOK_TILE, PEER_EXP_TILE
    wspec = pl.BlockSpec((PEER_HEADS, PEER_N_KEYS, tm), lambda i, e: (0, 0, i))
    return pl.pallas_call(
        _peer_dense_kernel,
        grid=(t // tm, n_exp // te),
        in_specs=[pl.BlockSpec((d, tm), lambda i, e: (0, i)),
                  wspec, wspec, wspec, wspec,
                  pl.BlockSpec((te, d), lambda i, e: (e, 0)),
                  pl.BlockSpec((te, d), lambda i, e: (e, 0)),
                  pl.BlockSpec((tm, d), lambda i, e: (i, 0))],
        out_specs=pl.BlockSpec((tm, d), lambda i, e: (i, 0)),
        out_shape=jax.ShapeDtypeStruct((t, d), F32),
        scratch_shapes=[pltpu.VMEM((tm, d), F32),
                        pltpu.VMEM((PEER_STAGE_ROWS, tm), F32),
                        pltpu.VMEM((PEER_STAGE_ROWS, tm), F32),
                        pltpu.VMEM((PEER_STAGE_ROWS, tm), BF16),
                        pltpu.VMEM((PEER_STAGE_ROWS, tm), BF16),
                        pltpu.VMEM((PEER_HEADS, PEER_N_KEYS, tm), BF16),
                        pltpu.VMEM((PEER_HEADS, PEER_N_KEYS, tm), BF16)],
        compiler_params=_cparams(("parallel", "arbitrary")),
        name="peer_dense",
    )(xt, r2, e2, n_sel, e1n, w_up, w_down, h)


def kernel(x, norm_mix, norm_ffn, norm_final, attn_w_qkv, attn_w_o, attn_rel_bias,
           conv_w_in, conv_w_conv, conv_w_out, peer_w_query, peer_sub_keys,
           peer_w_up, peer_w_down):
    batch, seq, d = x.shape
    depth = norm_mix.shape[0]
    rows = seq // GRID_W
    h = x.reshape(batch * seq, d)
    for i in range(depth):
        j = i // 2
        if i % 2 == 0:
            qkv = norm_matmul(h, norm_mix[i], attn_w_qkv[j].astype(BF16), BF16)
            bias_tabs = _attention_bias_tables(attn_rel_bias[j], rows)
            mixed = neighbourhood_attention_core(qkv, bias_tabs, batch, seq)
            w_out = attn_w_o[j]
        else:
            y = norm_matmul(h, norm_mix[i], conv_w_in[j].astype(BF16), F32)
            mixed = short_conv_gate(y, conv_w_conv[j], seq)
            w_out = conv_w_out[j]
        h, xt = matmul_residual_norm(mixed, w_out.astype(BF16), h, norm_ffn[i])
        keys = peer_sub_keys[i].reshape(PEER_HEADS * 2, PEER_N_KEYS, PEER_HALF_DIM).astype(BF16)
        r2, e2, n_sel, e1n = peer_weights(xt, peer_w_query[i].T.astype(BF16), keys)
        h = peer_dense(xt, r2, e2, n_sel, e1n, peer_w_up[i].astype(BF16),
                       peer_w_down[i].astype(BF16), h)
    return final_norm(h, norm_final).reshape(batch, seq, d)
```

```python
import functools
import math

import numpy as np
import jax
import jax.numpy as jnp
from jax import lax
from jax.experimental import pallas as pl
from jax.experimental.pallas import tpu as pltpu

F32 = jnp.float32
BF16 = jnp.bfloat16

RMS_EPS = 1e-6
NEG_INF = -1e30

GRID_W = 64
N_HEADS = 16
HEAD_DIM = 64
MAX_WIN_ROWS = 8
WIN_COLS = 16
ATT_Q_ROWS = 4
ATT_BAND_ROWS = ATT_Q_ROWS + MAX_WIN_ROWS - 1
ATT_LANES = 128
ATT_BLOCKS_PER_STEP = 4

PEER_HEADS = 8
PEER_N_KEYS = 128
PEER_TOPK = 16
PEER_HALF_DIM = 128
GELU_HALF = 0.5

TOK_TILE = 512
PEER_PRE_TILE = 512
PEER_TOK_TILE = 512
PEER_EXP_TILE = 2048
PEER_STAGE_ROWS = 256
VMEM_LIMIT = 56 * 1024 * 1024


def _cparams(sem):
    return pltpu.CompilerParams(dimension_semantics=sem, vmem_limit_bytes=VMEM_LIMIT)


def _rms(x, g):
    return x * lax.rsqrt(jnp.mean(x * x, axis=-1, keepdims=True) + RMS_EPS) * g


def _norm_matmul_kernel(x_ref, g_ref, w_ref, o_ref):
    y = _rms(x_ref[...], g_ref[...])
    o_ref[...] = jnp.dot(y.astype(BF16), w_ref[...],
                         preferred_element_type=F32).astype(o_ref.dtype)


def norm_matmul(x, gain, w, out_dtype):
    t, d = x.shape
    n = w.shape[1]
    return pl.pallas_call(
        _norm_matmul_kernel,
        grid=(t // TOK_TILE,),
        in_specs=[pl.BlockSpec((TOK_TILE, d), lambda i: (i, 0)),
                  pl.BlockSpec((1, d), lambda i: (0, 0)),
                  pl.BlockSpec((d, n), lambda i: (0, 0))],
        out_specs=pl.BlockSpec((TOK_TILE, n), lambda i: (i, 0)),
        out_shape=jax.ShapeDtypeStruct((t, n), out_dtype),
        compiler_params=_cparams(("parallel",)),
        name="norm_matmul",
    )(x, gain.reshape(1, d), w)


def _matmul_residual_norm_kernel(a_ref, w_ref, h_ref, g_ref, o_ref, xt_ref):
    h_new = h_ref[...] + jnp.dot(a_ref[...], w_ref[...], preferred_element_type=F32)
    o_ref[...] = h_new
    xt_ref[...] = _rms(h_new, g_ref[...]).T.astype(xt_ref.dtype)


def matmul_residual_norm(a, w, h, gain):
    t, k = a.shape
    d = w.shape[1]
    return pl.pallas_call(
        _matmul_residual_norm_kernel,
        grid=(t // TOK_TILE,),
        in_specs=[pl.BlockSpec((TOK_TILE, k), lambda i: (i, 0)),
                  pl.BlockSpec((k, d), lambda i: (0, 0)),
                  pl.BlockSpec((TOK_TILE, d), lambda i: (i, 0)),
                  pl.BlockSpec((1, d), lambda i: (0, 0))],
        out_specs=[pl.BlockSpec((TOK_TILE, d), lambda i: (i, 0)),
                   pl.BlockSpec((d, TOK_TILE), lambda i: (0, i))],
        out_shape=[jax.ShapeDtypeStruct((t, d), F32), jax.ShapeDtypeStruct((d, t), BF16)],
        compiler_params=_cparams(("parallel",)),
        name="matmul_residual_norm",
    )(a, w, h, gain.reshape(1, d))


def _attention_bias_tables(rel_bias, rows):
    n_h = rel_bias.shape[0]
    n_dr = 2 * MAX_WIN_ROWS - 1
    w = GRID_W
    edge = w - WIN_COLS
    line = jnp.pad(rel_bias, ((0, 0), (0, 0), (edge, edge + 1)), mode="edge")
    flat = jnp.tile(line, (1, 1, w))[:, :, w - 1:w - 1 + w * (2 * w - 1)]
    toep = flat.reshape(n_h, n_dr, w, 2 * w - 1)[:, :, :, :w]
    qc = np.arange(w)[:, None]
    kc = np.arange(w)[None, :]
    ws = np.clip(qc - WIN_COLS // 2, 0, w - WIN_COLS)
    toep = jnp.where(((kc >= ws) & (kc < ws + WIN_COLS))[None, None], toep, NEG_INF)
    pad = ATT_Q_ROWS - 1
    toep = jnp.pad(toep, ((0, 0), (pad, pad), (0, 0), (0, 0)), constant_values=NEG_INF)
    tabs = []
    for r0 in (0, ATT_Q_ROWS, rows - ATT_Q_ROWS):
        band = int(np.clip(r0 - MAX_WIN_ROWS // 2, 0, rows - ATT_BAND_ROWS))
        blocks = []
        for ql in range(ATT_Q_ROWS):
            lo = band - (r0 + ql) + MAX_WIN_ROWS - 1 + pad
            blk = toep[:, lo:lo + ATT_BAND_ROWS]
            blocks.append(blk.transpose(0, 2, 1, 3).reshape(n_h, w, ATT_BAND_ROWS * w))
        tab = jnp.stack(blocks, axis=1).reshape(n_h, ATT_Q_ROWS * w, ATT_BAND_ROWS * w)
        qr = r0 + np.arange(ATT_Q_ROWS)[:, None]
        kr = band + np.arange(ATT_BAND_ROWS)[None, :]
        rs = np.clip(qr - MAX_WIN_ROWS // 2, 0, rows - MAX_WIN_ROWS)
        row_ok = (kr >= rs) & (kr < rs + MAX_WIN_ROWS)
        row_ok = np.repeat(np.repeat(row_ok, w, axis=0), w, axis=1)
        tabs.append(jnp.where(row_ok[None], tab, NEG_INF))
    return jnp.stack(tabs)


def _attention_kernel(q_ref, k_ref, v_ref, *rest, rows):
    bias_refs, o_ref = rest[:ATT_BLOCKS_PER_STEP], rest[ATT_BLOCKS_PER_STEP]
    q_tok = ATT_Q_ROWS * GRID_W
    n_keys = ATT_BAND_ROWS * GRID_W
    scale = 1.0 / math.sqrt(HEAD_DIM)
    lane = lax.broadcasted_iota(jnp.int32, (q_tok, ATT_LANES), 1)
    for sub in range(ATT_BLOCKS_PER_STEP):
        rb = pl.program_id(2) * ATT_BLOCKS_PER_STEP + sub
        band = jnp.clip(rb * ATT_Q_ROWS - MAX_WIN_ROWS // 2, 0, rows - ATT_BAND_ROWS)
        start = pl.multiple_of(band * GRID_W, GRID_W)
        kb = k_ref[pl.ds(start, n_keys), :]
        vb = v_ref[pl.ds(start, n_keys), :]
        q = q_ref[sub * q_tok:(sub + 1) * q_tok, :]
        outs = []
        for hh in range(ATT_LANES // HEAD_DIM):
            in_head = (lane < HEAD_DIM) if hh == 0 else (lane >= HEAD_DIM)
            qm = jnp.where(in_head, q, jnp.zeros_like(q)) * jnp.asarray(scale, q.dtype)
            s = lax.dot_general(qm, kb, (((1,), (1,)), ((), ())), preferred_element_type=F32)
            s = s + bias_refs[sub][0, hh]
            m = jnp.max(s, axis=-1, keepdims=True)
            p = jnp.exp(s - m)
            l = jnp.sum(p, axis=-1, keepdims=True)
            o = jnp.dot(p.astype(BF16), vb, preferred_element_type=F32)
            outs.append(o / l)
        o_ref[sub * q_tok:(sub + 1) * q_tok, :] = jnp.where(lane < HEAD_DIM, outs[0],
                                                             outs[1]).astype(o_ref.dtype)


def neighbourhood_attention_core(qkv, bias_tabs, batch, seq):
    t = qkv.shape[0]
    d = qkv.shape[1] // 3
    rows = seq // GRID_W
    n_rb = rows // ATT_Q_ROWS
    n_steps = n_rb // ATT_BLOCKS_PER_STEP
    n_hp = d // ATT_LANES
    q_tok = ATT_Q_ROWS * GRID_W * ATT_BLOCKS_PER_STEP

    def bias_spec(sub):
        def index_map(b, hp, st):
            rb = st * ATT_BLOCKS_PER_STEP + sub
            return (jnp.where(rb == 0, 0, jnp.where(rb == n_rb - 1, 2, 1)), hp, 0, 0)
        return pl.BlockSpec((1, 2, ATT_Q_ROWS * GRID_W, ATT_BAND_ROWS * GRID_W), index_map)

    return pl.pallas_call(
        functools.partial(_attention_kernel, rows=rows),
        grid=(batch, n_hp, n_steps),
        in_specs=[pl.BlockSpec((q_tok, ATT_LANES), lambda b, hp, st: (b * n_steps + st, hp)),
                  pl.BlockSpec((seq, ATT_LANES), lambda b, hp, st: (b, n_hp + hp)),
                  pl.BlockSpec((seq, ATT_LANES), lambda b, hp, st: (b, 2 * n_hp + hp))]
                 + [bias_spec(sub) for sub in range(ATT_BLOCKS_PER_STEP)],
        out_specs=pl.BlockSpec((q_tok, ATT_LANES), lambda b, hp, st: (b * n_steps + st, hp)),
        out_shape=jax.ShapeDtypeStruct((t, d), BF16),
        compiler_params=_cparams(("parallel", "parallel", "arbitrary")),
        name="natten_core",
    )(qkv, qkv, qkv, *([bias_tabs] * ATT_BLOCKS_PER_STEP))


def _short_conv_kernel(b_ref, c_ref, v_ref, cp_ref, vp_ref, cn_ref, vn_ref, w_ref, o_ref, *, seq):
    i = pl.program_id(0)
    tm = b_ref.shape[0]
    halo = cp_ref.shape[0]
    u = c_ref[...].astype(F32) * v_ref[...].astype(F32)
    u_before = cp_ref[halo - 1:halo, :].astype(F32) * vp_ref[halo - 1:halo, :].astype(F32)
    u_after = cn_ref[0:1, :].astype(F32) * vn_ref[0:1, :].astype(F32)
    row = lax.broadcasted_iota(jnp.int32, u.shape, 0)
    pos = (i * tm + row) % seq
    u_prev = jnp.where(row == 0, u_before, pltpu.roll(u, 1, 0))
    u_prev = jnp.where(pos == 0, 0.0, u_prev)
    u_next = jnp.where(row == tm - 1, u_after, pltpu.roll(u, tm - 1, 0))
    u_next = jnp.where(pos == seq - 1, 0.0, u_next)
    conv = u_prev * w_ref[0:1, :] + u * w_ref[1:2, :] + u_next * w_ref[2:3, :]
    o_ref[...] = (b_ref[...].astype(F32) * conv).astype(o_ref.dtype)


def short_conv_gate(y, w_conv, seq):
    t = y.shape[0]
    d = y.shape[1] // 3
    cw = 512
    nj = d // cw
    tm = TOK_TILE
    halo = 16
    n_halo = t // halo

    def prev_map(off):
        return lambda i, j: (jnp.maximum(i * (tm // halo) - 1, 0), off * nj + j)

    def next_map(off):
        return lambda i, j: (jnp.minimum((i + 1) * (tm // halo), n_halo - 1), off * nj + j)

    return pl.pallas_call(
        functools.partial(_short_conv_kernel, seq=seq),
        grid=(t // tm, nj),
        in_specs=[pl.BlockSpec((tm, cw), lambda i, j: (i, j)),
                  pl.BlockSpec((tm, cw), lambda i, j: (i, nj + j)),
                  pl.BlockSpec((tm, cw), lambda i, j: (i, 2 * nj + j)),
                  pl.BlockSpec((halo, cw), prev_map(1)),
                  pl.BlockSpec((halo, cw), prev_map(2)),
                  pl.BlockSpec((halo, cw), next_map(1)),
                  pl.BlockSpec((halo, cw), next_map(2)),
                  pl.BlockSpec((3, cw), lambda i, j: (0, j))],
        out_specs=pl.BlockSpec((tm, cw), lambda i, j: (i, j)),
        out_shape=jax.ShapeDtypeStruct((t, d), BF16),
        compiler_params=_cparams(("parallel", "parallel")),
        name="short_conv_gate",
    )(y, y, y, y, y, y, y, w_conv)


def _top16(sc, want_rank):
    n, l = sc.shape
    iota_n = lax.broadcasted_iota(jnp.int32, (n, l), 0).astype(F32)
    iota_k = lax.broadcasted_iota(jnp.int32, (PEER_TOPK, l), 0).astype(F32)

    def body(r, carry):
        sc, rank, sv, si = carry
        rf = jnp.asarray(r, jnp.int32).astype(F32)
        m = jnp.max(sc, axis=0, keepdims=True)
        idx = jnp.min(jnp.where(sc == m, iota_n, float(n)), axis=0, keepdims=True)
        sel = iota_n == idx
        if want_rank:
            rank = jnp.where(sel, rf, rank)
        sc = jnp.where(sel, -jnp.inf, sc)
        sv = jnp.where(iota_k == rf, m, sv)
        si = jnp.where(iota_k == rf, idx, si)
        return sc, rank, sv, si

    init = (sc, jnp.full((n, l), float(PEER_TOPK), F32),
            jnp.zeros((PEER_TOPK, l), F32), jnp.zeros((PEER_TOPK, l), F32))
    _, rank, sv, si = lax.fori_loop(0, PEER_TOPK, body, init)
    return sv, si, rank


def _merge_top16(sv1, sv2):
    k = sv1.shape[0]
    iota_k = lax.broadcasted_iota(jnp.int32, sv1.shape, 0).astype(F32)
    top = sv1[0:1] + sv2[0:1]

    def body(_, carry):
        front, cnt, z = carry
        m = jnp.max(front, axis=0, keepdims=True)
        a_star = jnp.min(jnp.where(front == m, iota_k, float(k)), axis=0, keepdims=True)
        sel = iota_k == a_star
        z = z + jnp.exp(m - top)
        cnt = jnp.where(sel, cnt + 1.0, cnt)
        c_sel = jnp.sum(jnp.where(sel, cnt, 0.0), axis=0, keepdims=True)
        nxt = jnp.sum(jnp.where(iota_k == c_sel, sv2, 0.0), axis=0, keepdims=True)
        nxt = jnp.where(c_sel >= float(k), -jnp.inf, nxt)
        front = jnp.where(sel, sv1 + nxt, front)
        return front, cnt, z

    init = (sv1 + sv2[0:1], jnp.zeros(sv1.shape, F32), jnp.zeros(top.shape, F32))
    _, cnt, z = lax.fori_loop(0, PEER_TOPK, body, init)
    return cnt, z


def _tree(op, xs):
    while len(xs) > 1:
        xs = [op(xs[k], xs[k + 1]) for k in range(0, len(xs) - 1, 2)] + (xs[-1:] if len(xs) % 2 else [])
    return xs[0]


def _all_sublanes(op, x):
    for shift in (4, 2, 1):
        x = op(x, pltpu.roll(x, shift, 0))
    return x


def _count_greater(top, x):
    sel = jnp.where
    c8 = top[7] > x
    c4 = sel(c8, top[11], top[3]) > x
    c2 = sel(c8, sel(c4, top[13], top[9]), sel(c4, top[5], top[1])) > x
    c1 = sel(c8, sel(c4, sel(c2, top[14], top[12]), sel(c2, top[10], top[8])),
             sel(c4, sel(c2, top[6], top[4]), sel(c2, top[2], top[0]))) > x
    c16 = top[15] > x
    zero = jnp.zeros_like(x)
    return ((sel(c8, 8.0, zero) + sel(c4, 4.0, zero)) + (sel(c2, 2.0, zero) + sel(c1, 1.0, zero))
            + sel(c16, 1.0, zero))


def _sorting_network(n):
    def merge(lo, hi, r):
        step = r * 2
        if step < hi - lo:
            yield from merge(lo, hi, step)
            yield from merge(lo + r, hi, step)
            yield from [(k, k + r) for k in range(lo + r, hi - r, step)]
        else:
            yield (lo, lo + r)

    def sort(lo, hi):
        if hi - lo >= 1:
            mid = lo + (hi - lo) // 2
            yield from sort(lo, mid)
            yield from sort(mid + 1, hi)
            yield from merge(lo, hi, 1)

    return tuple(sort(0, n - 1))


_SORT16 = _sorting_network(PEER_TOPK)


def _compare_exchange(xs, i, j):
    xs[i], xs[j] = jnp.maximum(xs[i], xs[j]), jnp.minimum(xs[i], xs[j])


def _top16_sorted(rows):
    xs = list(rows)
    for i, j in _SORT16:
        _compare_exchange(xs, i, j)
    k = len(xs)
    for shift in (4, 2, 1):
        ys = [pltpu.roll(x, shift, 0) for x in xs]
        xs = [jnp.maximum(xs[v], ys[k - 1 - v]) for v in range(k)]
        stride = k // 2
        while stride:
            for v in range(k):
                if not v & stride:
                    _compare_exchange(xs, v, v + stride)
            stride //= 2
    return xs


def _peer_scores(h, keys_ref, q_scr, s_scr):
    for p in range(2):
        r0 = pl.multiple_of((2 * h + p) * PEER_HALF_DIM, PEER_HALF_DIM)
        s_scr[h, p] = jnp.dot(keys_ref[2 * h + p], q_scr[pl.ds(r0, PEER_HALF_DIM), :],
                              preferred_element_type=F32)


def _peer_head_exact(h, s_scr, r2_ref, e2_ref, n_ref, e1_ref):
    n_keys, lanes = PEER_N_KEYS, 128
    iota_n = lax.broadcasted_iota(jnp.int32, (n_keys, lanes), 0).astype(F32)
    for lt in range(s_scr.shape[3] // lanes):
        ls = slice(lt * lanes, (lt + 1) * lanes)
        s1 = s_scr[h, 0, :, ls]
        s2 = s_scr[h, 1, :, ls]
        sv1, si1, _ = _top16(s1, want_rank=False)
        sv2, _, rank2 = _top16(s2, want_rank=True)
        cnt, z = _merge_top16(sv1, sv2)
        n_sel = jnp.zeros((n_keys, lanes), F32)
        for a in range(PEER_TOPK):
            n_sel = jnp.where(iota_n == si1[a:a + 1, :], cnt[a:a + 1, :], n_sel)
        r2_ref[h, :, ls] = rank2.astype(r2_ref.dtype)
        e2_ref[h, :, ls] = jnp.exp(s2 - sv2[0:1, :]).astype(e2_ref.dtype)
        n_ref[h, :, ls] = n_sel
        e1_ref[h, :, ls] = jnp.exp(s1 - sv1[0:1, :]) * (GELU_HALF / z)


def _peer_head_top_values(h, s_scr, t_scr, sv_scr):
    n_keys, lanes, sub = PEER_N_KEYS, 128, 8

    head_row = lax.broadcasted_iota(jnp.int32, (sub, lanes), 0) == h

    def lane_tile(lt, _):
        ls = pl.ds(pl.multiple_of(lt * lanes, lanes), lanes)
        for p in range(2):
            rows = [s_scr[h, p, v * sub:(v + 1) * sub, ls] for v in range(n_keys // sub)]
            top = _top16_sorted(rows)
            for r in range(PEER_TOPK):
                t_scr[h, p, r, :, ls] = top[r]
                sv_scr[p, r, :, ls] = jnp.where(head_row, top[r], sv_scr[p, r, :, ls])
        return 0

    lax.fori_loop(0, s_scr.shape[3] // lanes, lane_tile, 0)


def _peer_head_distinct(h, s_scr, t_scr, cnt_scr, z_scr, r2_ref, e2_ref, n_ref, e1_ref):
    n_keys, lanes, sub = PEER_N_KEYS, 128, 8
    n_vr = n_keys // sub
    one = jnp.ones((sub, lanes), F32)
    zero = jnp.zeros((sub, lanes), F32)
    bad = zero
    cnt_rows = [cnt_scr[a, pl.ds(h, 1), :] for a in range(PEER_TOPK)]
    inv_row = GELU_HALF / z_scr[pl.ds(h, 1), :]
    for lt in range(s_scr.shape[3] // lanes):
        ls = slice(lt * lanes, (lt + 1) * lanes)
        for p in (1, 0):
            top = [t_scr[h, p, b, :, ls] for b in range(PEER_TOPK)]
            tied = _tree(jnp.maximum, [jnp.where(top[b] > top[b + 1], zero, one)
                                       for b in range(PEER_TOPK - 1)])
            n_top = zero
            if p == 1:
                for v in range(n_vr):
                    vs = slice(v * sub, (v + 1) * sub)
                    x = s_scr[h, 1, vs, ls]
                    rank = _count_greater(top, x)
                    r2_ref[h, vs, ls] = rank.astype(r2_ref.dtype)
                    e2_ref[h, vs, ls] = jnp.exp(x - top[0]).astype(e2_ref.dtype)
                    n_top = n_top + jnp.where(rank < float(PEER_TOPK), one, zero)
            else:
                cnt_b = [jnp.broadcast_to(r[:, ls], (sub, lanes)) for r in cnt_rows]
                inv_b = jnp.broadcast_to(inv_row[:, ls], (sub, lanes))
                for v in range(n_vr):
                    vs = slice(v * sub, (v + 1) * sub)
                    x = s_scr[h, 0, vs, ls]
                    n_sel = zero
                    for a in range(PEER_TOPK):
                        n_sel = jnp.where(x == top[a], cnt_b[a], n_sel)
                    n_ref[h, vs, ls] = n_sel
                    e1_ref[h, vs, ls] = jnp.exp(x - top[0]) * inv_b
                    n_top = n_top + jnp.where(x >= top[PEER_TOPK - 1], one, zero)
            miscount = jnp.where(_all_sublanes(jnp.add, n_top) != float(PEER_TOPK), one, zero)
            bad = jnp.maximum(bad, jnp.maximum(tied, miscount))
    return bad


def _peer_weights_kernel(xt_ref, wq_ref, keys_ref, r2_ref, e2_ref, n_ref, e1_ref,
                         q_scr, s_scr, t_scr, sv_scr, cnt_scr, z_scr):
    q_scr[...] = jnp.dot(wq_ref[...], xt_ref[...], preferred_element_type=F32).astype(BF16)
    sv_scr[...] = jnp.zeros_like(sv_scr)

    def top_values(h, _):
        _peer_scores(h, keys_ref, q_scr, s_scr)
        _peer_head_top_values(h, s_scr, t_scr, sv_scr)
        return 0

    lax.fori_loop(0, PEER_HEADS, top_values, 0)
    cnt, z = _merge_top16(sv_scr[0], sv_scr[1])
    cnt_scr[...] = cnt
    z_scr[...] = z[0]

    def weights(h, _):
        bad = _peer_head_distinct(h, s_scr, t_scr, cnt_scr, z_scr, r2_ref, e2_ref, n_ref, e1_ref)

        @pl.when(jnp.max(bad) > 0.0)
        def _():
            _peer_head_exact(h, s_scr, r2_ref, e2_ref, n_ref, e1_ref)

        return 0

    lax.fori_loop(0, PEER_HEADS, weights, 0)


def peer_weights(xt, wq_t, keys):
    d, t = xt.shape
    tm = PEER_PRE_TILE
    out16 = jax.ShapeDtypeStruct((PEER_HEADS, PEER_N_KEYS, t), BF16)
    out32 = jax.ShapeDtypeStruct((PEER_HEADS, PEER_N_KEYS, t), F32)
    ospec = pl.BlockSpec((PEER_HEADS, PEER_N_KEYS, tm), lambda i: (0, 0, i))
    return pl.pallas_call(
        _peer_weights_kernel,
        grid=(t // tm,),
        in_specs=[pl.BlockSpec((d, tm), lambda i: (0, i)),
                  pl.BlockSpec(wq_t.shape, lambda i: (0, 0)),
                  pl.BlockSpec(keys.shape, lambda i: (0, 0, 0))],
        out_specs=[ospec, ospec, ospec, ospec],
        out_shape=[out16, out16, out32, out32],
        scratch_shapes=[pltpu.VMEM((wq_t.shape[0], tm), BF16),
                        pltpu.VMEM((PEER_HEADS, 2, PEER_N_KEYS, tm), F32),
                        pltpu.VMEM((PEER_HEADS, 2, PEER_TOPK, 8, tm), F32),
                        pltpu.VMEM((2, PEER_TOPK, PEER_HEADS, tm), F32),
                        pltpu.VMEM((PEER_TOPK, PEER_HEADS, tm), F32),
                        pltpu.VMEM((PEER_HEADS, tm), F32)],
        compiler_params=_cparams(("parallel",)),
        name="peer_weights",
    )(xt, wq_t, keys)


def _peer_hidden_rows(i, il, a_ref, hid_ref, r2_ref, e2_ref, n_ref, e1_ref):
    nk, lanes, pack = PEER_N_KEYS, 128, 16
    row0 = il * nk
    n_rows = [n_ref[h, pl.ds(i, 1), :] for h in range(PEER_HEADS)]
    e1_rows = [e1_ref[h, pl.ds(i, 1), :] for h in range(PEER_HEADS)]
    zero = jnp.zeros((pack, lanes), BF16)
    for lt in range(a_ref.shape[1] // lanes):
        ls = slice(lt * lanes, (lt + 1) * lanes)
        n_b = [jnp.broadcast_to(r[:, ls], (pack, lanes)).astype(BF16) for r in n_rows]
        e1_b = [jnp.broadcast_to(r[:, ls], (pack, lanes)).astype(BF16) for r in e1_rows]
        for jb in range(nk // pack):
            js = slice(jb * pack, (jb + 1) * pack)
            g = None
            for h in range(PEER_HEADS):
                term = jnp.where(r2_ref[h, js, ls] < n_b[h], e2_ref[h, js, ls] * e1_b[h], zero)
                g = term if g is None else g + term
            rs = slice(row0 + jb * pack, row0 + (jb + 1) * pack)
            a = a_ref[rs, ls]
            act = a * (1.0 + lax.erf(a * math.sqrt(0.5)))
            hid_ref[rs, ls] = act.astype(BF16) * g


def _peer_dense_kernel(xt_ref, r2_ref, e2_ref, n_ref, e1_ref, wup_ref, wdn_ref, h_ref, g_ref, o_ref,
                       acc_ref, a0_ref, a1_ref, hid0_ref, hid1_ref, r2_scr, e2_scr, *, out_norm):
    e = pl.program_id(1)
    grp = a0_ref.shape[0]
    rows_per_grp = grp // PEER_N_KEYS
    n_grp = wup_ref.shape[0] // grp

    @pl.when(e == 0)
    def _():
        acc_ref[...] = jnp.zeros_like(acc_ref)
        r2_scr[...] = r2_ref[...]
        e2_scr[...] = e2_ref[...]

    a_refs = (a0_ref, a1_ref)
    hid_refs = (hid0_ref, hid1_ref)
    for p in range(n_grp + 2):
        if p < n_grp:
            a_refs[p % 2][...] = jnp.dot(wup_ref[p * grp:(p + 1) * grp, :], xt_ref[...],
                                         preferred_element_type=F32)
        if 1 <= p <= n_grp:
            g = p - 1
            for il in range(rows_per_grp):
                _peer_hidden_rows((e * n_grp + g) * rows_per_grp + il, il, a_refs[g % 2],
                                  hid_refs[g % 2], r2_scr, e2_scr, n_ref, e1_ref)
        if p >= 2:
            g = p - 2
            acc_ref[...] += lax.dot_general(hid_refs[g % 2][...], wdn_ref[g * grp:(g + 1) * grp, :],
                                            (((0,), (0,)), ((), ())), preferred_element_type=F32)

    @pl.when(e == pl.num_programs(1) - 1)
    def _():
        h_new = h_ref[...] + acc_ref[...]
        o_ref[...] = _rms(h_new, g_ref[...]) if out_norm else h_new


def peer_dense(xt, r2, e2, n_sel, e1n, w_up, w_down, layer, h, out_gain, out_norm):
    d, t = xt.shape
    n_exp = w_up.shape[1]
    tm, te = PEER_TOK_TILE, PEER_EXP_TILE
    wspec = pl.BlockSpec((PEER_HEADS, PEER_N_KEYS, tm), lambda i, e: (0, 0, i))
    return pl.pallas_call(
        functools.partial(_peer_dense_kernel, out_norm=out_norm),
        grid=(t // tm, n_exp // te),
        in_specs=[pl.BlockSpec((d, tm), lambda i, e: (0, i)),
                  wspec, wspec, wspec, wspec,
                  pl.BlockSpec((None, te, d), lambda i, e: (layer, e, 0)),
                  pl.BlockSpec((None, te, d), lambda i, e: (layer, e, 0)),
                  pl.BlockSpec((tm, d), lambda i, e: (i, 0)),
                  pl.BlockSpec((1, d), lambda i, e: (0, 0))],
        out_specs=pl.BlockSpec((tm, d), lambda i, e: (i, 0)),
        out_shape=jax.ShapeDtypeStruct((t, d), F32),
        scratch_shapes=[pltpu.VMEM((tm, d), F32),
                        pltpu.VMEM((PEER_STAGE_ROWS, tm), F32),
                        pltpu.VMEM((PEER_STAGE_ROWS, tm), F32),
                        pltpu.VMEM((PEER_STAGE_ROWS, tm), BF16),
                        pltpu.VMEM((PEER_STAGE_ROWS, tm), BF16),
                        pltpu.VMEM((PEER_HEADS, PEER_N_KEYS, tm), BF16),
                        pltpu.VMEM((PEER_HEADS, PEER_N_KEYS, tm), BF16)],
        compiler_params=_cparams(("parallel", "arbitrary")),
        name="peer_dense",
    )(xt, r2, e2, n_sel, e1n, w_up, w_down, h, out_gain.reshape(1, d))


def kernel(x, norm_mix, norm_ffn, norm_final, attn_w_qkv, attn_w_o, attn_rel_bias,
           conv_w_in, conv_w_conv, conv_w_out, peer_w_query, peer_sub_keys,
           peer_w_up, peer_w_down):
    batch, seq, d = x.shape
    depth = norm_mix.shape[0]
    rows = seq // GRID_W
    h = x.reshape(batch * seq, d)
    w_up_all = peer_w_up.astype(BF16)
    w_down_all = peer_w_down.astype(BF16)
    for i in range(depth):
        j = i // 2
        if i % 2 == 0:
            qkv = norm_matmul(h, norm_mix[i], attn_w_qkv[j].astype(BF16), BF16)
            bias_tabs = _attention_bias_tables(attn_rel_bias[j], rows)
            mixed = neighbourhood_attention_core(qkv, bias_tabs, batch, seq)
            w_out = attn_w_o[j]
        else:
            y = norm_matmul(h, norm_mix[i], conv_w_in[j].astype(BF16), BF16)
            mixed = short_conv_gate(y, conv_w_conv[j], seq)
            w_out = conv_w_out[j]
        h, xt = matmul_residual_norm(mixed, w_out.astype(BF16), h, norm_ffn[i])
        keys = peer_sub_keys[i].reshape(PEER_HEADS * 2, PEER_N_KEYS, PEER_HALF_DIM).astype(BF16)
        r2, e2, n_sel, e1n = peer_weights(xt, peer_w_query[i].T.astype(BF16), keys)
        h = peer_dense(xt, r2, e2, n_sel, e1n, w_up_all, w_down_all, i, h, norm_final,
                       out_norm=(i == depth - 1))
    return h.reshape(batch, seq, d)
```

```python
import functools
import math

import numpy as np
import jax
import jax.numpy as jnp
from jax import lax
from jax.experimental import pallas as pl
from jax.experimental.pallas import tpu as pltpu

F32 = jnp.float32
BF16 = jnp.bfloat16

RMS_EPS = 1e-6
NEG_INF = -1e30

GRID_W = 64
N_HEADS = 16
HEAD_DIM = 64
MAX_WIN_ROWS = 8
WIN_COLS = 16
ATT_Q_ROWS = 4
ATT_BAND_ROWS = ATT_Q_ROWS + MAX_WIN_ROWS - 1
ATT_LANES = 128
ATT_BLOCKS_PER_STEP = 4

PEER_HEADS = 8
PEER_N_KEYS = 128
PEER_TOPK = 16
PEER_HALF_DIM = 128
GELU_HALF = 0.5

TOK_TILE = 512
PEER_PRE_TILE = 512
PEER_TOK_TILE = 512
PEER_EXP_TILE = 2048
PEER_STAGE_ROWS = 256
VMEM_LIMIT = 56 * 1024 * 1024


def _cparams(sem):
    return pltpu.CompilerParams(dimension_semantics=sem, vmem_limit_bytes=VMEM_LIMIT)


def _rms(x, g):
    return x * lax.rsqrt(jnp.mean(x * x, axis=-1, keepdims=True) + RMS_EPS) * g


def _norm_matmul_kernel(x_ref, g_ref, w_ref, o_ref):
    y = _rms(x_ref[...], g_ref[...])
    o_ref[...] = jnp.dot(y.astype(BF16), w_ref[...],
                         preferred_element_type=F32).astype(o_ref.dtype)


def norm_matmul(x, gain, w, out_dtype):
    t, d = x.shape
    n = w.shape[1]
    return pl.pallas_call(
        _norm_matmul_kernel,
        grid=(t // TOK_TILE,),
        in_specs=[pl.BlockSpec((TOK_TILE, d), lambda i: (i, 0)),
                  pl.BlockSpec((1, d), lambda i: (0, 0)),
                  pl.BlockSpec((d, n), lambda i: (0, 0))],
        out_specs=pl.BlockSpec((TOK_TILE, n), lambda i: (i, 0)),
        out_shape=jax.ShapeDtypeStruct((t, n), out_dtype),
        compiler_params=_cparams(("parallel",)),
        name="norm_matmul",
    )(x, gain.reshape(1, d), w)


def _matmul_residual_norm_kernel(a_ref, w_ref, h_ref, g_ref, o_ref, xt_ref):
    h_new = h_ref[...] + jnp.dot(a_ref[...], w_ref[...], preferred_element_type=F32)
    o_ref[...] = h_new
    xt_ref[...] = _rms(h_new, g_ref[...]).T.astype(xt_ref.dtype)


def matmul_residual_norm(a, w, h, gain):
    t, k = a.shape
    d = w.shape[1]
    return pl.pallas_call(
        _matmul_residual_norm_kernel,
        grid=(t // TOK_TILE,),
        in_specs=[pl.BlockSpec((TOK_TILE, k), lambda i: (i, 0)),
                  pl.BlockSpec((k, d), lambda i: (0, 0)),
                  pl.BlockSpec((TOK_TILE, d), lambda i: (i, 0)),
                  pl.BlockSpec((1, d), lambda i: (0, 0))],
        out_specs=[pl.BlockSpec((TOK_TILE, d), lambda i: (i, 0)),
                   pl.BlockSpec((d, TOK_TILE), lambda i: (0, i))],
        out_shape=[jax.ShapeDtypeStruct((t, d), F32), jax.ShapeDtypeStruct((d, t), BF16)],
        compiler_params=_cparams(("parallel",)),
        name="matmul_residual_norm",
    )(a, w, h, gain.reshape(1, d))


def _attention_bias_tables(rel_bias, rows):
    n_h = rel_bias.shape[0]
    n_dr = 2 * MAX_WIN_ROWS - 1
    w = GRID_W
    edge = w - WIN_COLS
    line = jnp.pad(rel_bias, ((0, 0), (0, 0), (edge, edge + 1)), mode="edge")
    flat = jnp.tile(line, (1, 1, w))[:, :, w - 1:w - 1 + w * (2 * w - 1)]
    toep = flat.reshape(n_h, n_dr, w, 2 * w - 1)[:, :, :, :w]
    qc = np.arange(w)[:, None]
    kc = np.arange(w)[None, :]
    ws = np.clip(qc - WIN_COLS // 2, 0, w - WIN_COLS)
    toep = jnp.where(((kc >= ws) & (kc < ws + WIN_COLS))[None, None], toep, NEG_INF)
    pad = ATT_Q_ROWS - 1
    toep = jnp.pad(toep, ((0, 0), (pad, pad), (0, 0), (0, 0)), constant_values=NEG_INF)
    tabs = []
    for r0 in (0, ATT_Q_ROWS, rows - ATT_Q_ROWS):
        band = int(np.clip(r0 - MAX_WIN_ROWS // 2, 0, rows - ATT_BAND_ROWS))
        blocks = []
        for ql in range(ATT_Q_ROWS):
            lo = band - (r0 + ql) + MAX_WIN_ROWS - 1 + pad
            blk = toep[:, lo:lo + ATT_BAND_ROWS]
            blocks.append(blk.transpose(0, 2, 1, 3).reshape(n_h, w, ATT_BAND_ROWS * w))
        tab = jnp.stack(blocks, axis=1).reshape(n_h, ATT_Q_ROWS * w, ATT_BAND_ROWS * w)
        qr = r0 + np.arange(ATT_Q_ROWS)[:, None]
        kr = band + np.arange(ATT_BAND_ROWS)[None, :]
        rs = np.clip(qr - MAX_WIN_ROWS // 2, 0, rows - MAX_WIN_ROWS)
        row_ok = (kr >= rs) & (kr < rs + MAX_WIN_ROWS)
        row_ok = np.repeat(np.repeat(row_ok, w, axis=0), w, axis=1)
        tabs.append(jnp.where(row_ok[None], tab, NEG_INF))
    return jnp.stack(tabs)


def _attention_kernel(q_ref, k_ref, v_ref, *rest, rows):
    bias_refs, o_ref = rest[:ATT_BLOCKS_PER_STEP], rest[ATT_BLOCKS_PER_STEP]
    q_tok = ATT_Q_ROWS * GRID_W
    n_keys = ATT_BAND_ROWS * GRID_W
    scale = 1.0 / math.sqrt(HEAD_DIM)
    lane = lax.broadcasted_iota(jnp.int32, (q_tok, ATT_LANES), 1)
    for sub in range(ATT_BLOCKS_PER_STEP):
        rb = pl.program_id(2) * ATT_BLOCKS_PER_STEP + sub
        band = jnp.clip(rb * ATT_Q_ROWS - MAX_WIN_ROWS // 2, 0, rows - ATT_BAND_ROWS)
        start = pl.multiple_of(band * GRID_W, GRID_W)
        kb = k_ref[pl.ds(start, n_keys), :]
        vb = v_ref[pl.ds(start, n_keys), :]
        q = q_ref[sub * q_tok:(sub + 1) * q_tok, :]
        outs = []
        for hh in range(ATT_LANES // HEAD_DIM):
            in_head = (lane < HEAD_DIM) if hh == 0 else (lane >= HEAD_DIM)
            qm = jnp.where(in_head, q, jnp.zeros_like(q)) * jnp.asarray(scale, q.dtype)
            s = lax.dot_general(qm, kb, (((1,), (1,)), ((), ())), preferred_element_type=F32)
            s = s + bias_refs[sub][0, hh]
            m = jnp.max(s, axis=-1, keepdims=True)
            p = jnp.exp(s - m)
            l = jnp.sum(p, axis=-1, keepdims=True)
            o = jnp.dot(p.astype(BF16), vb, preferred_element_type=F32)
            outs.append(o / l)
        o_ref[sub * q_tok:(sub + 1) * q_tok, :] = jnp.where(lane < HEAD_DIM, outs[0],
                                                             outs[1]).astype(o_ref.dtype)


def neighbourhood_attention_core(qkv, bias_tabs, batch, seq):
    t = qkv.shape[0]
    d = qkv.shape[1] // 3
    rows = seq // GRID_W
    n_rb = rows // ATT_Q_ROWS
    n_steps = n_rb // ATT_BLOCKS_PER_STEP
    n_hp = d // ATT_LANES
    q_tok = ATT_Q_ROWS * GRID_W * ATT_BLOCKS_PER_STEP

    def bias_spec(sub):
        def index_map(b, hp, st):
            rb = st * ATT_BLOCKS_PER_STEP + sub
            return (jnp.where(rb == 0, 0, jnp.where(rb == n_rb - 1, 2, 1)), hp, 0, 0)
        return pl.BlockSpec((1, 2, ATT_Q_ROWS * GRID_W, ATT_BAND_ROWS * GRID_W), index_map)

    return pl.pallas_call(
        functools.partial(_attention_kernel, rows=rows),
        grid=(batch, n_hp, n_steps),
        in_specs=[pl.BlockSpec((q_tok, ATT_LANES), lambda b, hp, st: (b * n_steps + st, hp)),
                  pl.BlockSpec((seq, ATT_LANES), lambda b, hp, st: (b, n_hp + hp)),
                  pl.BlockSpec((seq, ATT_LANES), lambda b, hp, st: (b, 2 * n_hp + hp))]
                 + [bias_spec(sub) for sub in range(ATT_BLOCKS_PER_STEP)],
        out_specs=pl.BlockSpec((q_tok, ATT_LANES), lambda b, hp, st: (b * n_steps + st, hp)),
        out_shape=jax.ShapeDtypeStruct((t, d), BF16),
        compiler_params=_cparams(("parallel", "parallel", "arbitrary")),
        name="natten_core",
    )(qkv, qkv, qkv, *([bias_tabs] * ATT_BLOCKS_PER_STEP))


def _short_conv_kernel(b_ref, c_ref, v_ref, cp_ref, vp_ref, cn_ref, vn_ref, w_ref, o_ref, *, seq):
    i = pl.program_id(0)
    tm = b_ref.shape[0]
    halo = cp_ref.shape[0]
    u = c_ref[...].astype(F32) * v_ref[...].astype(F32)
    u_before = cp_ref[halo - 1:halo, :].astype(F32) * vp_ref[halo - 1:halo, :].astype(F32)
    u_after = cn_ref[0:1, :].astype(F32) * vn_ref[0:1, :].astype(F32)
    row = lax.broadcasted_iota(jnp.int32, u.shape, 0)
    pos = (i * tm + row) % seq
    u_prev = jnp.where(row == 0, u_before, pltpu.roll(u, 1, 0))
    u_prev = jnp.where(pos == 0, 0.0, u_prev)
    u_next = jnp.where(row == tm - 1, u_after, pltpu.roll(u, tm - 1, 0))
    u_next = jnp.where(pos == seq - 1, 0.0, u_next)
    conv = u_prev * w_ref[0:1, :] + u * w_ref[1:2, :] + u_next * w_ref[2:3, :]
    o_ref[...] = (b_ref[...].astype(F32) * conv).astype(o_ref.dtype)


def short_conv_gate(y, w_conv, seq):
    t = y.shape[0]
    d = y.shape[1] // 3
    cw = 512
    nj = d // cw
    tm = TOK_TILE
    halo = 16
    n_halo = t // halo

    def prev_map(off):
        return lambda i, j: (jnp.maximum(i * (tm // halo) - 1, 0), off * nj + j)

    def next_map(off):
        return lambda i, j: (jnp.minimum((i + 1) * (tm // halo), n_halo - 1), off * nj + j)

    return pl.pallas_call(
        functools.partial(_short_conv_kernel, seq=seq),
        grid=(t // tm, nj),
        in_specs=[pl.BlockSpec((tm, cw), lambda i, j: (i, j)),
                  pl.BlockSpec((tm, cw), lambda i, j: (i, nj + j)),
                  pl.BlockSpec((tm, cw), lambda i, j: (i, 2 * nj + j)),
                  pl.BlockSpec((halo, cw), prev_map(1)),
                  pl.BlockSpec((halo, cw), prev_map(2)),
                  pl.BlockSpec((halo, cw), next_map(1)),
                  pl.BlockSpec((halo, cw), next_map(2)),
                  pl.BlockSpec((3, cw), lambda i, j: (0, j))],
        out_specs=pl.BlockSpec((tm, cw), lambda i, j: (i, j)),
        out_shape=jax.ShapeDtypeStruct((t, d), BF16),
        compiler_params=_cparams(("parallel", "parallel")),
        name="short_conv_gate",
    )(y, y, y, y, y, y, y, w_conv)


def _top16(sc, want_rank):
    n, l = sc.shape
    iota_n = lax.broadcasted_iota(jnp.int32, (n, l), 0).astype(F32)
    iota_k = lax.broadcasted_iota(jnp.int32, (PEER_TOPK, l), 0).astype(F32)

    def body(r, carry):
        sc, rank, sv, si = carry
        rf = jnp.asarray(r, jnp.int32).astype(F32)
        m = jnp.max(sc, axis=0, keepdims=True)
        idx = jnp.min(jnp.where(sc == m, iota_n, float(n)), axis=0, keepdims=True)
        sel = iota_n == idx
        if want_rank:
            rank = jnp.where(sel, rf, rank)
        sc = jnp.where(sel, -jnp.inf, sc)
        sv = jnp.where(iota_k == rf, m, sv)
        si = jnp.where(iota_k == rf, idx, si)
        return sc, rank, sv, si

    init = (sc, jnp.full((n, l), float(PEER_TOPK), F32),
            jnp.zeros((PEER_TOPK, l), F32), jnp.zeros((PEER_TOPK, l), F32))
    _, rank, sv, si = lax.fori_loop(0, PEER_TOPK, body, init)
    return sv, si, rank


def _merge_top16(sv1, sv2):
    k = sv1.shape[0]
    iota_k = lax.broadcasted_iota(jnp.int32, sv1.shape, 0).astype(F32)
    top = sv1[0:1] + sv2[0:1]

    def body(_, carry):
        front, cnt, z = carry
        m = jnp.max(front, axis=0, keepdims=True)
        a_star = jnp.min(jnp.where(front == m, iota_k, float(k)), axis=0, keepdims=True)
        sel = iota_k == a_star
        z = z + jnp.exp(m - top)
        cnt = jnp.where(sel, cnt + 1.0, cnt)
        c_sel = jnp.sum(jnp.where(sel, cnt, 0.0), axis=0, keepdims=True)
        nxt = jnp.sum(jnp.where(iota_k == c_sel, sv2, 0.0), axis=0, keepdims=True)
        nxt = jnp.where(c_sel >= float(k), -jnp.inf, nxt)
        front = jnp.where(sel, sv1 + nxt, front)
        return front, cnt, z

    init = (sv1 + sv2[0:1], jnp.zeros(sv1.shape, F32), jnp.zeros(top.shape, F32))
    _, cnt, z = lax.fori_loop(0, PEER_TOPK, body, init)
    return cnt, z


def _tree(op, xs):
    while len(xs) > 1:
        xs = [op(xs[k], xs[k + 1]) for k in range(0, len(xs) - 1, 2)] + (xs[-1:] if len(xs) % 2 else [])
    return xs[0]


def _all_sublanes(op, x):
    for shift in (4, 2, 1):
        x = op(x, pltpu.roll(x, shift, 0))
    return x


def _count_greater(top, x):
    sel = jnp.where
    c8 = top[7] > x
    c4 = sel(c8, top[11], top[3]) > x
    c2 = sel(c8, sel(c4, top[13], top[9]), sel(c4, top[5], top[1])) > x
    c1 = sel(c8, sel(c4, sel(c2, top[14], top[12]), sel(c2, top[10], top[8])),
             sel(c4, sel(c2, top[6], top[4]), sel(c2, top[2], top[0]))) > x
    c16 = top[15] > x
    zero = jnp.zeros_like(x)
    return ((sel(c8, 8.0, zero) + sel(c4, 4.0, zero)) + (sel(c2, 2.0, zero) + sel(c1, 1.0, zero))
            + sel(c16, 1.0, zero))


def _sorting_network(n):
    def merge(lo, hi, r):
        step = r * 2
        if step < hi - lo:
            yield from merge(lo, hi, step)
            yield from merge(lo + r, hi, step)
            yield from [(k, k + r) for k in range(lo + r, hi - r, step)]
        else:
            yield (lo, lo + r)

    def sort(lo, hi):
        if hi - lo >= 1:
            mid = lo + (hi - lo) // 2
            yield from sort(lo, mid)
            yield from sort(mid + 1, hi)
            yield from merge(lo, hi, 1)

    return tuple(sort(0, n - 1))


_SORT16 = _sorting_network(PEER_TOPK)


def _compare_exchange(xs, i, j):
    xs[i], xs[j] = jnp.maximum(xs[i], xs[j]), jnp.minimum(xs[i], xs[j])


def _top16_sorted(rows):
    xs = list(rows)
    for i, j in _SORT16:
        _compare_exchange(xs, i, j)
    k = len(xs)
    for shift in (4, 2, 1):
        ys = [pltpu.roll(x, shift, 0) for x in xs]
        xs = [jnp.maximum(xs[v], ys[k - 1 - v]) for v in range(k)]
        stride = k // 2
        while stride:
            for v in range(k):
                if not v & stride:
                    _compare_exchange(xs, v, v + stride)
            stride //= 2
    return xs


def _peer_scores(h, keys_ref, q_scr, s_scr):
    for p in range(2):
        r0 = pl.multiple_of((2 * h + p) * PEER_HALF_DIM, PEER_HALF_DIM)
        s_scr[h, p] = jnp.dot(keys_ref[2 * h + p], q_scr[pl.ds(r0, PEER_HALF_DIM), :],
                              preferred_element_type=F32)


def _peer_head_exact(h, lt, s_scr, r2_ref, e2_ref, n_ref, e1_ref):
    n_keys, lanes = PEER_N_KEYS, 128
    iota_n = lax.broadcasted_iota(jnp.int32, (n_keys, lanes), 0).astype(F32)
    ls = slice(lt * lanes, (lt + 1) * lanes)
    s1 = s_scr[h, 0, :, ls]
    s2 = s_scr[h, 1, :, ls]
    sv1, si1, _ = _top16(s1, want_rank=False)
    sv2, _, rank2 = _top16(s2, want_rank=True)
    cnt, z = _merge_top16(sv1, sv2)
    n_sel = jnp.zeros((n_keys, lanes), F32)
    for a in range(PEER_TOPK):
        n_sel = jnp.where(iota_n == si1[a:a + 1, :], cnt[a:a + 1, :], n_sel)
    r2_ref[h, :, ls] = rank2.astype(r2_ref.dtype)
    e2_ref[h, :, ls] = jnp.exp(s2 - sv2[0:1, :]).astype(e2_ref.dtype)
    n_ref[h, :, ls] = n_sel
    e1_ref[h, :, ls] = jnp.exp(s1 - sv1[0:1, :]) * (GELU_HALF / z)


def _peer_head_top_values(h, s_scr, t_scr, sv_scr):
    n_keys, lanes, sub = PEER_N_KEYS, 128, 8

    head_row = lax.broadcasted_iota(jnp.int32, (sub, lanes), 0) == h

    def lane_tile(lt, _):
        ls = pl.ds(pl.multiple_of(lt * lanes, lanes), lanes)
        for p in range(2):
            rows = [s_scr[h, p, v * sub:(v + 1) * sub, ls] for v in range(n_keys // sub)]
            top = _top16_sorted(rows)
            for r in range(PEER_TOPK):
                t_scr[h, p, r, :, ls] = top[r]
                sv_scr[p, r, :, ls] = jnp.where(head_row, top[r], sv_scr[p, r, :, ls])
        return 0

    lax.fori_loop(0, s_scr.shape[3] // lanes, lane_tile, 0)


def _peer_head_distinct(h, s_scr, t_scr, cnt_scr, z_scr, r2_ref, e2_ref, n_ref, e1_ref):
    n_keys, lanes, sub = PEER_N_KEYS, 128, 8
    n_vr = n_keys // sub
    one = jnp.ones((sub, lanes), F32)
    zero = jnp.zeros((sub, lanes), F32)
    bad_tiles = []
    cnt_rows = [cnt_scr[a, pl.ds(h, 1), :] for a in range(PEER_TOPK)]
    inv_row = GELU_HALF / z_scr[pl.ds(h, 1), :]
    for lt in range(s_scr.shape[3] // lanes):
        ls = slice(lt * lanes, (lt + 1) * lanes)
        bad = zero
        for p in (1, 0):
            top = [t_scr[h, p, b, :, ls] for b in range(PEER_TOPK)]
            tied = _tree(jnp.maximum, [jnp.where(top[b] > top[b + 1], zero, one)
                                       for b in range(PEER_TOPK - 1)])
            n_top = zero
            if p == 1:
                for v in range(n_vr):
                    vs = slice(v * sub, (v + 1) * sub)
                    x = s_scr[h, 1, vs, ls]
                    rank = _count_greater(top, x)
                    r2_ref[h, vs, ls] = rank.astype(r2_ref.dtype)
                    e2_ref[h, vs, ls] = jnp.exp(x - top[0]).astype(e2_ref.dtype)
                    n_top = n_top + jnp.where(rank < float(PEER_TOPK), one, zero)
            else:
                cnt_b = [jnp.broadcast_to(r[:, ls], (sub, lanes)) for r in cnt_rows]
                inv_b = jnp.broadcast_to(inv_row[:, ls], (sub, lanes))
                for v in range(n_vr):
                    vs = slice(v * sub, (v + 1) * sub)
                    x = s_scr[h, 0, vs, ls]
                    n_sel = zero
                    for a in range(PEER_TOPK):
                        n_sel = jnp.where(x == top[a], cnt_b[a], n_sel)
                    n_ref[h, vs, ls] = n_sel
                    e1_ref[h, vs, ls] = jnp.exp(x - top[0]) * inv_b
                    n_top = n_top + jnp.where(x >= top[PEER_TOPK - 1], one, zero)
            miscount = jnp.where(_all_sublanes(jnp.add, n_top) != float(PEER_TOPK), one, zero)
            bad = jnp.maximum(bad, jnp.maximum(tied, miscount))
        bad_tiles.append(bad)
    return bad_tiles


def _peer_weights_kernel(xt_ref, wq_ref, keys_ref, r2_ref, e2_ref, n_ref, e1_ref,
                         q_scr, s_scr, t_scr, sv_scr, cnt_scr, z_scr):
    q_scr[...] = jnp.dot(wq_ref[...], xt_ref[...], preferred_element_type=F32).astype(BF16)
    sv_scr[...] = jnp.zeros_like(sv_scr)

    def top_values(h, _):
        _peer_scores(h, keys_ref, q_scr, s_scr)
        _peer_head_top_values(h, s_scr, t_scr, sv_scr)
        return 0

    lax.fori_loop(0, PEER_HEADS, top_values, 0)
    cnt, z = _merge_top16(sv_scr[0], sv_scr[1])
    cnt_scr[...] = cnt
    z_scr[...] = z[0]

    def weights(h, _):
        bad_tiles = _peer_head_distinct(h, s_scr, t_scr, cnt_scr, z_scr,
                                        r2_ref, e2_ref, n_ref, e1_ref)

        @pl.when(jnp.max(_tree(jnp.maximum, bad_tiles)) > 0.0)
        def _():
            for lt, bad in enumerate(bad_tiles):
                @pl.when(jnp.max(bad) > 0.0)
                def _():
                    _peer_head_exact(h, lt, s_scr, r2_ref, e2_ref, n_ref, e1_ref)

        return 0

    lax.fori_loop(0, PEER_HEADS, weights, 0)


def peer_weights(xt, wq_t, keys):
    d, t = xt.shape
    tm = PEER_PRE_TILE
    out16 = jax.ShapeDtypeStruct((PEER_HEADS, PEER_N_KEYS, t), BF16)
    out32 = jax.ShapeDtypeStruct((PEER_HEADS, PEER_N_KEYS, t), F32)
    ospec = pl.BlockSpec((PEER_HEADS, PEER_N_KEYS, tm), lambda i: (0, 0, i))
    return pl.pallas_call(
        _peer_weights_kernel,
        grid=(t // tm,),
        in_specs=[pl.BlockSpec((d, tm), lambda i: (0, i)),
                  pl.BlockSpec(wq_t.shape, lambda i: (0, 0)),
                  pl.BlockSpec(keys.shape, lambda i: (0, 0, 0))],
        out_specs=[ospec, ospec, ospec, ospec],
        out_shape=[out16, out16, out32, out32],
        scratch_shapes=[pltpu.VMEM((wq_t.shape[0], tm), BF16),
                        pltpu.VMEM((PEER_HEADS, 2, PEER_N_KEYS, tm), F32),
                        pltpu.VMEM((PEER_HEADS, 2, PEER_TOPK, 8, tm), F32),
                        pltpu.VMEM((2, PEER_TOPK, PEER_HEADS, tm), F32),
                        pltpu.VMEM((PEER_TOPK, PEER_HEADS, tm), F32),
                        pltpu.VMEM((PEER_HEADS, tm), F32)],
        compiler_params=_cparams(("parallel",)),
        name="peer_weights",
    )(xt, wq_t, keys)


def _peer_hidden_rows(i, il, a_ref, hid_ref, r2_ref, e2_ref, n_ref, e1_ref):
    nk, lanes, pack = PEER_N_KEYS, 128, 16
    row0 = il * nk
    n_rows = [n_ref[h, pl.ds(i, 1), :] for h in range(PEER_HEADS)]
    e1_rows = [e1_ref[h, pl.ds(i, 1), :] for h in range(PEER_HEADS)]
    zero = jnp.zeros((pack, lanes), BF16)
    for lt in range(a_ref.shape[1] // lanes):
        ls = slice(lt * lanes, (lt + 1) * lanes)
        n_b = [jnp.broadcast_to(r[:, ls], (pack, lanes)).astype(BF16) for r in n_rows]
        e1_b = [jnp.broadcast_to(r[:, ls], (pack, lanes)).astype(BF16) for r in e1_rows]
        for jb in range(nk // pack):
            js = slice(jb * pack, (jb + 1) * pack)
            g = None
            for h in range(PEER_HEADS):
                term = jnp.where(r2_ref[h, js, ls] < n_b[h], e2_ref[h, js, ls] * e1_b[h], zero)
                g = term if g is None else g + term
            rs = slice(row0 + jb * pack, row0 + (jb + 1) * pack)
            a = a_ref[rs, ls]
            act = a * (1.0 + lax.erf(a * math.sqrt(0.5)))
            hid_ref[rs, ls] = act.astype(BF16) * g


def _peer_dense_kernel(xt_ref, r2_ref, e2_ref, n_ref, e1_ref, wup_ref, wdn_ref, h_ref, g_ref, o_ref,
                       acc_ref, a0_ref, a1_ref, hid0_ref, hid1_ref, r2_scr, e2_scr, *, out_norm):
    e = pl.program_id(1)
    grp = a0_ref.shape[0]
    rows_per_grp = grp // PEER_N_KEYS
    n_grp = wup_ref.shape[0] // grp

    @pl.when(e == 0)
    def _():
        acc_ref[...] = jnp.zeros_like(acc_ref)
        r2_scr[...] = r2_ref[...]
        e2_scr[...] = e2_ref[...]

    a_refs = (a0_ref, a1_ref)
    hid_refs = (hid0_ref, hid1_ref)
    for p in range(n_grp + 2):
        if p < n_grp:
            a_refs[p % 2][...] = jnp.dot(wup_ref[p * grp:(p + 1) * grp, :], xt_ref[...],
                                         preferred_element_type=F32)
        if 1 <= p <= n_grp:
            g = p - 1
            for il in range(rows_per_grp):
                _peer_hidden_rows((e * n_grp + g) * rows_per_grp + il, il, a_refs[g % 2],
                                  hid_refs[g % 2], r2_scr, e2_scr, n_ref, e1_ref)
        if p >= 2:
            g = p - 2
            acc_ref[...] += lax.dot_general(hid_refs[g % 2][...], wdn_ref[g * grp:(g + 1) * grp, :],
                                            (((0,), (0,)), ((), ())), preferred_element_type=F32)

    @pl.when(e == pl.num_programs(1) - 1)
    def _():
        h_new = h_ref[...] + acc_ref[...]
        o_ref[...] = _rms(h_new, g_ref[...]) if out_norm else h_new


def peer_dense(xt, r2, e2, n_sel, e1n, w_up, w_down, layer, h, out_gain, out_norm):
    d, t = xt.shape
    n_exp = w_up.shape[1]
    tm, te = PEER_TOK_TILE, PEER_EXP_TILE
    wspec = pl.BlockSpec((PEER_HEADS, PEER_N_KEYS, tm), lambda i, e: (0, 0, i))
    return pl.pallas_call(
        functools.partial(_peer_dense_kernel, out_norm=out_norm),
        grid=(t // tm, n_exp // te),
        in_specs=[pl.BlockSpec((d, tm), lambda i, e: (0, i)),
                  wspec, wspec, wspec, wspec,
                  pl.BlockSpec((None, te, d), lambda i, e: (layer, e, 0)),
                  pl.BlockSpec((None, te, d), lambda i, e: (layer, e, 0)),
                  pl.BlockSpec((tm, d), lambda i, e: (i, 0)),
                  pl.BlockSpec((1, d), lambda i, e: (0, 0))],
        out_specs=pl.BlockSpec((tm, d), lambda i, e: (i, 0)),
        out_shape=jax.ShapeDtypeStruct((t, d), F32),
        scratch_shapes=[pltpu.VMEM((tm, d), F32),
                        pltpu.VMEM((PEER_STAGE_ROWS, tm), F32),
                        pltpu.VMEM((PEER_STAGE_ROWS, tm), F32),
                        pltpu.VMEM((PEER_STAGE_ROWS, tm), BF16),
                        pltpu.VMEM((PEER_STAGE_ROWS, tm), BF16),
                        pltpu.VMEM((PEER_HEADS, PEER_N_KEYS, tm), BF16),
                        pltpu.VMEM((PEER_HEADS, PEER_N_KEYS, tm), BF16)],
        compiler_params=_cparams(("parallel", "arbitrary")),
        name="peer_dense",
    )(xt, r2, e2, n_sel, e1n, w_up, w_down, h, out_gain.reshape(1, d))


def kernel(x, norm_mix, norm_ffn, norm_final, attn_w_qkv, attn_w_o, attn_rel_bias,
           conv_w_in, conv_w_conv, conv_w_out, peer_w_query, peer_sub_keys,
           peer_w_up, peer_w_down):
    batch, seq, d = x.shape
    depth = norm_mix.shape[0]
    rows = seq // GRID_W
    h = x.reshape(batch * seq, d)
    w_up_all = peer_w_up.astype(BF16)
    w_down_all = peer_w_down.astype(BF16)
    for i in range(depth):
        j = i // 2
        if i % 2 == 0:
            qkv = norm_matmul(h, norm_mix[i], attn_w_qkv[j].astype(BF16), BF16)
            bias_tabs = _attention_bias_tables(attn_rel_bias[j], rows)
            mixed = neighbourhood_attention_core(qkv, bias_tabs, batch, seq)
            w_out = attn_w_o[j]
        else:
            y = norm_matmul(h, norm_mix[i], conv_w_in[j].astype(BF16), BF16)
            mixed = short_conv_gate(y, conv_w_conv[j], seq)
            w_out = conv_w_out[j]
        h, xt = matmul_residual_norm(mixed, w_out.astype(BF16), h, norm_ffn[i])
        keys = peer_sub_keys[i].reshape(PEER_HEADS * 2, PEER_N_KEYS, PEER_HALF_DIM).astype(BF16)
        r2, e2, n_sel, e1n = peer_weights(xt, peer_w_query[i].T.astype(BF16), keys)
        h = peer_dense(xt, r2, e2, n_sel, e1n, w_up_all, w_down_all, i, h, norm_final,
                       out_norm=(i == depth - 1))
    return h.reshape(batch, seq, d)
```

```python
import functools
import math

import numpy as np
import jax
import jax.numpy as jnp
from jax import lax
from jax.experimental import pallas as pl
from jax.experimental.pallas import tpu as pltpu

F32 = jnp.float32
BF16 = jnp.bfloat16

RMS_EPS = 1e-6
NEG_INF = -1e30

GRID_W = 64
N_HEADS = 16
HEAD_DIM = 64
MAX_WIN_ROWS = 8
WIN_COLS = 16
ATT_Q_ROWS = 4
ATT_BAND_ROWS = ATT_Q_ROWS + MAX_WIN_ROWS - 1
ATT_LANES = 128
ATT_BLOCKS_PER_STEP = 4

PEER_HEADS = 8
PEER_N_KEYS = 128
PEER_TOPK = 16
PEER_HALF_DIM = 128
GELU_HALF = 0.5

TOK_TILE = 512
PEER_PRE_TILE = 512
PEER_TOK_TILE = 256
PEER_EXP_TILE = 4096
PEER_STAGE_ROWS = 256
VMEM_LIMIT = 56 * 1024 * 1024


def _cparams(sem):
    return pltpu.CompilerParams(dimension_semantics=sem, vmem_limit_bytes=VMEM_LIMIT)


def _rms(x, g):
    return x * lax.rsqrt(jnp.mean(x * x, axis=-1, keepdims=True) + RMS_EPS) * g


def _norm_matmul_kernel(x_ref, g_ref, w_ref, o_ref):
    y = _rms(x_ref[...], g_ref[...])
    o_ref[...] = jnp.dot(y.astype(BF16), w_ref[...],
                         preferred_element_type=F32).astype(o_ref.dtype)


def norm_matmul(x, gain, w, out_dtype):
    t, d = x.shape
    n = w.shape[1]
    return pl.pallas_call(
        _norm_matmul_kernel,
        grid=(t // TOK_TILE,),
        in_specs=[pl.BlockSpec((TOK_TILE, d), lambda i: (i, 0)),
                  pl.BlockSpec((1, d), lambda i: (0, 0)),
                  pl.BlockSpec((d, n), lambda i: (0, 0))],
        out_specs=pl.BlockSpec((TOK_TILE, n), lambda i: (i, 0)),
        out_shape=jax.ShapeDtypeStruct((t, n), out_dtype),
        compiler_params=_cparams(("parallel",)),
        name="norm_matmul",
    )(x, gain.reshape(1, d), w)


def _matmul_residual_norm_kernel(a_ref, w_ref, h_ref, g_ref, o_ref, xt_ref):
    h_new = h_ref[...] + jnp.dot(a_ref[...], w_ref[...], preferred_element_type=F32)
    o_ref[...] = h_new
    xt_ref[...] = _rms(h_new, g_ref[...]).T.astype(xt_ref.dtype)


def matmul_residual_norm(a, w, h, gain):
    t, k = a.shape
    d = w.shape[1]
    return pl.pallas_call(
        _matmul_residual_norm_kernel,
        grid=(t // TOK_TILE,),
        in_specs=[pl.BlockSpec((TOK_TILE, k), lambda i: (i, 0)),
                  pl.BlockSpec((k, d), lambda i: (0, 0)),
                  pl.BlockSpec((TOK_TILE, d), lambda i: (i, 0)),
                  pl.BlockSpec((1, d), lambda i: (0, 0))],
        out_specs=[pl.BlockSpec((TOK_TILE, d), lambda i: (i, 0)),
                   pl.BlockSpec((d, TOK_TILE), lambda i: (0, i))],
        out_shape=[jax.ShapeDtypeStruct((t, d), F32), jax.ShapeDtypeStruct((d, t), BF16)],
        compiler_params=_cparams(("parallel",)),
        name="matmul_residual_norm",
    )(a, w, h, gain.reshape(1, d))


def _attention_bias_tables(rel_bias, rows):
    n_h = rel_bias.shape[0]
    n_dr = 2 * MAX_WIN_ROWS - 1
    w = GRID_W
    edge = w - WIN_COLS
    line = jnp.pad(rel_bias, ((0, 0), (0, 0), (edge, edge + 1)), mode="edge")
    flat = jnp.tile(line, (1, 1, w))[:, :, w - 1:w - 1 + w * (2 * w - 1)]
    toep = flat.reshape(n_h, n_dr, w, 2 * w - 1)[:, :, :, :w]
    qc = np.arange(w)[:, None]
    kc = np.arange(w)[None, :]
    ws = np.clip(qc - WIN_COLS // 2, 0, w - WIN_COLS)
    toep = jnp.where(((kc >= ws) & (kc < ws + WIN_COLS))[None, None], toep, NEG_INF)
    pad = ATT_Q_ROWS - 1
    toep = jnp.pad(toep, ((0, 0), (pad, pad), (0, 0), (0, 0)), constant_values=NEG_INF)
    tabs = []
    for r0 in (0, ATT_Q_ROWS, rows - ATT_Q_ROWS):
        band = int(np.clip(r0 - MAX_WIN_ROWS // 2, 0, rows - ATT_BAND_ROWS))
        blocks = []
        for ql in range(ATT_Q_ROWS):
            lo = band - (r0 + ql) + MAX_WIN_ROWS - 1 + pad
            blk = toep[:, lo:lo + ATT_BAND_ROWS]
            blocks.append(blk.transpose(0, 2, 1, 3).reshape(n_h, w, ATT_BAND_ROWS * w))
        tab = jnp.stack(blocks, axis=1).reshape(n_h, ATT_Q_ROWS * w, ATT_BAND_ROWS * w)
        qr = r0 + np.arange(ATT_Q_ROWS)[:, None]
        kr = band + np.arange(ATT_BAND_ROWS)[None, :]
        rs = np.clip(qr - MAX_WIN_ROWS // 2, 0, rows - MAX_WIN_ROWS)
        row_ok = (kr >= rs) & (kr < rs + MAX_WIN_ROWS)
        row_ok = np.repeat(np.repeat(row_ok, w, axis=0), w, axis=1)
        tabs.append(jnp.where(row_ok[None], tab, NEG_INF))
    return jnp.stack(tabs)


def _attention_kernel(q_ref, k_ref, v_ref, *rest, rows):
    bias_refs, o_ref = rest[:ATT_BLOCKS_PER_STEP], rest[ATT_BLOCKS_PER_STEP]
    q_tok = ATT_Q_ROWS * GRID_W
    n_keys = ATT_BAND_ROWS * GRID_W
    scale = 1.0 / math.sqrt(HEAD_DIM)
    lane = lax.broadcasted_iota(jnp.int32, (q_tok, ATT_LANES), 1)
    for sub in range(ATT_BLOCKS_PER_STEP):
        rb = pl.program_id(2) * ATT_BLOCKS_PER_STEP + sub
        band = jnp.clip(rb * ATT_Q_ROWS - MAX_WIN_ROWS // 2, 0, rows - ATT_BAND_ROWS)
        start = pl.multiple_of(band * GRID_W, GRID_W)
        kb = k_ref[pl.ds(start, n_keys), :]
        vb = v_ref[pl.ds(start, n_keys), :]
        q = q_ref[sub * q_tok:(sub + 1) * q_tok, :]
        outs = []
        for hh in range(ATT_LANES // HEAD_DIM):
            in_head = (lane < HEAD_DIM) if hh == 0 else (lane >= HEAD_DIM)
            qm = jnp.where(in_head, q, jnp.zeros_like(q)) * jnp.asarray(scale, q.dtype)
            s = lax.dot_general(qm, kb, (((1,), (1,)), ((), ())), preferred_element_type=F32)
            s = s + bias_refs[sub][0, hh]
            m = jnp.max(s, axis=-1, keepdims=True)
            p = jnp.exp(s - m)
            l = jnp.sum(p, axis=-1, keepdims=True)
            o = jnp.dot(p.astype(BF16), vb, preferred_element_type=F32)
            outs.append(o / l)
        o_ref[sub * q_tok:(sub + 1) * q_tok, :] = jnp.where(lane < HEAD_DIM, outs[0],
                                                             outs[1]).astype(o_ref.dtype)


def neighbourhood_attention_core(qkv, bias_tabs, batch, seq):
    t = qkv.shape[0]
    d = qkv.shape[1] // 3
    rows = seq // GRID_W
    n_rb = rows // ATT_Q_ROWS
    n_steps = n_rb // ATT_BLOCKS_PER_STEP
    n_hp = d // ATT_LANES
    q_tok = ATT_Q_ROWS * GRID_W * ATT_BLOCKS_PER_STEP

    def bias_spec(sub):
        def index_map(b, hp, st):
            rb = st * ATT_BLOCKS_PER_STEP + sub
            return (jnp.where(rb == 0, 0, jnp.where(rb == n_rb - 1, 2, 1)), hp, 0, 0)
        return pl.BlockSpec((1, 2, ATT_Q_ROWS * GRID_W, ATT_BAND_ROWS * GRID_W), index_map)

    return pl.pallas_call(
        functools.partial(_attention_kernel, rows=rows),
        grid=(batch, n_hp, n_steps),
        in_specs=[pl.BlockSpec((q_tok, ATT_LANES), lambda b, hp, st: (b * n_steps + st, hp)),
                  pl.BlockSpec((seq, ATT_LANES), lambda b, hp, st: (b, n_hp + hp)),
                  pl.BlockSpec((seq, ATT_LANES), lambda b, hp, st: (b, 2 * n_hp + hp))]
                 + [bias_spec(sub) for sub in range(ATT_BLOCKS_PER_STEP)],
        out_specs=pl.BlockSpec((q_tok, ATT_LANES), lambda b, hp, st: (b * n_steps + st, hp)),
        out_shape=jax.ShapeDtypeStruct((t, d), BF16),
        compiler_params=_cparams(("parallel", "parallel", "arbitrary")),
        name="natten_core",
    )(qkv, qkv, qkv, *([bias_tabs] * ATT_BLOCKS_PER_STEP))


def _short_conv_kernel(b_ref, c_ref, v_ref, cp_ref, vp_ref, cn_ref, vn_ref, w_ref, o_ref, *, seq):
    i = pl.program_id(0)
    tm = b_ref.shape[0]
    halo = cp_ref.shape[0]
    u = c_ref[...].astype(F32) * v_ref[...].astype(F32)
    u_before = cp_ref[halo - 1:halo, :].astype(F32) * vp_ref[halo - 1:halo, :].astype(F32)
    u_after = cn_ref[0:1, :].astype(F32) * vn_ref[0:1, :].astype(F32)
    row = lax.broadcasted_iota(jnp.int32, u.shape, 0)
    pos = (i * tm + row) % seq
    u_prev = jnp.where(row == 0, u_before, pltpu.roll(u, 1, 0))
    u_prev = jnp.where(pos == 0, 0.0, u_prev)
    u_next = jnp.where(row == tm - 1, u_after, pltpu.roll(u, tm - 1, 0))
    u_next = jnp.where(pos == seq - 1, 0.0, u_next)
    conv = u_prev * w_ref[0:1, :] + u * w_ref[1:2, :] + u_next * w_ref[2:3, :]
    o_ref[...] = (b_ref[...].astype(F32) * conv).astype(o_ref.dtype)


def short_conv_gate(y, w_conv, seq):
    t = y.shape[0]
    d = y.shape[1] // 3
    cw = 512
    nj = d // cw
    tm = TOK_TILE
    halo = 16
    n_halo = t // halo

    def prev_map(off):
        return lambda i, j: (jnp.maximum(i * (tm // halo) - 1, 0), off * nj + j)

    def next_map(off):
        return lambda i, j: (jnp.minimum((i + 1) * (tm // halo), n_halo - 1), off * nj + j)

    return pl.pallas_call(
        functools.partial(_short_conv_kernel, seq=seq),
        grid=(t // tm, nj),
        in_specs=[pl.BlockSpec((tm, cw), lambda i, j: (i, j)),
                  pl.BlockSpec((tm, cw), lambda i, j: (i, nj + j)),
                  pl.BlockSpec((tm, cw), lambda i, j: (i, 2 * nj + j)),
                  pl.BlockSpec((halo, cw), prev_map(1)),
                  pl.BlockSpec((halo, cw), prev_map(2)),
                  pl.BlockSpec((halo, cw), next_map(1)),
                  pl.BlockSpec((halo, cw), next_map(2)),
                  pl.BlockSpec((3, cw), lambda i, j: (0, j))],
        out_specs=pl.BlockSpec((tm, cw), lambda i, j: (i, j)),
        out_shape=jax.ShapeDtypeStruct((t, d), BF16),
        compiler_params=_cparams(("parallel", "parallel")),
        name="short_conv_gate",
    )(y, y, y, y, y, y, y, w_conv)


def _top16(sc, want_rank):
    n, l = sc.shape
    iota_n = lax.broadcasted_iota(jnp.int32, (n, l), 0).astype(F32)
    iota_k = lax.broadcasted_iota(jnp.int32, (PEER_TOPK, l), 0).astype(F32)

    def body(r, carry):
        sc, rank, sv, si = carry
        rf = jnp.asarray(r, jnp.int32).astype(F32)
        m = jnp.max(sc, axis=0, keepdims=True)
        idx = jnp.min(jnp.where(sc == m, iota_n, float(n)), axis=0, keepdims=True)
        sel = iota_n == idx
        if want_rank:
            rank = jnp.where(sel, rf, rank)
        sc = jnp.where(sel, -jnp.inf, sc)
        sv = jnp.where(iota_k == rf, m, sv)
        si = jnp.where(iota_k == rf, idx, si)
        return sc, rank, sv, si

    init = (sc, jnp.full((n, l), float(PEER_TOPK), F32),
            jnp.zeros((PEER_TOPK, l), F32), jnp.zeros((PEER_TOPK, l), F32))
    _, rank, sv, si = lax.fori_loop(0, PEER_TOPK, body, init)
    return sv, si, rank


def _merge_top16(sv1, sv2):
    k = sv1.shape[0]
    iota_k = lax.broadcasted_iota(jnp.int32, sv1.shape, 0).astype(F32)
    top = sv1[0:1] + sv2[0:1]

    def body(_, carry):
        front, cnt, z = carry
        m = jnp.max(front, axis=0, keepdims=True)
        a_star = jnp.min(jnp.where(front == m, iota_k, float(k)), axis=0, keepdims=True)
        sel = iota_k == a_star
        z = z + jnp.exp(m - top)
        cnt = jnp.where(sel, cnt + 1.0, cnt)
        c_sel = jnp.sum(jnp.where(sel, cnt, 0.0), axis=0, keepdims=True)
        nxt = jnp.sum(jnp.where(iota_k == c_sel, sv2, 0.0), axis=0, keepdims=True)
        nxt = jnp.where(c_sel >= float(k), -jnp.inf, nxt)
        front = jnp.where(sel, sv1 + nxt, front)
        return front, cnt, z

    init = (sv1 + sv2[0:1], jnp.zeros(sv1.shape, F32), jnp.zeros(top.shape, F32))
    _, cnt, z = lax.fori_loop(0, PEER_TOPK, body, init)
    return cnt, z


def _tree(op, xs):
    while len(xs) > 1:
        xs = [op(xs[k], xs[k + 1]) for k in range(0, len(xs) - 1, 2)] + (xs[-1:] if len(xs) % 2 else [])
    return xs[0]


def _all_sublanes(op, x):
    for shift in (4, 2, 1):
        x = op(x, pltpu.roll(x, shift, 0))
    return x


def _count_greater(top, x):
    sel = jnp.where
    c8 = top[7] > x
    c4 = sel(c8, top[11], top[3]) > x
    c2 = sel(c8, sel(c4, top[13], top[9]), sel(c4, top[5], top[1])) > x
    c1 = sel(c8, sel(c4, sel(c2, top[14], top[12]), sel(c2, top[10], top[8])),
             sel(c4, sel(c2, top[6], top[4]), sel(c2, top[2], top[0]))) > x
    c16 = top[15] > x
    zero = jnp.zeros_like(x)
    return ((sel(c8, 8.0, zero) + sel(c4, 4.0, zero)) + (sel(c2, 2.0, zero) + sel(c1, 1.0, zero))
            + sel(c16, 1.0, zero))


def _sorting_network(n):
    def merge(lo, hi, r):
        step = r * 2
        if step < hi - lo:
            yield from merge(lo, hi, step)
            yield from merge(lo + r, hi, step)
            yield from [(k, k + r) for k in range(lo + r, hi - r, step)]
        else:
            yield (lo, lo + r)

    def sort(lo, hi):
        if hi - lo >= 1:
            mid = lo + (hi - lo) // 2
            yield from sort(lo, mid)
            yield from sort(mid + 1, hi)
            yield from merge(lo, hi, 1)

    return tuple(sort(0, n - 1))


_SORT16 = _sorting_network(PEER_TOPK)


def _compare_exchange(xs, i, j):
    xs[i], xs[j] = jnp.maximum(xs[i], xs[j]), jnp.minimum(xs[i], xs[j])


def _top16_sorted(rows):
    xs = list(rows)
    for i, j in _SORT16:
        _compare_exchange(xs, i, j)
    k = len(xs)
    for shift in (4, 2, 1):
        ys = [pltpu.roll(x, shift, 0) for x in xs]
        xs = [jnp.maximum(xs[v], ys[k - 1 - v]) for v in range(k)]
        stride = k // 2
        while stride:
            for v in range(k):
                if not v & stride:
                    _compare_exchange(xs, v, v + stride)
            stride //= 2
    return xs


def _peer_scores(h, keys_ref, q_scr, s_scr):
    for p in range(2):
        r0 = pl.multiple_of((2 * h + p) * PEER_HALF_DIM, PEER_HALF_DIM)
        s_scr[h, p] = jnp.dot(keys_ref[2 * h + p], q_scr[pl.ds(r0, PEER_HALF_DIM), :],
                              preferred_element_type=F32)


def _peer_head_exact(h, lt, s_scr, r2_ref, e2_ref, n_ref, e1_ref):
    n_keys, lanes = PEER_N_KEYS, 128
    iota_n = lax.broadcasted_iota(jnp.int32, (n_keys, lanes), 0).astype(F32)
    ls = slice(lt * lanes, (lt + 1) * lanes)
    s1 = s_scr[h, 0, :, ls]
    s2 = s_scr[h, 1, :, ls]
    sv1, si1, _ = _top16(s1, want_rank=False)
    sv2, _, rank2 = _top16(s2, want_rank=True)
    cnt, z = _merge_top16(sv1, sv2)
    n_sel = jnp.zeros((n_keys, lanes), F32)
    for a in range(PEER_TOPK):
        n_sel = jnp.where(iota_n == si1[a:a + 1, :], cnt[a:a + 1, :], n_sel)
    r2_ref[h, :, ls] = rank2.astype(r2_ref.dtype)
    e2_ref[h, :, ls] = jnp.exp(s2 - sv2[0:1, :]).astype(e2_ref.dtype)
    n_ref[h, :, ls] = n_sel
    e1_ref[h, :, ls] = jnp.exp(s1 - sv1[0:1, :]) * (GELU_HALF / z)


def _peer_head_top_values(h, s_scr, t_scr, sv_scr):
    n_keys, lanes, sub = PEER_N_KEYS, 128, 8

    head_row = lax.broadcasted_iota(jnp.int32, (sub, lanes), 0) == h

    def lane_tile(lt, _):
        ls = pl.ds(pl.multiple_of(lt * lanes, lanes), lanes)
        for p in range(2):
            rows = [s_scr[h, p, v * sub:(v + 1) * sub, ls] for v in range(n_keys // sub)]
            top = _top16_sorted(rows)
            for r in range(PEER_TOPK):
                t_scr[h, p, r, :, ls] = top[r]
                sv_scr[p, r, :, ls] = jnp.where(head_row, top[r], sv_scr[p, r, :, ls])
        return 0

    lax.fori_loop(0, s_scr.shape[3] // lanes, lane_tile, 0)


def _peer_head_distinct(h, s_scr, t_scr, cnt_scr, z_scr, r2_ref, e2_ref, n_ref, e1_ref):
    n_keys, lanes, sub = PEER_N_KEYS, 128, 8
    n_vr = n_keys // sub
    one = jnp.ones((sub, lanes), F32)
    zero = jnp.zeros((sub, lanes), F32)
    bad_tiles = []
    cnt_rows = [cnt_scr[a, pl.ds(h, 1), :] for a in range(PEER_TOPK)]
    inv_row = GELU_HALF / z_scr[pl.ds(h, 1), :]
    for lt in range(s_scr.shape[3] // lanes):
        ls = slice(lt * lanes, (lt + 1) * lanes)
        bad = zero
        for p in (1, 0):
            top = [t_scr[h, p, b, :, ls] for b in range(PEER_TOPK)]
            tied = _tree(jnp.maximum, [jnp.where(top[b] > top[b + 1], zero, one)
                                       for b in range(PEER_TOPK - 1)])
            n_top = zero
            if p == 1:
                for v in range(n_vr):
                    vs = slice(v * sub, (v + 1) * sub)
                    x = s_scr[h, 1, vs, ls]
                    rank = _count_greater(top, x)
                    r2_ref[h, vs, ls] = rank.astype(r2_ref.dtype)
                    e2_ref[h, vs, ls] = jnp.exp(x - top[0]).astype(e2_ref.dtype)
                    n_top = n_top + jnp.where(rank < float(PEER_TOPK), one, zero)
            else:
                cnt_b = [jnp.broadcast_to(r[:, ls], (sub, lanes)) for r in cnt_rows]
                inv_b = jnp.broadcast_to(inv_row[:, ls], (sub, lanes))
                for v in range(n_vr):
                    vs = slice(v * sub, (v + 1) * sub)
                    x = s_scr[h, 0, vs, ls]
                    n_sel = zero
                    for a in range(PEER_TOPK):
                        n_sel = jnp.where(x == top[a], cnt_b[a], n_sel)
                    n_ref[h, vs, ls] = n_sel
                    e1_ref[h, vs, ls] = jnp.exp(x - top[0]) * inv_b
                    n_top = n_top + jnp.where(x >= top[PEER_TOPK - 1], one, zero)
            miscount = jnp.where(_all_sublanes(jnp.add, n_top) != float(PEER_TOPK), one, zero)
            bad = jnp.maximum(bad, jnp.maximum(tied, miscount))
        bad_tiles.append(bad)
    return bad_tiles


def _peer_weights_kernel(xt_ref, wq_ref, keys_ref, r2_ref, e2_ref, n_ref, e1_ref,
                         q_scr, s_scr, t_scr, sv_scr, cnt_scr, z_scr):
    q_scr[...] = jnp.dot(wq_ref[...], xt_ref[...], preferred_element_type=F32).astype(BF16)
    sv_scr[...] = jnp.zeros_like(sv_scr)

    def top_values(h, _):
        _peer_scores(h, keys_ref, q_scr, s_scr)
        _peer_head_top_values(h, s_scr, t_scr, sv_scr)
        return 0

    lax.fori_loop(0, PEER_HEADS, top_values, 0)
    cnt, z = _merge_top16(sv_scr[0], sv_scr[1])
    cnt_scr[...] = cnt
    z_scr[...] = z[0]

    def weights(h, _):
        bad_tiles = _peer_head_distinct(h, s_scr, t_scr, cnt_scr, z_scr,
                                        r2_ref, e2_ref, n_ref, e1_ref)

        @pl.when(jnp.max(_tree(jnp.maximum, bad_tiles)) > 0.0)
        def _():
            for lt, bad in enumerate(bad_tiles):
                @pl.when(jnp.max(bad) > 0.0)
                def _():
                    _peer_head_exact(h, lt, s_scr, r2_ref, e2_ref, n_ref, e1_ref)

        return 0

    lax.fori_loop(0, PEER_HEADS, weights, 0)


def peer_weights(xt, wq_t, keys):
    d, t = xt.shape
    tm = PEER_PRE_TILE
    out16 = jax.ShapeDtypeStruct((PEER_HEADS, PEER_N_KEYS, t), BF16)
    out32 = jax.ShapeDtypeStruct((PEER_HEADS, PEER_N_KEYS, t), F32)
    ospec = pl.BlockSpec((PEER_HEADS, PEER_N_KEYS, tm), lambda i: (0, 0, i))
    return pl.pallas_call(
        _peer_weights_kernel,
        grid=(t // tm,),
        in_specs=[pl.BlockSpec((d, tm), lambda i: (0, i)),
                  pl.BlockSpec(wq_t.shape, lambda i: (0, 0)),
                  pl.BlockSpec(keys.shape, lambda i: (0, 0, 0))],
        out_specs=[ospec, ospec, ospec, ospec],
        out_shape=[out16, out16, out32, out32],
        scratch_shapes=[pltpu.VMEM((wq_t.shape[0], tm), BF16),
                        pltpu.VMEM((PEER_HEADS, 2, PEER_N_KEYS, tm), F32),
                        pltpu.VMEM((PEER_HEADS, 2, PEER_TOPK, 8, tm), F32),
                        pltpu.VMEM((2, PEER_TOPK, PEER_HEADS, tm), F32),
                        pltpu.VMEM((PEER_TOPK, PEER_HEADS, tm), F32),
                        pltpu.VMEM((PEER_HEADS, tm), F32)],
        compiler_params=_cparams(("parallel",)),
        name="peer_weights",
    )(xt, wq_t, keys)


def _peer_hidden_rows(i, il, a_ref, hid_ref, r2_ref, e2_ref, n_ref, e1_ref):
    nk, lanes, pack = PEER_N_KEYS, 128, 16
    row0 = il * nk
    n_rows = [n_ref[h, pl.ds(i, 1), :] for h in range(PEER_HEADS)]
    e1_rows = [e1_ref[h, pl.ds(i, 1), :] for h in range(PEER_HEADS)]
    zero = jnp.zeros((pack, lanes), BF16)
    for lt in range(a_ref.shape[1] // lanes):
        ls = slice(lt * lanes, (lt + 1) * lanes)
        n_b = [jnp.broadcast_to(r[:, ls], (pack, lanes)).astype(BF16) for r in n_rows]
        e1_b = [jnp.broadcast_to(r[:, ls], (pack, lanes)).astype(BF16) for r in e1_rows]
        for jb in range(nk // pack):
            js = slice(jb * pack, (jb + 1) * pack)
            g = None
            for h in range(PEER_HEADS):
                term = jnp.where(r2_ref[h, js, ls] < n_b[h], e2_ref[h, js, ls] * e1_b[h], zero)
                g = term if g is None else g + term
            rs = slice(row0 + jb * pack, row0 + (jb + 1) * pack)
            a = a_ref[rs, ls]
            act = a * (1.0 + lax.erf(a * math.sqrt(0.5)))
            hid_ref[rs, ls] = act.astype(BF16) * g


def _peer_dense_kernel(xt_ref, r2_ref, e2_ref, n_ref, e1_ref, wup_ref, wdn_ref, h_ref, g_ref, o_ref,
                       acc_ref, a0_ref, a1_ref, hid0_ref, hid1_ref, r2_scr, e2_scr, *, out_norm):
    e = pl.program_id(1)
    grp = a0_ref.shape[0]
    rows_per_grp = grp // PEER_N_KEYS
    n_grp = wup_ref.shape[0] // grp

    @pl.when(e == 0)
    def _():
        acc_ref[...] = jnp.zeros_like(acc_ref)
        r2_scr[...] = r2_ref[...]
        e2_scr[...] = e2_ref[...]

    a_refs = (a0_ref, a1_ref)
    hid_refs = (hid0_ref, hid1_ref)
    for p in range(n_grp + 2):
        if p < n_grp:
            a_refs[p % 2][...] = jnp.dot(wup_ref[p * grp:(p + 1) * grp, :], xt_ref[...],
                                         preferred_element_type=F32)
        if 1 <= p <= n_grp:
            g = p - 1
            for il in range(rows_per_grp):
                _peer_hidden_rows((e * n_grp + g) * rows_per_grp + il, il, a_refs[g % 2],
                                  hid_refs[g % 2], r2_scr, e2_scr, n_ref, e1_ref)
        if p >= 2:
            g = p - 2
            acc_ref[...] += lax.dot_general(hid_refs[g % 2][...], wdn_ref[g * grp:(g + 1) * grp, :],
                                            (((0,), (0,)), ((), ())), preferred_element_type=F32)

    @pl.when(e == pl.num_programs(1) - 1)
    def _():
        h_new = h_ref[...] + acc_ref[...]
        o_ref[...] = _rms(h_new, g_ref[...]) if out_norm else h_new


def peer_dense(xt, r2, e2, n_sel, e1n, w_up, w_down, layer, h, out_gain, out_norm):
    d, t = xt.shape
    n_exp = w_up.shape[1]
    tm, te = PEER_TOK_TILE, PEER_EXP_TILE
    wspec = pl.BlockSpec((PEER_HEADS, PEER_N_KEYS, tm), lambda i, e: (0, 0, i))
    return pl.pallas_call(
        functools.partial(_peer_dense_kernel, out_norm=out_norm),
        grid=(t // tm, n_exp // te),
        in_specs=[pl.BlockSpec((d, tm), lambda i, e: (0, i)),
                  wspec, wspec, wspec, wspec,
                  pl.BlockSpec((None, te, d), lambda i, e: (layer, e, 0)),
                  pl.BlockSpec((None, te, d), lambda i, e: (layer, e, 0)),
                  pl.BlockSpec((tm, d), lambda i, e: (i, 0)),
                  pl.BlockSpec((1, d), lambda i, e: (0, 0))],
        out_specs=pl.BlockSpec((tm, d), lambda i, e: (i, 0)),
        out_shape=jax.ShapeDtypeStruct((t, d), F32),
        scratch_shapes=[pltpu.VMEM((tm, d), F32),
                        pltpu.VMEM((PEER_STAGE_ROWS, tm), F32),
                        pltpu.VMEM((PEER_STAGE_ROWS, tm), F32),
                        pltpu.VMEM((PEER_STAGE_ROWS, tm), BF16),
                        pltpu.VMEM((PEER_STAGE_ROWS, tm), BF16),
                        pltpu.VMEM((PEER_HEADS, PEER_N_KEYS, tm), BF16),
                        pltpu.VMEM((PEER_HEADS, PEER_N_KEYS, tm), BF16)],
        compiler_params=_cparams(("parallel", "arbitrary")),
        name="peer_dense",
    )(xt, r2, e2, n_sel, e1n, w_up, w_down, h, out_gain.reshape(1, d))


def kernel(x, norm_mix, norm_ffn, norm_final, attn_w_qkv, attn_w_o, attn_rel_bias,
           conv_w_in, conv_w_conv, conv_w_out, peer_w_query, peer_sub_keys,
           peer_w_up, peer_w_down):
    batch, seq, d = x.shape
    depth = norm_mix.shape[0]
    rows = seq // GRID_W
    h = x.reshape(batch * seq, d)
    w_up_all = peer_w_up.astype(BF16)
    w_down_all = peer_w_down.astype(BF16)
    for i in range(depth):
        j = i // 2
        if i % 2 == 0:
            qkv = norm_matmul(h, norm_mix[i], attn_w_qkv[j].astype(BF16), BF16)
            bias_tabs = _attention_bias_tables(attn_rel_bias[j], rows)
            mixed = neighbourhood_attention_core(qkv, bias_tabs, batch, seq)
            w_out = attn_w_o[j]
        else:
            y = norm_matmul(h, norm_mix[i], conv_w_in[j].astype(BF16), BF16)
            mixed = short_conv_gate(y, conv_w_conv[j], seq)
            w_out = conv_w_out[j]
        h, xt = matmul_residual_norm(mixed, w_out.astype(BF16), h, norm_ffn[i])
        keys = peer_sub_keys[i].reshape(PEER_HEADS * 2, PEER_N_KEYS, PEER_HALF_DIM).astype(BF16)
        r2, e2, n_sel, e1n = peer_weights(xt, peer_w_query[i].T.astype(BF16), keys)
        h = peer_dense(xt, r2, e2, n_sel, e1n, w_up_all, w_down_all, i, h, norm_final,
                       out_norm=(i == depth - 1))
    return h.reshape(batch, seq, d)
```

```python
import functools
import math

import numpy as np
import jax
import jax.numpy as jnp
from jax import lax
from jax.experimental import pallas as pl
from jax.experimental.pallas import tpu as pltpu

F32 = jnp.float32
BF16 = jnp.bfloat16

RMS_EPS = 1e-6
NEG_INF = -1e30

GRID_W = 64
N_HEADS = 16
HEAD_DIM = 64
MAX_WIN_ROWS = 8
WIN_COLS = 16
ATT_Q_ROWS = 4
ATT_BAND_ROWS = ATT_Q_ROWS + MAX_WIN_ROWS - 1
ATT_LANES = 128
ATT_BLOCKS_PER_STEP = 4

PEER_HEADS = 8
PEER_N_KEYS = 128
PEER_TOPK = 16
PEER_HALF_DIM = 128
GELU_HALF = 0.5

TOK_TILE = 512
PEER_PRE_TILE = 512
PEER_TOK_TILE = 256
PEER_EXP_TILE = 4096
PEER_STAGE_ROWS = 256
VMEM_LIMIT = 56 * 1024 * 1024


def _cparams(sem):
    return pltpu.CompilerParams(dimension_semantics=sem, vmem_limit_bytes=VMEM_LIMIT)


def _rms(x, g):
    return x * lax.rsqrt(jnp.mean(x * x, axis=-1, keepdims=True) + RMS_EPS) * g


def _norm_matmul_kernel(x_ref, g_ref, w_ref, o_ref):
    y = _rms(x_ref[...], g_ref[...])
    o_ref[...] = jnp.dot(y.astype(BF16), w_ref[...],
                         preferred_element_type=F32).astype(o_ref.dtype)


def norm_matmul(x, gain, w, out_dtype):
    t, d = x.shape
    n = w.shape[1]
    return pl.pallas_call(
        _norm_matmul_kernel,
        grid=(t // TOK_TILE,),
        in_specs=[pl.BlockSpec((TOK_TILE, d), lambda i: (i, 0)),
                  pl.BlockSpec((1, d), lambda i: (0, 0)),
                  pl.BlockSpec((d, n), lambda i: (0, 0))],
        out_specs=pl.BlockSpec((TOK_TILE, n), lambda i: (i, 0)),
        out_shape=jax.ShapeDtypeStruct((t, n), out_dtype),
        compiler_params=_cparams(("parallel",)),
        name="norm_matmul",
    )(x, gain.reshape(1, d), w)


def _matmul_residual_norm_kernel(a_ref, w_ref, h_ref, g_ref, o_ref, xt_ref):
    h_new = h_ref[...] + jnp.dot(a_ref[...], w_ref[...], preferred_element_type=F32)
    o_ref[...] = h_new
    xt_ref[...] = _rms(h_new, g_ref[...]).T.astype(xt_ref.dtype)


def matmul_residual_norm(a, w, h, gain):
    t, k = a.shape
    d = w.shape[1]
    return pl.pallas_call(
        _matmul_residual_norm_kernel,
        grid=(t // TOK_TILE,),
        in_specs=[pl.BlockSpec((TOK_TILE, k), lambda i: (i, 0)),
                  pl.BlockSpec((k, d), lambda i: (0, 0)),
                  pl.BlockSpec((TOK_TILE, d), lambda i: (i, 0)),
                  pl.BlockSpec((1, d), lambda i: (0, 0))],
        out_specs=[pl.BlockSpec((TOK_TILE, d), lambda i: (i, 0)),
                   pl.BlockSpec((d, TOK_TILE), lambda i: (0, i))],
        out_shape=[jax.ShapeDtypeStruct((t, d), F32), jax.ShapeDtypeStruct((d, t), BF16)],
        compiler_params=_cparams(("parallel",)),
        name="matmul_residual_norm",
    )(a, w, h, gain.reshape(1, d))


def _attention_bias_tables(rel_bias, rows):
    n_h = rel_bias.shape[0]
    n_dr = 2 * MAX_WIN_ROWS - 1
    w = GRID_W
    edge = w - WIN_COLS
    line = jnp.pad(rel_bias, ((0, 0), (0, 0), (edge, edge + 1)), mode="edge")
    flat = jnp.tile(line, (1, 1, w))[:, :, w - 1:w - 1 + w * (2 * w - 1)]
    toep = flat.reshape(n_h, n_dr, w, 2 * w - 1)[:, :, :, :w]
    qc = np.arange(w)[:, None]
    kc = np.arange(w)[None, :]
    ws = np.clip(qc - WIN_COLS // 2, 0, w - WIN_COLS)
    toep = jnp.where(((kc >= ws) & (kc < ws + WIN_COLS))[None, None], toep, NEG_INF).astype(BF16)
    pad = ATT_Q_ROWS - 1
    toep = jnp.pad(toep, ((0, 0), (pad, pad), (0, 0), (0, 0)), constant_values=NEG_INF)
    tabs = []
    for r0 in (0, ATT_Q_ROWS, rows - ATT_Q_ROWS):
        band = int(np.clip(r0 - MAX_WIN_ROWS // 2, 0, rows - ATT_BAND_ROWS))
        blocks = []
        for ql in range(ATT_Q_ROWS):
            lo = band - (r0 + ql) + MAX_WIN_ROWS - 1 + pad
            blk = toep[:, lo:lo + ATT_BAND_ROWS]
            blocks.append(blk.transpose(0, 2, 1, 3).reshape(n_h, w, ATT_BAND_ROWS * w))
        tab = jnp.stack(blocks, axis=1).reshape(n_h, ATT_Q_ROWS * w, ATT_BAND_ROWS * w)
        qr = r0 + np.arange(ATT_Q_ROWS)[:, None]
        kr = band + np.arange(ATT_BAND_ROWS)[None, :]
        rs = np.clip(qr - MAX_WIN_ROWS // 2, 0, rows - MAX_WIN_ROWS)
        row_ok = (kr >= rs) & (kr < rs + MAX_WIN_ROWS)
        row_ok = np.repeat(np.repeat(row_ok, w, axis=0), w, axis=1)
        tabs.append(jnp.where(row_ok[None], tab, NEG_INF))
    return jnp.stack(tabs)


def _attention_kernel(q_ref, k_ref, v_ref, *rest, rows):
    bias_refs, o_ref = rest[:ATT_BLOCKS_PER_STEP], rest[ATT_BLOCKS_PER_STEP]
    q_tok = ATT_Q_ROWS * GRID_W
    n_keys = ATT_BAND_ROWS * GRID_W
    scale = 1.0 / math.sqrt(HEAD_DIM)
    lane = lax.broadcasted_iota(jnp.int32, (q_tok, ATT_LANES), 1)
    for sub in range(ATT_BLOCKS_PER_STEP):
        rb = pl.program_id(2) * ATT_BLOCKS_PER_STEP + sub
        band = jnp.clip(rb * ATT_Q_ROWS - MAX_WIN_ROWS // 2, 0, rows - ATT_BAND_ROWS)
        start = pl.multiple_of(band * GRID_W, GRID_W)
        kb = k_ref[pl.ds(start, n_keys), :]
        vb = v_ref[pl.ds(start, n_keys), :]
        q = q_ref[sub * q_tok:(sub + 1) * q_tok, :]
        outs = []
        for hh in range(ATT_LANES // HEAD_DIM):
            in_head = (lane < HEAD_DIM) if hh == 0 else (lane >= HEAD_DIM)
            qm = jnp.where(in_head, q, jnp.zeros_like(q)) * jnp.asarray(scale, q.dtype)
            s = lax.dot_general(qm, kb, (((1,), (1,)), ((), ())), preferred_element_type=F32)
            s = s + bias_refs[sub][0, hh].astype(F32)
            m = jnp.max(s, axis=-1, keepdims=True)
            p = jnp.exp(s - m)
            l = jnp.sum(p, axis=-1, keepdims=True)
            o = jnp.dot(p.astype(BF16), vb, preferred_element_type=F32)
            outs.append(o / l)
        o_ref[sub * q_tok:(sub + 1) * q_tok, :] = jnp.where(lane < HEAD_DIM, outs[0],
                                                             outs[1]).astype(o_ref.dtype)


def neighbourhood_attention_core(qkv, bias_tabs, batch, seq):
    t = qkv.shape[0]
    d = qkv.shape[1] // 3
    rows = seq // GRID_W
    n_rb = rows // ATT_Q_ROWS
    n_steps = n_rb // ATT_BLOCKS_PER_STEP
    n_hp = d // ATT_LANES
    q_tok = ATT_Q_ROWS * GRID_W * ATT_BLOCKS_PER_STEP

    def bias_spec(sub):
        def index_map(b, hp, st):
            rb = st * ATT_BLOCKS_PER_STEP + sub
            return (jnp.where(rb == 0, 0, jnp.where(rb == n_rb - 1, 2, 1)), hp, 0, 0)
        return pl.BlockSpec((1, 2, ATT_Q_ROWS * GRID_W, ATT_BAND_ROWS * GRID_W), index_map)

    return pl.pallas_call(
        functools.partial(_attention_kernel, rows=rows),
        grid=(batch, n_hp, n_steps),
        in_specs=[pl.BlockSpec((q_tok, ATT_LANES), lambda b, hp, st: (b * n_steps + st, hp)),
                  pl.BlockSpec((seq, ATT_LANES), lambda b, hp, st: (b, n_hp + hp)),
                  pl.BlockSpec((seq, ATT_LANES), lambda b, hp, st: (b, 2 * n_hp + hp))]
                 + [bias_spec(sub) for sub in range(ATT_BLOCKS_PER_STEP)],
        out_specs=pl.BlockSpec((q_tok, ATT_LANES), lambda b, hp, st: (b * n_steps + st, hp)),
        out_shape=jax.ShapeDtypeStruct((t, d), BF16),
        compiler_params=_cparams(("parallel", "parallel", "arbitrary")),
        name="natten_core",
    )(qkv, qkv, qkv, *([bias_tabs] * ATT_BLOCKS_PER_STEP))


def _short_conv_kernel(b_ref, c_ref, v_ref, cp_ref, vp_ref, cn_ref, vn_ref, w_ref, o_ref, *, seq):
    i = pl.program_id(0)
    tm = b_ref.shape[0]
    halo = cp_ref.shape[0]
    u = c_ref[...].astype(F32) * v_ref[...].astype(F32)
    u_before = cp_ref[halo - 1:halo, :].astype(F32) * vp_ref[halo - 1:halo, :].astype(F32)
    u_after = cn_ref[0:1, :].astype(F32) * vn_ref[0:1, :].astype(F32)
    row = lax.broadcasted_iota(jnp.int32, u.shape, 0)
    pos = (i * tm + row) % seq
    u_prev = jnp.where(row == 0, u_before, pltpu.roll(u, 1, 0))
    u_prev = jnp.where(pos == 0, 0.0, u_prev)
    u_next = jnp.where(row == tm - 1, u_after, pltpu.roll(u, tm - 1, 0))
    u_next = jnp.where(pos == seq - 1, 0.0, u_next)
    conv = u_prev * w_ref[0:1, :] + u * w_ref[1:2, :] + u_next * w_ref[2:3, :]
    o_ref[...] = (b_ref[...].astype(F32) * conv).astype(o_ref.dtype)


def short_conv_gate(y, w_conv, seq):
    t = y.shape[0]
    d = y.shape[1] // 3
    cw = 512
    nj = d // cw
    tm = TOK_TILE
    halo = 16
    n_halo = t // halo

    def prev_map(off):
        return lambda i, j: (jnp.maximum(i * (tm // halo) - 1, 0), off * nj + j)

    def next_map(off):
        return lambda i, j: (jnp.minimum((i + 1) * (tm // halo), n_halo - 1), off * nj + j)

    return pl.pallas_call(
        functools.partial(_short_conv_kernel, seq=seq),
        grid=(t // tm, nj),
        in_specs=[pl.BlockSpec((tm, cw), lambda i, j: (i, j)),
                  pl.BlockSpec((tm, cw), lambda i, j: (i, nj + j)),
                  pl.BlockSpec((tm, cw), lambda i, j: (i, 2 * nj + j)),
                  pl.BlockSpec((halo, cw), prev_map(1)),
                  pl.BlockSpec((halo, cw), prev_map(2)),
                  pl.BlockSpec((halo, cw), next_map(1)),
                  pl.BlockSpec((halo, cw), next_map(2)),
                  pl.BlockSpec((3, cw), lambda i, j: (0, j))],
        out_specs=pl.BlockSpec((tm, cw), lambda i, j: (i, j)),
        out_shape=jax.ShapeDtypeStruct((t, d), BF16),
        compiler_params=_cparams(("parallel", "parallel")),
        name="short_conv_gate",
    )(y, y, y, y, y, y, y, w_conv)


def _top16(sc, want_rank):
    n, l = sc.shape
    iota_n = lax.broadcasted_iota(jnp.int32, (n, l), 0).astype(F32)
    iota_k = lax.broadcasted_iota(jnp.int32, (PEER_TOPK, l), 0).astype(F32)

    def body(r, carry):
        sc, rank, sv, si = carry
        rf = jnp.asarray(r, jnp.int32).astype(F32)
        m = jnp.max(sc, axis=0, keepdims=True)
        idx = jnp.min(jnp.where(sc == m, iota_n, float(n)), axis=0, keepdims=True)
        sel = iota_n == idx
        if want_rank:
            rank = jnp.where(sel, rf, rank)
        sc = jnp.where(sel, -jnp.inf, sc)
        sv = jnp.where(iota_k == rf, m, sv)
        si = jnp.where(iota_k == rf, idx, si)
        return sc, rank, sv, si

    init = (sc, jnp.full((n, l), float(PEER_TOPK), F32),
            jnp.zeros((PEER_TOPK, l), F32), jnp.zeros((PEER_TOPK, l), F32))
    _, rank, sv, si = lax.fori_loop(0, PEER_TOPK, body, init)
    return sv, si, rank


def _merge_top16(sv1, sv2):
    k = sv1.shape[0]
    iota_k = lax.broadcasted_iota(jnp.int32, sv1.shape, 0).astype(F32)
    top = sv1[0:1] + sv2[0:1]

    def body(_, carry):
        front, cnt, z = carry
        m = jnp.max(front, axis=0, keepdims=True)
        a_star = jnp.min(jnp.where(front == m, iota_k, float(k)), axis=0, keepdims=True)
        sel = iota_k == a_star
        z = z + jnp.exp(m - top)
        cnt = jnp.where(sel, cnt + 1.0, cnt)
        c_sel = jnp.sum(jnp.where(sel, cnt, 0.0), axis=0, keepdims=True)
        nxt = jnp.sum(jnp.where(iota_k == c_sel, sv2, 0.0), axis=0, keepdims=True)
        nxt = jnp.where(c_sel >= float(k), -jnp.inf, nxt)
        front = jnp.where(sel, sv1 + nxt, front)
        return front, cnt, z

    init = (sv1 + sv2[0:1], jnp.zeros(sv1.shape, F32), jnp.zeros(top.shape, F32))
    _, cnt, z = lax.fori_loop(0, PEER_TOPK, body, init)
    return cnt, z


def _tree(op, xs):
    while len(xs) > 1:
        xs = [op(xs[k], xs[k + 1]) for k in range(0, len(xs) - 1, 2)] + (xs[-1:] if len(xs) % 2 else [])
    return xs[0]


def _all_sublanes(op, x):
    for shift in (4, 2, 1):
        x = op(x, pltpu.roll(x, shift, 0))
    return x


def _count_greater(top, x):
    sel = jnp.where
    c8 = top[7] > x
    c4 = sel(c8, top[11], top[3]) > x
    c2 = sel(c8, sel(c4, top[13], top[9]), sel(c4, top[5], top[1])) > x
    c1 = sel(c8, sel(c4, sel(c2, top[14], top[12]), sel(c2, top[10], top[8])),
             sel(c4, sel(c2, top[6], top[4]), sel(c2, top[2], top[0]))) > x
    c16 = top[15] > x
    zero = jnp.zeros_like(x)
    return ((sel(c8, 8.0, zero) + sel(c4, 4.0, zero)) + (sel(c2, 2.0, zero) + sel(c1, 1.0, zero))
            + sel(c16, 1.0, zero))


def _sorting_network(n):
    def merge(lo, hi, r):
        step = r * 2
        if step < hi - lo:
            yield from merge(lo, hi, step)
            yield from merge(lo + r, hi, step)
            yield from [(k, k + r) for k in range(lo + r, hi - r, step)]
        else:
            yield (lo, lo + r)

    def sort(lo, hi):
        if hi - lo >= 1:
            mid = lo + (hi - lo) // 2
            yield from sort(lo, mid)
            yield from sort(mid + 1, hi)
            yield from merge(lo, hi, 1)

    return tuple(sort(0, n - 1))


_SORT16 = _sorting_network(PEER_TOPK)


def _compare_exchange(xs, i, j):
    xs[i], xs[j] = jnp.maximum(xs[i], xs[j]), jnp.minimum(xs[i], xs[j])


def _top16_sorted(rows):
    xs = list(rows)
    for i, j in _SORT16:
        _compare_exchange(xs, i, j)
    k = len(xs)
    for shift in (4, 2, 1):
        ys = [pltpu.roll(x, shift, 0) for x in xs]
        xs = [jnp.maximum(xs[v], ys[k - 1 - v]) for v in range(k)]
        stride = k // 2
        while stride:
            for v in range(k):
                if not v & stride:
                    _compare_exchange(xs, v, v + stride)
            stride //= 2
    return xs


def _peer_scores(h, keys_ref, q_scr, s_scr):
    for p in range(2):
        r0 = pl.multiple_of((2 * h + p) * PEER_HALF_DIM, PEER_HALF_DIM)
        s_scr[h, p] = jnp.dot(keys_ref[2 * h + p], q_scr[pl.ds(r0, PEER_HALF_DIM), :],
                              preferred_element_type=F32)


def _peer_head_exact(h, lt, s_scr, r2_ref, e2_ref, n_ref, e1_ref):
    n_keys, lanes = PEER_N_KEYS, 128
    iota_n = lax.broadcasted_iota(jnp.int32, (n_keys, lanes), 0).astype(F32)
    ls = slice(lt * lanes, (lt + 1) * lanes)
    s1 = s_scr[h, 0, :, ls]
    s2 = s_scr[h, 1, :, ls]
    sv1, si1, _ = _top16(s1, want_rank=False)
    sv2, _, rank2 = _top16(s2, want_rank=True)
    cnt, z = _merge_top16(sv1, sv2)
    n_sel = jnp.zeros((n_keys, lanes), F32)
    for a in range(PEER_TOPK):
        n_sel = jnp.where(iota_n == si1[a:a + 1, :], cnt[a:a + 1, :], n_sel)
    r2_ref[h, :, ls] = rank2.astype(r2_ref.dtype)
    e2_ref[h, :, ls] = jnp.exp(s2 - sv2[0:1, :]).astype(e2_ref.dtype)
    n_ref[h, :, ls] = n_sel
    e1_ref[h, :, ls] = jnp.exp(s1 - sv1[0:1, :]) * (GELU_HALF / z)


def _peer_head_top_values(h, s_scr, t_scr, sv_scr):
    n_keys, lanes, sub = PEER_N_KEYS, 128, 8

    head_row = lax.broadcasted_iota(jnp.int32, (sub, lanes), 0) == h

    def lane_tile(lt, _):
        ls = pl.ds(pl.multiple_of(lt * lanes, lanes), lanes)
        for p in range(2):
            rows = [s_scr[h, p, v * sub:(v + 1) * sub, ls] for v in range(n_keys // sub)]
            top = _top16_sorted(rows)
            for r in range(PEER_TOPK):
                t_scr[h, p, r, :, ls] = top[r]
                sv_scr[p, r, :, ls] = jnp.where(head_row, top[r], sv_scr[p, r, :, ls])
        return 0

    lax.fori_loop(0, s_scr.shape[3] // lanes, lane_tile, 0)


def _peer_head_distinct(h, s_scr, t_scr, cnt_scr, z_scr, r2_ref, e2_ref, n_ref, e1_ref):
    n_keys, lanes, sub = PEER_N_KEYS, 128, 8
    n_vr = n_keys // sub
    one = jnp.ones((sub, lanes), F32)
    zero = jnp.zeros((sub, lanes), F32)
    bad_tiles = []
    cnt_rows = [cnt_scr[a, pl.ds(h, 1), :] for a in range(PEER_TOPK)]
    inv_row = GELU_HALF / z_scr[pl.ds(h, 1), :]
    for lt in range(s_scr.shape[3] // lanes):
        ls = slice(lt * lanes, (lt + 1) * lanes)
        bad = zero
        for p in (1, 0):
            top = [t_scr[h, p, b, :, ls] for b in range(PEER_TOPK)]
            tied = _tree(jnp.maximum, [jnp.where(top[b] > top[b + 1], zero, one)
                                       for b in range(PEER_TOPK - 1)])
            n_top = zero
            if p == 1:
                for v in range(n_vr):
                    vs = slice(v * sub, (v + 1) * sub)
                    x = s_scr[h, 1, vs, ls]
                    rank = _count_greater(top, x)
                    r2_ref[h, vs, ls] = rank.astype(r2_ref.dtype)
                    e2_ref[h, vs, ls] = jnp.exp(x - top[0]).astype(e2_ref.dtype)
                    n_top = n_top + jnp.where(rank < float(PEER_TOPK), one, zero)
            else:
                cnt_b = [jnp.broadcast_to(r[:, ls], (sub, lanes)) for r in cnt_rows]
                inv_b = jnp.broadcast_to(inv_row[:, ls], (sub, lanes))
                for v in range(n_vr):
                    vs = slice(v * sub, (v + 1) * sub)
                    x = s_scr[h, 0, vs, ls]
                    n_sel = zero
                    for a in range(PEER_TOPK):
                        n_sel = jnp.where(x == top[a], cnt_b[a], n_sel)
                    n_ref[h, vs, ls] = n_sel
                    e1_ref[h, vs, ls] = jnp.exp(x - top[0]) * inv_b
                    n_top = n_top + jnp.where(x >= top[PEER_TOPK - 1], one, zero)
            miscount = jnp.where(_all_sublanes(jnp.add, n_top) != float(PEER_TOPK), one, zero)
            bad = jnp.maximum(bad, jnp.maximum(tied, miscount))
        bad_tiles.append(bad)
    return bad_tiles


def _peer_weights_kernel(xt_ref, wq_ref, keys_ref, r2_ref, e2_ref, n_ref, e1_ref,
                         q_scr, s_scr, t_scr, sv_scr, cnt_scr, z_scr):
    q_scr[...] = jnp.dot(wq_ref[...], xt_ref[...], preferred_element_type=F32).astype(BF16)
    sv_scr[...] = jnp.zeros_like(sv_scr)

    def top_values(h, _):
        _peer_scores(h, keys_ref, q_scr, s_scr)
        _peer_head_top_values(h, s_scr, t_scr, sv_scr)
        return 0

    lax.fori_loop(0, PEER_HEADS, top_values, 0)
    cnt, z = _merge_top16(sv_scr[0], sv_scr[1])
    cnt_scr[...] = cnt
    z_scr[...] = z[0]

    def weights(h, _):
        bad_tiles = _peer_head_distinct(h, s_scr, t_scr, cnt_scr, z_scr,
                                        r2_ref, e2_ref, n_ref, e1_ref)

        @pl.when(jnp.max(_tree(jnp.maximum, bad_tiles)) > 0.0)
        def _():
            for lt, bad in enumerate(bad_tiles):
                @pl.when(jnp.max(bad) > 0.0)
                def _():
                    _peer_head_exact(h, lt, s_scr, r2_ref, e2_ref, n_ref, e1_ref)

        return 0

    lax.fori_loop(0, PEER_HEADS, weights, 0)


def peer_weights(xt, wq_t, keys):
    d, t = xt.shape
    tm = PEER_PRE_TILE
    out16 = jax.ShapeDtypeStruct((PEER_HEADS, PEER_N_KEYS, t), BF16)
    out32 = jax.ShapeDtypeStruct((PEER_HEADS, PEER_N_KEYS, t), F32)
    ospec = pl.BlockSpec((PEER_HEADS, PEER_N_KEYS, tm), lambda i: (0, 0, i))
    return pl.pallas_call(
        _peer_weights_kernel,
        grid=(t // tm,),
        in_specs=[pl.BlockSpec((d, tm), lambda i: (0, i)),
                  pl.BlockSpec(wq_t.shape, lambda i: (0, 0)),
                  pl.BlockSpec(keys.shape, lambda i: (0, 0, 0))],
        out_specs=[ospec, ospec, ospec, ospec],
        out_shape=[out16, out16, out32, out32],
        scratch_shapes=[pltpu.VMEM((wq_t.shape[0], tm), BF16),
                        pltpu.VMEM((PEER_HEADS, 2, PEER_N_KEYS, tm), F32),
                        pltpu.VMEM((PEER_HEADS, 2, PEER_TOPK, 8, tm), F32),
                        pltpu.VMEM((2, PEER_TOPK, PEER_HEADS, tm), F32),
                        pltpu.VMEM((PEER_TOPK, PEER_HEADS, tm), F32),
                        pltpu.VMEM((PEER_HEADS, tm), F32)],
        compiler_params=_cparams(("parallel",)),
        name="peer_weights",
    )(xt, wq_t, keys)


def _peer_hidden_rows(i, il, a_ref, hid_ref, r2_ref, e2_ref, n_ref, e1_ref):
    nk, lanes, pack = PEER_N_KEYS, 128, 16
    row0 = il * nk
    n_rows = [n_ref[h, pl.ds(i, 1), :] for h in range(PEER_HEADS)]
    e1_rows = [e1_ref[h, pl.ds(i, 1), :] for h in range(PEER_HEADS)]
    zero = jnp.zeros((pack, lanes), BF16)
    for lt in range(a_ref.shape[1] // lanes):
        ls = slice(lt * lanes, (lt + 1) * lanes)
        n_b = [jnp.broadcast_to(r[:, ls], (pack, lanes)).astype(BF16) for r in n_rows]
        e1_b = [jnp.broadcast_to(r[:, ls], (pack, lanes)).astype(BF16) for r in e1_rows]
        for jb in range(nk // pack):
            js = slice(jb * pack, (jb + 1) * pack)
            g = None
            for h in range(PEER_HEADS):
                term = jnp.where(r2_ref[h, js, ls] < n_b[h], e2_ref[h, js, ls] * e1_b[h], zero)
                g = term if g is None else g + term
            rs = slice(row0 + jb * pack, row0 + (jb + 1) * pack)
            a = a_ref[rs, ls]
            act = a * (1.0 + lax.erf(a * math.sqrt(0.5)))
            hid_ref[rs, ls] = act.astype(BF16) * g


def _peer_dense_kernel(xt_ref, r2_ref, e2_ref, n_ref, e1_ref, wup_ref, wdn_ref, h_ref, g_ref, o_ref,
                       acc_ref, a0_ref, a1_ref, hid0_ref, hid1_ref, r2_scr, e2_scr, *, out_norm):
    e = pl.program_id(1)
    grp = a0_ref.shape[0]
    rows_per_grp = grp // PEER_N_KEYS
    n_grp = wup_ref.shape[0] // grp

    @pl.when(e == 0)
    def _():
        acc_ref[...] = jnp.zeros_like(acc_ref)
        r2_scr[...] = r2_ref[...]
        e2_scr[...] = e2_ref[...]

    a_refs = (a0_ref, a1_ref)
    hid_refs = (hid0_ref, hid1_ref)
    for p in range(n_grp + 2):
        if p < n_grp:
            a_refs[p % 2][...] = jnp.dot(wup_ref[p * grp:(p + 1) * grp, :], xt_ref[...],
                                         preferred_element_type=F32)
        if 1 <= p <= n_grp:
            g = p - 1
            for il in range(rows_per_grp):
                _peer_hidden_rows((e * n_grp + g) * rows_per_grp + il, il, a_refs[g % 2],
                                  hid_refs[g % 2], r2_scr, e2_scr, n_ref, e1_ref)
        if p >= 2:
            g = p - 2
            acc_ref[...] += lax.dot_general(hid_refs[g % 2][...], wdn_ref[g * grp:(g + 1) * grp, :],
                                            (((0,), (0,)), ((), ())), preferred_element_type=F32)

    @pl.when(e == pl.num_programs(1) - 1)
    def _():
        h_new = h_ref[...] + acc_ref[...]
        o_ref[...] = _rms(h_new, g_ref[...]) if out_norm else h_new


def peer_dense(xt, r2, e2, n_sel, e1n, w_up, w_down, layer, h, out_gain, out_norm):
    d, t = xt.shape
    n_exp = w_up.shape[1]
    tm, te = PEER_TOK_TILE, PEER_EXP_TILE
    wspec = pl.BlockSpec((PEER_HEADS, PEER_N_KEYS, tm), lambda i, e: (0, 0, i))
    return pl.pallas_call(
        functools.partial(_peer_dense_kernel, out_norm=out_norm),
        grid=(t // tm, n_exp // te),
        in_specs=[pl.BlockSpec((d, tm), lambda i, e: (0, i)),
                  wspec, wspec, wspec, wspec,
                  pl.BlockSpec((None, te, d), lambda i, e: (layer, e, 0)),
                  pl.BlockSpec((None, te, d), lambda i, e: (layer, e, 0)),
                  pl.BlockSpec((tm, d), lambda i, e: (i, 0)),
                  pl.BlockSpec((1, d), lambda i, e: (0, 0))],
        out_specs=pl.BlockSpec((tm, d), lambda i, e: (i, 0)),
        out_shape=jax.ShapeDtypeStruct((t, d), F32),
        scratch_shapes=[pltpu.VMEM((tm, d), F32),
                        pltpu.VMEM((PEER_STAGE_ROWS, tm), F32),
                        pltpu.VMEM((PEER_STAGE_ROWS, tm), F32),
                        pltpu.VMEM((PEER_STAGE_ROWS, tm), BF16),
                        pltpu.VMEM((PEER_STAGE_ROWS, tm), BF16),
                        pltpu.VMEM((PEER_HEADS, PEER_N_KEYS, tm), BF16),
                        pltpu.VMEM((PEER_HEADS, PEER_N_KEYS, tm), BF16)],
        compiler_params=_cparams(("parallel", "arbitrary")),
        name="peer_dense",
    )(xt, r2, e2, n_sel, e1n, w_up, w_down, h, out_gain.reshape(1, d))


def kernel(x, norm_mix, norm_ffn, norm_final, attn_w_qkv, attn_w_o, attn_rel_bias,
           conv_w_in, conv_w_conv, conv_w_out, peer_w_query, peer_sub_keys,
           peer_w_up, peer_w_down):
    batch, seq, d = x.shape
    depth = norm_mix.shape[0]
    rows = seq // GRID_W
    h = x.reshape(batch * seq, d)
    w_up_all = peer_w_up.astype(BF16)
    w_down_all = peer_w_down.astype(BF16)
    for i in range(depth):
        j = i // 2
        if i % 2 == 0:
            qkv = norm_matmul(h, norm_mix[i], attn_w_qkv[j].astype(BF16), BF16)
            bias_tabs = _attention_bias_tables(attn_rel_bias[j], rows)
            mixed = neighbourhood_attention_core(qkv, bias_tabs, batch, seq)
            w_out = attn_w_o[j]
        else:
            y = norm_matmul(h, norm_mix[i], conv_w_in[j].astype(BF16), BF16)
            mixed = short_conv_gate(y, conv_w_conv[j], seq)
            w_out = conv_w_out[j]
        h, xt = matmul_residual_norm(mixed, w_out.astype(BF16), h, norm_ffn[i])
        keys = peer_sub_keys[i].reshape(PEER_HEADS * 2, PEER_N_KEYS, PEER_HALF_DIM).astype(BF16)
        r2, e2, n_sel, e1n = peer_weights(xt, peer_w_query[i].T.astype(BF16), keys)
        h = peer_dense(xt, r2, e2, n_sel, e1n, w_up_all, w_down_all, i, h, norm_final,
                       out_norm=(i == depth - 1))
    return h.reshape(batch, seq, d)
```

```python
import functools
import math

import numpy as np
import jax
import jax.numpy as jnp
from jax import lax
from jax.experimental import pallas as pl
from jax.experimental.pallas import tpu as pltpu

F32 = jnp.float32
BF16 = jnp.bfloat16

RMS_EPS = 1e-6
NEG_INF = -1e30

GRID_W = 64
N_HEADS = 16
HEAD_DIM = 64
MAX_WIN_ROWS = 8
WIN_COLS = 16
ATT_Q_ROWS = 4
ATT_BAND_ROWS = ATT_Q_ROWS + MAX_WIN_ROWS - 1
ATT_LANES = 128
ATT_BLOCKS_PER_STEP = 8

PEER_HEADS = 8
PEER_N_KEYS = 128
PEER_TOPK = 16
PEER_HALF_DIM = 128
GELU_HALF = 0.5

TOK_TILE = 1024
PEER_PRE_TILE = 512
PEER_TOK_TILE = 256
PEER_EXP_TILE = 4096
PEER_STAGE_ROWS = 256
VMEM_LIMIT = 56 * 1024 * 1024


def _cparams(sem):
    return pltpu.CompilerParams(dimension_semantics=sem, vmem_limit_bytes=VMEM_LIMIT)


def _rms(x, g):
    return x * lax.rsqrt(jnp.mean(x * x, axis=-1, keepdims=True) + RMS_EPS) * g


def _norm_matmul_kernel(x_ref, g_ref, w_ref, o_ref):
    y = _rms(x_ref[...], g_ref[...])
    o_ref[...] = jnp.dot(y.astype(BF16), w_ref[...],
                         preferred_element_type=F32).astype(o_ref.dtype)


def norm_matmul(x, gain, w, out_dtype):
    t, d = x.shape
    n = w.shape[1]
    return pl.pallas_call(
        _norm_matmul_kernel,
        grid=(t // TOK_TILE,),
        in_specs=[pl.BlockSpec((TOK_TILE, d), lambda i: (i, 0)),
                  pl.BlockSpec((1, d), lambda i: (0, 0)),
                  pl.BlockSpec((d, n), lambda i: (0, 0))],
        out_specs=pl.BlockSpec((TOK_TILE, n), lambda i: (i, 0)),
        out_shape=jax.ShapeDtypeStruct((t, n), out_dtype),
        compiler_params=_cparams(("parallel",)),
        name="norm_matmul",
    )(x, gain.reshape(1, d), w)


def _matmul_residual_norm_kernel(a_ref, w_ref, h_ref, g_ref, o_ref, xt_ref):
    h_new = h_ref[...] + jnp.dot(a_ref[...], w_ref[...], preferred_element_type=F32)
    o_ref[...] = h_new
    xt_ref[...] = _rms(h_new, g_ref[...]).T.astype(xt_ref.dtype)


def matmul_residual_norm(a, w, h, gain):
    t, k = a.shape
    d = w.shape[1]
    return pl.pallas_call(
        _matmul_residual_norm_kernel,
        grid=(t // TOK_TILE,),
        in_specs=[pl.BlockSpec((TOK_TILE, k), lambda i: (i, 0)),
                  pl.BlockSpec((k, d), lambda i: (0, 0)),
                  pl.BlockSpec((TOK_TILE, d), lambda i: (i, 0)),
                  pl.BlockSpec((1, d), lambda i: (0, 0))],
        out_specs=[pl.BlockSpec((TOK_TILE, d), lambda i: (i, 0)),
                   pl.BlockSpec((d, TOK_TILE), lambda i: (0, i))],
        out_shape=[jax.ShapeDtypeStruct((t, d), F32), jax.ShapeDtypeStruct((d, t), BF16)],
        compiler_params=_cparams(("parallel",)),
        name="matmul_residual_norm",
    )(a, w, h, gain.reshape(1, d))


def _attention_bias_tables(rel_bias, rows):
    n_h = rel_bias.shape[0]
    n_dr = 2 * MAX_WIN_ROWS - 1
    w = GRID_W
    edge = w - WIN_COLS
    line = jnp.pad(rel_bias, ((0, 0), (0, 0), (edge, edge + 1)), mode="edge")
    flat = jnp.tile(line, (1, 1, w))[:, :, w - 1:w - 1 + w * (2 * w - 1)]
    toep = flat.reshape(n_h, n_dr, w, 2 * w - 1)[:, :, :, :w]
    qc = np.arange(w)[:, None]
    kc = np.arange(w)[None, :]
    ws = np.clip(qc - WIN_COLS // 2, 0, w - WIN_COLS)
    toep = jnp.where(((kc >= ws) & (kc < ws + WIN_COLS))[None, None], toep, NEG_INF).astype(BF16)
    pad = ATT_Q_ROWS - 1
    toep = jnp.pad(toep, ((0, 0), (pad, pad), (0, 0), (0, 0)), constant_values=NEG_INF)
    tabs = []
    for r0 in (0, ATT_Q_ROWS, rows - ATT_Q_ROWS):
        band = int(np.clip(r0 - MAX_WIN_ROWS // 2, 0, rows - ATT_BAND_ROWS))
        blocks = []
        for ql in range(ATT_Q_ROWS):
            lo = band - (r0 + ql) + MAX_WIN_ROWS - 1 + pad
            blk = toep[:, lo:lo + ATT_BAND_ROWS]
            blocks.append(blk.transpose(0, 2, 1, 3).reshape(n_h, w, ATT_BAND_ROWS * w))
        tab = jnp.stack(blocks, axis=1).reshape(n_h, ATT_Q_ROWS * w, ATT_BAND_ROWS * w)
        qr = r0 + np.arange(ATT_Q_ROWS)[:, None]
        kr = band + np.arange(ATT_BAND_ROWS)[None, :]
        rs = np.clip(qr - MAX_WIN_ROWS // 2, 0, rows - MAX_WIN_ROWS)
        row_ok = (kr >= rs) & (kr < rs + MAX_WIN_ROWS)
        row_ok = np.repeat(np.repeat(row_ok, w, axis=0), w, axis=1)
        tabs.append(jnp.where(row_ok[None], tab, NEG_INF))
    return jnp.stack(tabs)


def _attention_kernel(q_ref, k_ref, v_ref, *rest, rows):
    bias_refs, o_ref = rest[:ATT_BLOCKS_PER_STEP], rest[ATT_BLOCKS_PER_STEP]
    q_tok = ATT_Q_ROWS * GRID_W
    n_keys = ATT_BAND_ROWS * GRID_W
    scale = 1.0 / math.sqrt(HEAD_DIM)
    lane = lax.broadcasted_iota(jnp.int32, (q_tok, ATT_LANES), 1)
    for sub in range(ATT_BLOCKS_PER_STEP):
        rb = pl.program_id(2) * ATT_BLOCKS_PER_STEP + sub
        band = jnp.clip(rb * ATT_Q_ROWS - MAX_WIN_ROWS // 2, 0, rows - ATT_BAND_ROWS)
        start = pl.multiple_of(band * GRID_W, GRID_W)
        kb = k_ref[pl.ds(start, n_keys), :]
        vb = v_ref[pl.ds(start, n_keys), :]
        q = q_ref[sub * q_tok:(sub + 1) * q_tok, :]
        outs = []
        for hh in range(ATT_LANES // HEAD_DIM):
            in_head = (lane < HEAD_DIM) if hh == 0 else (lane >= HEAD_DIM)
            qm = jnp.where(in_head, q, jnp.zeros_like(q)) * jnp.asarray(scale, q.dtype)
            s = lax.dot_general(qm, kb, (((1,), (1,)), ((), ())), preferred_element_type=F32)
            s = s + bias_refs[sub][0, hh].astype(F32)
            m = jnp.max(s, axis=-1, keepdims=True)
            p = jnp.exp(s - m)
            l = jnp.sum(p, axis=-1, keepdims=True)
            o = jnp.dot(p.astype(BF16), vb, preferred_element_type=F32)
            outs.append(o / l)
        o_ref[sub * q_tok:(sub + 1) * q_tok, :] = jnp.where(lane < HEAD_DIM, outs[0],
                                                             outs[1]).astype(o_ref.dtype)


def neighbourhood_attention_core(qkv, bias_tabs, batch, seq):
    t = qkv.shape[0]
    d = qkv.shape[1] // 3
    rows = seq // GRID_W
    n_rb = rows // ATT_Q_ROWS
    n_steps = n_rb // ATT_BLOCKS_PER_STEP
    n_hp = d // ATT_LANES
    q_tok = ATT_Q_ROWS * GRID_W * ATT_BLOCKS_PER_STEP

    def bias_spec(sub):
        def index_map(b, hp, st):
            rb = st * ATT_BLOCKS_PER_STEP + sub
            return (jnp.where(rb == 0, 0, jnp.where(rb == n_rb - 1, 2, 1)), hp, 0, 0)
        return pl.BlockSpec((1, 2, ATT_Q_ROWS * GRID_W, ATT_BAND_ROWS * GRID_W), index_map)

    return pl.pallas_call(
        functools.partial(_attention_kernel, rows=rows),
        grid=(batch, n_hp, n_steps),
        in_specs=[pl.BlockSpec((q_tok, ATT_LANES), lambda b, hp, st: (b * n_steps + st, hp)),
                  pl.BlockSpec((seq, ATT_LANES), lambda b, hp, st: (b, n_hp + hp)),
                  pl.BlockSpec((seq, ATT_LANES), lambda b, hp, st: (b, 2 * n_hp + hp))]
                 + [bias_spec(sub) for sub in range(ATT_BLOCKS_PER_STEP)],
        out_specs=pl.BlockSpec((q_tok, ATT_LANES), lambda b, hp, st: (b * n_steps + st, hp)),
        out_shape=jax.ShapeDtypeStruct((t, d), BF16),
        compiler_params=_cparams(("parallel", "parallel", "arbitrary")),
        name="natten_core",
    )(qkv, qkv, qkv, *([bias_tabs] * ATT_BLOCKS_PER_STEP))


def _short_conv_kernel(b_ref, c_ref, v_ref, cp_ref, vp_ref, cn_ref, vn_ref, w_ref, o_ref, *, seq):
    i = pl.program_id(0)
    tm = b_ref.shape[0]
    halo = cp_ref.shape[0]
    u = c_ref[...].astype(F32) * v_ref[...].astype(F32)
    u_before = cp_ref[halo - 1:halo, :].astype(F32) * vp_ref[halo - 1:halo, :].astype(F32)
    u_after = cn_ref[0:1, :].astype(F32) * vn_ref[0:1, :].astype(F32)
    row = lax.broadcasted_iota(jnp.int32, u.shape, 0)
    pos = (i * tm + row) % seq
    u_prev = jnp.where(row == 0, u_before, pltpu.roll(u, 1, 0))
    u_prev = jnp.where(pos == 0, 0.0, u_prev)
    u_next = jnp.where(row == tm - 1, u_after, pltpu.roll(u, tm - 1, 0))
    u_next = jnp.where(pos == seq - 1, 0.0, u_next)
    conv = u_prev * w_ref[0:1, :] + u * w_ref[1:2, :] + u_next * w_ref[2:3, :]
    o_ref[...] = (b_ref[...].astype(F32) * conv).astype(o_ref.dtype)


def short_conv_gate(y, w_conv, seq):
    t = y.shape[0]
    d = y.shape[1] // 3
    cw = 512
    nj = d // cw
    tm = TOK_TILE
    halo = 16
    n_halo = t // halo

    def prev_map(off):
        return lambda i, j: (jnp.maximum(i * (tm // halo) - 1, 0), off * nj + j)

    def next_map(off):
        return lambda i, j: (jnp.minimum((i + 1) * (tm // halo), n_halo - 1), off * nj + j)

    return pl.pallas_call(
        functools.partial(_short_conv_kernel, seq=seq),
        grid=(t // tm, nj),
        in_specs=[pl.BlockSpec((tm, cw), lambda i, j: (i, j)),
                  pl.BlockSpec((tm, cw), lambda i, j: (i, nj + j)),
                  pl.BlockSpec((tm, cw), lambda i, j: (i, 2 * nj + j)),
                  pl.BlockSpec((halo, cw), prev_map(1)),
                  pl.BlockSpec((halo, cw), prev_map(2)),
                  pl.BlockSpec((halo, cw), next_map(1)),
                  pl.BlockSpec((halo, cw), next_map(2)),
                  pl.BlockSpec((3, cw), lambda i, j: (0, j))],
        out_specs=pl.BlockSpec((tm, cw), lambda i, j: (i, j)),
        out_shape=jax.ShapeDtypeStruct((t, d), BF16),
        compiler_params=_cparams(("parallel", "parallel")),
        name="short_conv_gate",
    )(y, y, y, y, y, y, y, w_conv)


def _top16(sc, want_rank):
    n, l = sc.shape
    iota_n = lax.broadcasted_iota(jnp.int32, (n, l), 0).astype(F32)
    iota_k = lax.broadcasted_iota(jnp.int32, (PEER_TOPK, l), 0).astype(F32)

    def body(r, carry):
        sc, rank, sv, si = carry
        rf = jnp.asarray(r, jnp.int32).astype(F32)
        m = jnp.max(sc, axis=0, keepdims=True)
        idx = jnp.min(jnp.where(sc == m, iota_n, float(n)), axis=0, keepdims=True)
        sel = iota_n == idx
        if want_rank:
            rank = jnp.where(sel, rf, rank)
        sc = jnp.where(sel, -jnp.inf, sc)
        sv = jnp.where(iota_k == rf, m, sv)
        si = jnp.where(iota_k == rf, idx, si)
        return sc, rank, sv, si

    init = (sc, jnp.full((n, l), float(PEER_TOPK), F32),
            jnp.zeros((PEER_TOPK, l), F32), jnp.zeros((PEER_TOPK, l), F32))
    _, rank, sv, si = lax.fori_loop(0, PEER_TOPK, body, init)
    return sv, si, rank


def _merge_top16(sv1, sv2):
    k = sv1.shape[0]
    iota_k = lax.broadcasted_iota(jnp.int32, sv1.shape, 0).astype(F32)
    top = sv1[0:1] + sv2[0:1]

    def body(_, carry):
        front, cnt, z = carry
        m = jnp.max(front, axis=0, keepdims=True)
        a_star = jnp.min(jnp.where(front == m, iota_k, float(k)), axis=0, keepdims=True)
        sel = iota_k == a_star
        z = z + jnp.exp(m - top)
        cnt = jnp.where(sel, cnt + 1.0, cnt)
        c_sel = jnp.sum(jnp.where(sel, cnt, 0.0), axis=0, keepdims=True)
        nxt = jnp.sum(jnp.where(iota_k == c_sel, sv2, 0.0), axis=0, keepdims=True)
        nxt = jnp.where(c_sel >= float(k), -jnp.inf, nxt)
        front = jnp.where(sel, sv1 + nxt, front)
        return front, cnt, z

    init = (sv1 + sv2[0:1], jnp.zeros(sv1.shape, F32), jnp.zeros(top.shape, F32))
    _, cnt, z = lax.fori_loop(0, PEER_TOPK, body, init)
    return cnt, z


def _tree(op, xs):
    while len(xs) > 1:
        xs = [op(xs[k], xs[k + 1]) for k in range(0, len(xs) - 1, 2)] + (xs[-1:] if len(xs) % 2 else [])
    return xs[0]


def _all_sublanes(op, x):
    for shift in (4, 2, 1):
        x = op(x, pltpu.roll(x, shift, 0))
    return x


def _count_greater(top, x):
    sel = jnp.where
    c8 = top[7] > x
    c4 = sel(c8, top[11], top[3]) > x
    c2 = sel(c8, sel(c4, top[13], top[9]), sel(c4, top[5], top[1])) > x
    c1 = sel(c8, sel(c4, sel(c2, top[14], top[12]), sel(c2, top[10], top[8])),
             sel(c4, sel(c2, top[6], top[4]), sel(c2, top[2], top[0]))) > x
    c16 = top[15] > x
    zero = jnp.zeros_like(x)
    return ((sel(c8, 8.0, zero) + sel(c4, 4.0, zero)) + (sel(c2, 2.0, zero) + sel(c1, 1.0, zero))
            + sel(c16, 1.0, zero))


def _sorting_network(n):
    def merge(lo, hi, r):
        step = r * 2
        if step < hi - lo:
            yield from merge(lo, hi, step)
            yield from merge(lo + r, hi, step)
            yield from [(k, k + r) for k in range(lo + r, hi - r, step)]
        else:
            yield (lo, lo + r)

    def sort(lo, hi):
        if hi - lo >= 1:
            mid = lo + (hi - lo) // 2
            yield from sort(lo, mid)
            yield from sort(mid + 1, hi)
            yield from merge(lo, hi, 1)

    return tuple(sort(0, n - 1))


_SORT16 = _sorting_network(PEER_TOPK)


def _compare_exchange(xs, i, j):
    xs[i], xs[j] = jnp.maximum(xs[i], xs[j]), jnp.minimum(xs[i], xs[j])


def _top16_sorted(rows):
    xs = list(rows)
    for i, j in _SORT16:
        _compare_exchange(xs, i, j)
    k = len(xs)
    for shift in (4, 2, 1):
        ys = [pltpu.roll(x, shift, 0) for x in xs]
        xs = [jnp.maximum(xs[v], ys[k - 1 - v]) for v in range(k)]
        stride = k // 2
        while stride:
            for v in range(k):
                if not v & stride:
                    _compare_exchange(xs, v, v + stride)
            stride //= 2
    return xs


def _peer_scores(h, keys_ref, q_scr, s_scr):
    for p in range(2):
        r0 = pl.multiple_of((2 * h + p) * PEER_HALF_DIM, PEER_HALF_DIM)
        s_scr[h, p] = jnp.dot(keys_ref[2 * h + p], q_scr[pl.ds(r0, PEER_HALF_DIM), :],
                              preferred_element_type=F32)


def _peer_head_exact(h, lt, s_scr, r2_ref, e2_ref, n_ref, e1_ref):
    n_keys, lanes = PEER_N_KEYS, 128
    iota_n = lax.broadcasted_iota(jnp.int32, (n_keys, lanes), 0).astype(F32)
    ls = slice(lt * lanes, (lt + 1) * lanes)
    s1 = s_scr[h, 0, :, ls]
    s2 = s_scr[h, 1, :, ls]
    sv1, si1, _ = _top16(s1, want_rank=False)
    sv2, _, rank2 = _top16(s2, want_rank=True)
    cnt, z = _merge_top16(sv1, sv2)
    n_sel = jnp.zeros((n_keys, lanes), F32)
    for a in range(PEER_TOPK):
        n_sel = jnp.where(iota_n == si1[a:a + 1, :], cnt[a:a + 1, :], n_sel)
    r2_ref[h, :, ls] = rank2.astype(r2_ref.dtype)
    e2_ref[h, :, ls] = jnp.exp(s2 - sv2[0:1, :]).astype(e2_ref.dtype)
    n_ref[h, :, ls] = n_sel
    e1_ref[h, :, ls] = jnp.exp(s1 - sv1[0:1, :]) * (GELU_HALF / z)


def _peer_head_top_values(h, s_scr, t_scr, sv_scr):
    n_keys, lanes, sub = PEER_N_KEYS, 128, 8

    head_row = lax.broadcasted_iota(jnp.int32, (sub, lanes), 0) == h

    def lane_tile(lt, _):
        ls = pl.ds(pl.multiple_of(lt * lanes, lanes), lanes)
        for p in range(2):
            rows = [s_scr[h, p, v * sub:(v + 1) * sub, ls] for v in range(n_keys // sub)]
            top = _top16_sorted(rows)
            for r in range(PEER_TOPK):
                t_scr[h, p, r, :, ls] = top[r]
                sv_scr[p, r, :, ls] = jnp.where(head_row, top[r], sv_scr[p, r, :, ls])
        return 0

    lax.fori_loop(0, s_scr.shape[3] // lanes, lane_tile, 0)


def _peer_head_distinct(h, s_scr, t_scr, cnt_scr, z_scr, r2_ref, e2_ref, n_ref, e1_ref):
    n_keys, lanes, sub = PEER_N_KEYS, 128, 8
    n_vr = n_keys // sub
    one = jnp.ones((sub, lanes), F32)
    zero = jnp.zeros((sub, lanes), F32)
    bad_tiles = []
    cnt_rows = [cnt_scr[a, pl.ds(h, 1), :] for a in range(PEER_TOPK)]
    inv_row = GELU_HALF / z_scr[pl.ds(h, 1), :]
    for lt in range(s_scr.shape[3] // lanes):
        ls = slice(lt * lanes, (lt + 1) * lanes)
        bad = zero
        for p in (1, 0):
            top = [t_scr[h, p, b, :, ls] for b in range(PEER_TOPK)]
            tied = _tree(jnp.maximum, [jnp.where(top[b] > top[b + 1], zero, one)
                                       for b in range(PEER_TOPK - 1)])
            n_top = zero
            if p == 1:
                for v in range(n_vr):
                    vs = slice(v * sub, (v + 1) * sub)
                    x = s_scr[h, 1, vs, ls]
                    rank = _count_greater(top, x)
                    r2_ref[h, vs, ls] = rank.astype(r2_ref.dtype)
                    e2_ref[h, vs, ls] = jnp.exp(x - top[0]).astype(e2_ref.dtype)
                    n_top = n_top + jnp.where(rank < float(PEER_TOPK), one, zero)
            else:
                cnt_b = [jnp.broadcast_to(r[:, ls], (sub, lanes)) for r in cnt_rows]
                inv_b = jnp.broadcast_to(inv_row[:, ls], (sub, lanes))
                for v in range(n_vr):
                    vs = slice(v * sub, (v + 1) * sub)
                    x = s_scr[h, 0, vs, ls]
                    n_sel = zero
                    for a in range(PEER_TOPK):
                        n_sel = jnp.where(x == top[a], cnt_b[a], n_sel)
                    n_ref[h, vs, ls] = n_sel
                    e1_ref[h, vs, ls] = jnp.exp(x - top[0]) * inv_b
                    n_top = n_top + jnp.where(x >= top[PEER_TOPK - 1], one, zero)
            miscount = jnp.where(_all_sublanes(jnp.add, n_top) != float(PEER_TOPK), one, zero)
            bad = jnp.maximum(bad, jnp.maximum(tied, miscount))
        bad_tiles.append(bad)
    return bad_tiles


def _peer_weights_kernel(xt_ref, wq_ref, keys_ref, r2_ref, e2_ref, n_ref, e1_ref,
                         q_scr, s_scr, t_scr, sv_scr, cnt_scr, z_scr):
    q_scr[...] = jnp.dot(wq_ref[...], xt_ref[...], preferred_element_type=F32).astype(BF16)
    sv_scr[...] = jnp.zeros_like(sv_scr)

    def top_values(h, _):
        _peer_scores(h, keys_ref, q_scr, s_scr)
        _peer_head_top_values(h, s_scr, t_scr, sv_scr)
        return 0

    lax.fori_loop(0, PEER_HEADS, top_values, 0)
    cnt, z = _merge_top16(sv_scr[0], sv_scr[1])
    cnt_scr[...] = cnt
    z_scr[...] = z[0]

    def weights(h, _):
        bad_tiles = _peer_head_distinct(h, s_scr, t_scr, cnt_scr, z_scr,
                                        r2_ref, e2_ref, n_ref, e1_ref)

        @pl.when(jnp.max(_tree(jnp.maximum, bad_tiles)) > 0.0)
        def _():
            for lt, bad in enumerate(bad_tiles):
                @pl.when(jnp.max(bad) > 0.0)
                def _():
                    _peer_head_exact(h, lt, s_scr, r2_ref, e2_ref, n_ref, e1_ref)

        return 0

    lax.fori_loop(0, PEER_HEADS, weights, 0)


def peer_weights(xt, wq_t, keys):
    d, t = xt.shape
    tm = PEER_PRE_TILE
    out16 = jax.ShapeDtypeStruct((PEER_HEADS, PEER_N_KEYS, t), BF16)
    out32 = jax.ShapeDtypeStruct((PEER_HEADS, PEER_N_KEYS, t), F32)
    ospec = pl.BlockSpec((PEER_HEADS, PEER_N_KEYS, tm), lambda i: (0, 0, i))
    return pl.pallas_call(
        _peer_weights_kernel,
        grid=(t // tm,),
        in_specs=[pl.BlockSpec((d, tm), lambda i: (0, i)),
                  pl.BlockSpec(wq_t.shape, lambda i: (0, 0)),
                  pl.BlockSpec(keys.shape, lambda i: (0, 0, 0))],
        out_specs=[ospec, ospec, ospec, ospec],
        out_shape=[out16, out16, out32, out32],
        scratch_shapes=[pltpu.VMEM((wq_t.shape[0], tm), BF16),
                        pltpu.VMEM((PEER_HEADS, 2, PEER_N_KEYS, tm), F32),
                        pltpu.VMEM((PEER_HEADS, 2, PEER_TOPK, 8, tm), F32),
                        pltpu.VMEM((2, PEER_TOPK, PEER_HEADS, tm), F32),
                        pltpu.VMEM((PEER_TOPK, PEER_HEADS, tm), F32),
                        pltpu.VMEM((PEER_HEADS, tm), F32)],
        compiler_params=_cparams(("parallel",)),
        name="peer_weights",
    )(xt, wq_t, keys)


def _peer_hidden_rows(i, il, a_ref, hid_ref, r2_ref, e2_ref, n_ref, e1_ref):
    nk, lanes, pack = PEER_N_KEYS, 128, 16
    row0 = il * nk
    n_rows = [n_ref[h, pl.ds(i, 1), :] for h in range(PEER_HEADS)]
    e1_rows = [e1_ref[h, pl.ds(i, 1), :] for h in range(PEER_HEADS)]
    zero = jnp.zeros((pack, lanes), BF16)
    for lt in range(a_ref.shape[1] // lanes):
        ls = slice(lt * lanes, (lt + 1) * lanes)
        n_b = [jnp.broadcast_to(r[:, ls], (pack, lanes)).astype(BF16) for r in n_rows]
        e1_b = [jnp.broadcast_to(r[:, ls], (pack, lanes)).astype(BF16) for r in e1_rows]
        for jb in range(nk // pack):
            js = slice(jb * pack, (jb + 1) * pack)
            g = None
            for h in range(PEER_HEADS):
                term = jnp.where(r2_ref[h, js, ls] < n_b[h], e2_ref[h, js, ls] * e1_b[h], zero)
                g = term if g is None else g + term
            rs = slice(row0 + jb * pack, row0 + (jb + 1) * pack)
            a = a_ref[rs, ls]
            act = a * (1.0 + lax.erf(a * math.sqrt(0.5)))
            hid_ref[rs, ls] = act.astype(BF16) * g


def _peer_dense_kernel(xt_ref, r2_ref, e2_ref, n_ref, e1_ref, wup_ref, wdn_ref, h_ref, g_ref, o_ref,
                       acc_ref, a0_ref, a1_ref, hid0_ref, hid1_ref, r2_scr, e2_scr, *, out_norm):
    e = pl.program_id(1)
    grp = a0_ref.shape[0]
    rows_per_grp = grp // PEER_N_KEYS
    n_grp = wup_ref.shape[0] // grp

    @pl.when(e == 0)
    def _():
        acc_ref[...] = jnp.zeros_like(acc_ref)
        r2_scr[...] = r2_ref[...]
        e2_scr[...] = e2_ref[...]

    a_refs = (a0_ref, a1_ref)
    hid_refs = (hid0_ref, hid1_ref)
    for p in range(n_grp + 2):
        if p < n_grp:
            a_refs[p % 2][...] = jnp.dot(wup_ref[p * grp:(p + 1) * grp, :], xt_ref[...],
                                         preferred_element_type=F32)
        if 1 <= p <= n_grp:
            g = p - 1
            for il in range(rows_per_grp):
                _peer_hidden_rows((e * n_grp + g) * rows_per_grp + il, il, a_refs[g % 2],
                                  hid_refs[g % 2], r2_scr, e2_scr, n_ref, e1_ref)
        if p >= 2:
            g = p - 2
            acc_ref[...] += lax.dot_general(hid_refs[g % 2][...], wdn_ref[g * grp:(g + 1) * grp, :],
                                            (((0,), (0,)), ((), ())), preferred_element_type=F32)

    @pl.when(e == pl.num_programs(1) - 1)
    def _():
        h_new = h_ref[...] + acc_ref[...]
        o_ref[...] = _rms(h_new, g_ref[...]) if out_norm else h_new


def peer_dense(xt, r2, e2, n_sel, e1n, w_up, w_down, layer, h, out_gain, out_norm):
    d, t = xt.shape
    n_exp = w_up.shape[1]
    tm, te = PEER_TOK_TILE, PEER_EXP_TILE
    wspec = pl.BlockSpec((PEER_HEADS, PEER_N_KEYS, tm), lambda i, e: (0, 0, i))
    return pl.pallas_call(
        functools.partial(_peer_dense_kernel, out_norm=out_norm),
        grid=(t // tm, n_exp // te),
        in_specs=[pl.BlockSpec((d, tm), lambda i, e: (0, i)),
                  wspec, wspec, wspec, wspec,
                  pl.BlockSpec((None, te, d), lambda i, e: (layer, e, 0)),
                  pl.BlockSpec((None, te, d), lambda i, e: (layer, e, 0)),
                  pl.BlockSpec((tm, d), lambda i, e: (i, 0)),
                  pl.BlockSpec((1, d), lambda i, e: (0, 0))],
        out_specs=pl.BlockSpec((tm, d), lambda i, e: (i, 0)),
        out_shape=jax.ShapeDtypeStruct((t, d), F32),
        scratch_shapes=[pltpu.VMEM((tm, d), F32),
                        pltpu.VMEM((PEER_STAGE_ROWS, tm), F32),
                        pltpu.VMEM((PEER_STAGE_ROWS, tm), F32),
                        pltpu.VMEM((PEER_STAGE_ROWS, tm), BF16),
                        pltpu.VMEM((PEER_STAGE_ROWS, tm), BF16),
                        pltpu.VMEM((PEER_HEADS, PEER_N_KEYS, tm), BF16),
                        pltpu.VMEM((PEER_HEADS, PEER_N_KEYS, tm), BF16)],
        compiler_params=_cparams(("parallel", "arbitrary")),
        name="peer_dense",
    )(xt, r2, e2, n_sel, e1n, w_up, w_down, h, out_gain.reshape(1, d))


def kernel(x, norm_mix, norm_ffn, norm_final, attn_w_qkv, attn_w_o, attn_rel_bias,
           conv_w_in, conv_w_conv, conv_w_out, peer_w_query, peer_sub_keys,
           peer_w_up, peer_w_down):
    batch, seq, d = x.shape
    depth = norm_mix.shape[0]
    rows = seq // GRID_W
    h = x.reshape(batch * seq, d)
    w_up_all = peer_w_up.astype(BF16)
    w_down_all = peer_w_down.astype(BF16)
    for i in range(depth):
        j = i // 2
        if i % 2 == 0:
            qkv = norm_matmul(h, norm_mix[i], attn_w_qkv[j].astype(BF16), BF16)
            bias_tabs = _attention_bias_tables(attn_rel_bias[j], rows)
            mixed = neighbourhood_attention_core(qkv, bias_tabs, batch, seq)
            w_out = attn_w_o[j]
        else:
            y = norm_matmul(h, norm_mix[i], conv_w_in[j].astype(BF16), BF16)
            mixed = short_conv_gate(y, conv_w_conv[j], seq)
            w_out = conv_w_out[j]
        h, xt = matmul_residual_norm(mixed, w_out.astype(BF16), h, norm_ffn[i])
        keys = peer_sub_keys[i].reshape(PEER_HEADS * 2, PEER_N_KEYS, PEER_HALF_DIM).astype(BF16)
        r2, e2, n_sel, e1n = peer_weights(xt, peer_w_query[i].T.astype(BF16), keys)
        h = peer_dense(xt, r2, e2, n_sel, e1n, w_up_all, w_down_all, i, h, norm_final,
                       out_norm=(i == depth - 1))
    return h.reshape(batch, seq, d)
```

```python
import functools
import math

import numpy as np
import jax
import jax.numpy as jnp
from jax import lax
from jax.experimental import pallas as pl
from jax.experimental.pallas import tpu as pltpu

F32 = jnp.float32
BF16 = jnp.bfloat16

RMS_EPS = 1e-6
NEG_INF = -1e30

GRID_W = 64
N_HEADS = 16
HEAD_DIM = 64
MAX_WIN_ROWS = 8
WIN_COLS = 16
ATT_Q_ROWS = 4
ATT_BAND_ROWS = ATT_Q_ROWS + MAX_WIN_ROWS - 1
ATT_LANES = 128
ATT_BLOCKS_PER_STEP = 8

PEER_HEADS = 8
PEER_N_KEYS = 128
PEER_TOPK = 16
PEER_HALF_DIM = 128
GELU_HALF = 0.5

TOK_TILE = 1024
PEER_PRE_TILE = 512
PEER_TOK_TILE = 256
PEER_EXP_TILE = 4096
PEER_STAGE_ROWS = 256
VMEM_LIMIT = 56 * 1024 * 1024


def _cparams(sem):
    return pltpu.CompilerParams(dimension_semantics=sem, vmem_limit_bytes=VMEM_LIMIT)


def _rms(x, g):
    return x * lax.rsqrt(jnp.mean(x * x, axis=-1, keepdims=True) + RMS_EPS) * g


def _norm_matmul_kernel(x_ref, g_ref, w_ref, o_ref):
    y = _rms(x_ref[...], g_ref[...])
    o_ref[...] = jnp.dot(y.astype(BF16), w_ref[...],
                         preferred_element_type=F32).astype(o_ref.dtype)


def norm_matmul(x, gain, w, out_dtype):
    t, d = x.shape
    n = w.shape[1]
    return pl.pallas_call(
        _norm_matmul_kernel,
        grid=(t // TOK_TILE,),
        in_specs=[pl.BlockSpec((TOK_TILE, d), lambda i: (i, 0)),
                  pl.BlockSpec((1, d), lambda i: (0, 0)),
                  pl.BlockSpec((d, n), lambda i: (0, 0))],
        out_specs=pl.BlockSpec((TOK_TILE, n), lambda i: (i, 0)),
        out_shape=jax.ShapeDtypeStruct((t, n), out_dtype),
        compiler_params=_cparams(("parallel",)),
        name="norm_matmul",
    )(x, gain.reshape(1, d), w)


def _matmul_residual_norm_kernel(a_ref, w_ref, h_ref, g_ref, o_ref, xt_ref):
    h_new = h_ref[...] + jnp.dot(a_ref[...], w_ref[...], preferred_element_type=F32)
    o_ref[...] = h_new
    xt_ref[...] = _rms(h_new, g_ref[...]).T.astype(xt_ref.dtype)


def matmul_residual_norm(a, w, h, gain):
    t, k = a.shape
    d = w.shape[1]
    return pl.pallas_call(
        _matmul_residual_norm_kernel,
        grid=(t // TOK_TILE,),
        in_specs=[pl.BlockSpec((TOK_TILE, k), lambda i: (i, 0)),
                  pl.BlockSpec((k, d), lambda i: (0, 0)),
                  pl.BlockSpec((TOK_TILE, d), lambda i: (i, 0)),
                  pl.BlockSpec((1, d), lambda i: (0, 0))],
        out_specs=[pl.BlockSpec((TOK_TILE, d), lambda i: (i, 0)),
                   pl.BlockSpec((d, TOK_TILE), lambda i: (0, i))],
        out_shape=[jax.ShapeDtypeStruct((t, d), F32), jax.ShapeDtypeStruct((d, t), BF16)],
        compiler_params=_cparams(("parallel",)),
        name="matmul_residual_norm",
    )(a, w, h, gain.reshape(1, d))


def _attention_bias_tables(rel_bias, rows):
    n_h = rel_bias.shape[0]
    n_dr = 2 * MAX_WIN_ROWS - 1
    w = GRID_W
    edge = w - WIN_COLS
    line = jnp.pad(rel_bias, ((0, 0), (0, 0), (edge, edge + 1)), mode="edge")
    flat = jnp.tile(line, (1, 1, w))[:, :, w - 1:w - 1 + w * (2 * w - 1)]
    toep = flat.reshape(n_h, n_dr, w, 2 * w - 1)[:, :, :, :w]
    qc = np.arange(w)[:, None]
    kc = np.arange(w)[None, :]
    ws = np.clip(qc - WIN_COLS // 2, 0, w - WIN_COLS)
    toep = jnp.where(((kc >= ws) & (kc < ws + WIN_COLS))[None, None], toep, NEG_INF).astype(BF16)
    pad = ATT_Q_ROWS - 1
    toep = jnp.pad(toep, ((0, 0), (pad, pad), (0, 0), (0, 0)), constant_values=NEG_INF)
    tabs = []
    for r0 in (0, ATT_Q_ROWS, rows - ATT_Q_ROWS):
        band = int(np.clip(r0 - MAX_WIN_ROWS // 2, 0, rows - ATT_BAND_ROWS))
        blocks = []
        for ql in range(ATT_Q_ROWS):
            lo = band - (r0 + ql) + MAX_WIN_ROWS - 1 + pad
            blk = toep[:, lo:lo + ATT_BAND_ROWS]
            blocks.append(blk.transpose(0, 2, 1, 3).reshape(n_h, w, ATT_BAND_ROWS * w))
        tab = jnp.stack(blocks, axis=1).reshape(n_h, ATT_Q_ROWS * w, ATT_BAND_ROWS * w)
        qr = r0 + np.arange(ATT_Q_ROWS)[:, None]
        kr = band + np.arange(ATT_BAND_ROWS)[None, :]
        rs = np.clip(qr - MAX_WIN_ROWS // 2, 0, rows - MAX_WIN_ROWS)
        row_ok = (kr >= rs) & (kr < rs + MAX_WIN_ROWS)
        row_ok = np.repeat(np.repeat(row_ok, w, axis=0), w, axis=1)
        tabs.append(jnp.where(row_ok[None], tab, NEG_INF))
    return jnp.stack(tabs)


def _attention_kernel(q_ref, k_ref, v_ref, *rest, rows):
    bias_refs, o_ref = rest[:ATT_BLOCKS_PER_STEP], rest[ATT_BLOCKS_PER_STEP]
    q_tok = ATT_Q_ROWS * GRID_W
    n_keys = ATT_BAND_ROWS * GRID_W
    scale = 1.0 / math.sqrt(HEAD_DIM)
    lane = lax.broadcasted_iota(jnp.int32, (q_tok, ATT_LANES), 1)
    for sub in range(ATT_BLOCKS_PER_STEP):
        rb = pl.program_id(2) * ATT_BLOCKS_PER_STEP + sub
        band = jnp.clip(rb * ATT_Q_ROWS - MAX_WIN_ROWS // 2, 0, rows - ATT_BAND_ROWS)
        start = pl.multiple_of(band * GRID_W, GRID_W)
        kb = k_ref[pl.ds(start, n_keys), :]
        vb = v_ref[pl.ds(start, n_keys), :]
        q = q_ref[sub * q_tok:(sub + 1) * q_tok, :]
        zero = jnp.zeros_like(q)
        qs = jnp.concatenate([jnp.where(lane < HEAD_DIM, q, zero),
                              jnp.where(lane >= HEAD_DIM, q, zero)], axis=0) * jnp.asarray(scale, q.dtype)
        s = lax.dot_general(qs, kb, (((1,), (1,)), ((), ())), preferred_element_type=F32)
        s = s + bias_refs[sub][0].reshape(2 * q_tok, n_keys).astype(F32)
        m = jnp.max(s, axis=-1, keepdims=True)
        p = jnp.exp(s - m)
        l = jnp.sum(p, axis=-1, keepdims=True)
        o = jnp.dot(p.astype(BF16), vb, preferred_element_type=F32) / l
        o_ref[sub * q_tok:(sub + 1) * q_tok, :] = jnp.where(lane < HEAD_DIM, o[:q_tok],
                                                             o[q_tok:]).astype(o_ref.dtype)


def neighbourhood_attention_core(qkv, bias_tabs, batch, seq):
    t = qkv.shape[0]
    d = qkv.shape[1] // 3
    rows = seq // GRID_W
    n_rb = rows // ATT_Q_ROWS
    n_steps = n_rb // ATT_BLOCKS_PER_STEP
    n_hp = d // ATT_LANES
    q_tok = ATT_Q_ROWS * GRID_W * ATT_BLOCKS_PER_STEP

    def bias_spec(sub):
        def index_map(b, hp, st):
            rb = st * ATT_BLOCKS_PER_STEP + sub
            return (jnp.where(rb == 0, 0, jnp.where(rb == n_rb - 1, 2, 1)), hp, 0, 0)
        return pl.BlockSpec((1, 2, ATT_Q_ROWS * GRID_W, ATT_BAND_ROWS * GRID_W), index_map)

    return pl.pallas_call(
        functools.partial(_attention_kernel, rows=rows),
        grid=(batch, n_hp, n_steps),
        in_specs=[pl.BlockSpec((q_tok, ATT_LANES), lambda b, hp, st: (b * n_steps + st, hp)),
                  pl.BlockSpec((seq, ATT_LANES), lambda b, hp, st: (b, n_hp + hp)),
                  pl.BlockSpec((seq, ATT_LANES), lambda b, hp, st: (b, 2 * n_hp + hp))]
                 + [bias_spec(sub) for sub in range(ATT_BLOCKS_PER_STEP)],
        out_specs=pl.BlockSpec((q_tok, ATT_LANES), lambda b, hp, st: (b * n_steps + st, hp)),
        out_shape=jax.ShapeDtypeStruct((t, d), BF16),
        compiler_params=_cparams(("parallel", "parallel", "arbitrary")),
        name="natten_core",
    )(qkv, qkv, qkv, *([bias_tabs] * ATT_BLOCKS_PER_STEP))


def _short_conv_kernel(b_ref, c_ref, v_ref, cp_ref, vp_ref, cn_ref, vn_ref, w_ref, o_ref, *, seq):
    i = pl.program_id(0)
    tm = b_ref.shape[0]
    halo = cp_ref.shape[0]
    u = c_ref[...].astype(F32) * v_ref[...].astype(F32)
    u_before = cp_ref[halo - 1:halo, :].astype(F32) * vp_ref[halo - 1:halo, :].astype(F32)
    u_after = cn_ref[0:1, :].astype(F32) * vn_ref[0:1, :].astype(F32)
    row = lax.broadcasted_iota(jnp.int32, u.shape, 0)
    pos = (i * tm + row) % seq
    u_prev = jnp.where(row == 0, u_before, pltpu.roll(u, 1, 0))
    u_prev = jnp.where(pos == 0, 0.0, u_prev)
    u_next = jnp.where(row == tm - 1, u_after, pltpu.roll(u, tm - 1, 0))
    u_next = jnp.where(pos == seq - 1, 0.0, u_next)
    conv = u_prev * w_ref[0:1, :] + u * w_ref[1:2, :] + u_next * w_ref[2:3, :]
    o_ref[...] = (b_ref[...].astype(F32) * conv).astype(o_ref.dtype)


def short_conv_gate(y, w_conv, seq):
    t = y.shape[0]
    d = y.shape[1] // 3
    cw = 512
    nj = d // cw
    tm = TOK_TILE
    halo = 16
    n_halo = t // halo

    def prev_map(off):
        return lambda i, j: (jnp.maximum(i * (tm // halo) - 1, 0), off * nj + j)

    def next_map(off):
        return lambda i, j: (jnp.minimum((i + 1) * (tm // halo), n_halo - 1), off * nj + j)

    return pl.pallas_call(
        functools.partial(_short_conv_kernel, seq=seq),
        grid=(t // tm, nj),
        in_specs=[pl.BlockSpec((tm, cw), lambda i, j: (i, j)),
                  pl.BlockSpec((tm, cw), lambda i, j: (i, nj + j)),
                  pl.BlockSpec((tm, cw), lambda i, j: (i, 2 * nj + j)),
                  pl.BlockSpec((halo, cw), prev_map(1)),
                  pl.BlockSpec((halo, cw), prev_map(2)),
                  pl.BlockSpec((halo, cw), next_map(1)),
                  pl.BlockSpec((halo, cw), next_map(2)),
                  pl.BlockSpec((3, cw), lambda i, j: (0, j))],
        out_specs=pl.BlockSpec((tm, cw), lambda i, j: (i, j)),
        out_shape=jax.ShapeDtypeStruct((t, d), BF16),
        compiler_params=_cparams(("parallel", "parallel")),
        name="short_conv_gate",
    )(y, y, y, y, y, y, y, w_conv)


def _top16(sc, want_rank):
    n, l = sc.shape
    iota_n = lax.broadcasted_iota(jnp.int32, (n, l), 0).astype(F32)
    iota_k = lax.broadcasted_iota(jnp.int32, (PEER_TOPK, l), 0).astype(F32)

    def body(r, carry):
        sc, rank, sv, si = carry
        rf = jnp.asarray(r, jnp.int32).astype(F32)
        m = jnp.max(sc, axis=0, keepdims=True)
        idx = jnp.min(jnp.where(sc == m, iota_n, float(n)), axis=0, keepdims=True)
        sel = iota_n == idx
        if want_rank:
            rank = jnp.where(sel, rf, rank)
        sc = jnp.where(sel, -jnp.inf, sc)
        sv = jnp.where(iota_k == rf, m, sv)
        si = jnp.where(iota_k == rf, idx, si)
        return sc, rank, sv, si

    init = (sc, jnp.full((n, l), float(PEER_TOPK), F32),
            jnp.zeros((PEER_TOPK, l), F32), jnp.zeros((PEER_TOPK, l), F32))
    _, rank, sv, si = lax.fori_loop(0, PEER_TOPK, body, init)
    return sv, si, rank


def _merge_top16(sv1, sv2):
    k = sv1.shape[0]
    iota_k = lax.broadcasted_iota(jnp.int32, sv1.shape, 0).astype(F32)
    top = sv1[0:1] + sv2[0:1]

    def body(_, carry):
        front, cnt, z = carry
        m = jnp.max(front, axis=0, keepdims=True)
        a_star = jnp.min(jnp.where(front == m, iota_k, float(k)), axis=0, keepdims=True)
        sel = iota_k == a_star
        z = z + jnp.exp(m - top)
        cnt = jnp.where(sel, cnt + 1.0, cnt)
        c_sel = jnp.sum(jnp.where(sel, cnt, 0.0), axis=0, keepdims=True)
        nxt = jnp.sum(jnp.where(iota_k == c_sel, sv2, 0.0), axis=0, keepdims=True)
        nxt = jnp.where(c_sel >= float(k), -jnp.inf, nxt)
        front = jnp.where(sel, sv1 + nxt, front)
        return front, cnt, z

    init = (sv1 + sv2[0:1], jnp.zeros(sv1.shape, F32), jnp.zeros(top.shape, F32))
    _, cnt, z = lax.fori_loop(0, PEER_TOPK, body, init)
    return cnt, z


def _tree(op, xs):
    while len(xs) > 1:
        xs = [op(xs[k], xs[k + 1]) for k in range(0, len(xs) - 1, 2)] + (xs[-1:] if len(xs) % 2 else [])
    return xs[0]


def _all_sublanes(op, x):
    for shift in (4, 2, 1):
        x = op(x, pltpu.roll(x, shift, 0))
    return x


def _count_greater(top, x):
    sel = jnp.where
    c8 = top[7] > x
    c4 = sel(c8, top[11], top[3]) > x
    c2 = sel(c8, sel(c4, top[13], top[9]), sel(c4, top[5], top[1])) > x
    c1 = sel(c8, sel(c4, sel(c2, top[14], top[12]), sel(c2, top[10], top[8])),
             sel(c4, sel(c2, top[6], top[4]), sel(c2, top[2], top[0]))) > x
    c16 = top[15] > x
    zero = jnp.zeros_like(x)
    return ((sel(c8, 8.0, zero) + sel(c4, 4.0, zero)) + (sel(c2, 2.0, zero) + sel(c1, 1.0, zero))
            + sel(c16, 1.0, zero))


def _sorting_network(n):
    def merge(lo, hi, r):
        step = r * 2
        if step < hi - lo:
            yield from merge(lo, hi, step)
            yield from merge(lo + r, hi, step)
            yield from [(k, k + r) for k in range(lo + r, hi - r, step)]
        else:
            yield (lo, lo + r)

    def sort(lo, hi):
        if hi - lo >= 1:
            mid = lo + (hi - lo) // 2
            yield from sort(lo, mid)
            yield from sort(mid + 1, hi)
            yield from merge(lo, hi, 1)

    return tuple(sort(0, n - 1))


_SORT16 = _sorting_network(PEER_TOPK)


def _compare_exchange(xs, i, j):
    xs[i], xs[j] = jnp.maximum(xs[i], xs[j]), jnp.minimum(xs[i], xs[j])


def _top16_sorted(rows):
    xs = list(rows)
    for i, j in _SORT16:
        _compare_exchange(xs, i, j)
    k = len(xs)
    for shift in (4, 2, 1):
        ys = [pltpu.roll(x, shift, 0) for x in xs]
        xs = [jnp.maximum(xs[v], ys[k - 1 - v]) for v in range(k)]
        stride = k // 2
        while stride:
            for v in range(k):
                if not v & stride:
                    _compare_exchange(xs, v, v + stride)
            stride //= 2
    return xs


def _peer_scores(h, keys_ref, q_scr, s_scr):
    for p in range(2):
        r0 = pl.multiple_of((2 * h + p) * PEER_HALF_DIM, PEER_HALF_DIM)
        s_scr[h, p] = jnp.dot(keys_ref[2 * h + p], q_scr[pl.ds(r0, PEER_HALF_DIM), :],
                              preferred_element_type=F32)


def _peer_head_exact(h, lt, s_scr, r2_ref, e2_ref, n_ref, e1_ref):
    n_keys, lanes = PEER_N_KEYS, 128
    iota_n = lax.broadcasted_iota(jnp.int32, (n_keys, lanes), 0).astype(F32)
    ls = slice(lt * lanes, (lt + 1) * lanes)
    s1 = s_scr[h, 0, :, ls]
    s2 = s_scr[h, 1, :, ls]
    sv1, si1, _ = _top16(s1, want_rank=False)
    sv2, _, rank2 = _top16(s2, want_rank=True)
    cnt, z = _merge_top16(sv1, sv2)
    n_sel = jnp.zeros((n_keys, lanes), F32)
    for a in range(PEER_TOPK):
        n_sel = jnp.where(iota_n == si1[a:a + 1, :], cnt[a:a + 1, :], n_sel)
    r2_ref[h, :, ls] = rank2.astype(r2_ref.dtype)
    e2_ref[h, :, ls] = jnp.exp(s2 - sv2[0:1, :]).astype(e2_ref.dtype)
    n_ref[h, :, ls] = n_sel
    e1_ref[h, :, ls] = jnp.exp(s1 - sv1[0:1, :]) * (GELU_HALF / z)


def _peer_head_top_values(h, s_scr, t_scr, sv_scr):
    n_keys, lanes, sub = PEER_N_KEYS, 128, 8

    head_row = lax.broadcasted_iota(jnp.int32, (sub, lanes), 0) == h

    def lane_tile(lt, _):
        ls = pl.ds(pl.multiple_of(lt * lanes, lanes), lanes)
        for p in range(2):
            rows = [s_scr[h, p, v * sub:(v + 1) * sub, ls] for v in range(n_keys // sub)]
            top = _top16_sorted(rows)
            for r in range(PEER_TOPK):
                t_scr[h, p, r, :, ls] = top[r]
                sv_scr[p, r, :, ls] = jnp.where(head_row, top[r], sv_scr[p, r, :, ls])
        return 0

    lax.fori_loop(0, s_scr.shape[3] // lanes, lane_tile, 0)


def _peer_head_distinct(h, s_scr, t_scr, cnt_scr, z_scr, r2_ref, e2_ref, n_ref, e1_ref):
    n_keys, lanes, sub = PEER_N_KEYS, 128, 8
    n_vr = n_keys // sub
    one = jnp.ones((sub, lanes), F32)
    zero = jnp.zeros((sub, lanes), F32)
    bad_tiles = []
    cnt_rows = [cnt_scr[a, pl.ds(h, 1), :] for a in range(PEER_TOPK)]
    inv_row = GELU_HALF / z_scr[pl.ds(h, 1), :]
    for lt in range(s_scr.shape[3] // lanes):
        ls = slice(lt * lanes, (lt + 1) * lanes)
        bad = zero
        for p in (1, 0):
            top = [t_scr[h, p, b, :, ls] for b in range(PEER_TOPK)]
            tied = _tree(jnp.maximum, [jnp.where(top[b] > top[b + 1], zero, one)
                                       for b in range(PEER_TOPK - 1)])
            n_top = zero
            if p == 1:
                for v in range(n_vr):
                    vs = slice(v * sub, (v + 1) * sub)
                    x = s_scr[h, 1, vs, ls]
                    rank = _count_greater(top, x)
                    r2_ref[h, vs, ls] = rank.astype(r2_ref.dtype)
                    e2_ref[h, vs, ls] = jnp.exp(x - top[0]).astype(e2_ref.dtype)
                    n_top = n_top + jnp.where(rank < float(PEER_TOPK), one, zero)
            else:
                cnt_b = [jnp.broadcast_to(r[:, ls], (sub, lanes)) for r in cnt_rows]
                inv_b = jnp.broadcast_to(inv_row[:, ls], (sub, lanes))
                for v in range(n_vr):
                    vs = slice(v * sub, (v + 1) * sub)
                    x = s_scr[h, 0, vs, ls]
                    n_sel = zero
                    for a in range(PEER_TOPK):
                        n_sel = jnp.where(x == top[a], cnt_b[a], n_sel)
                    n_ref[h, vs, ls] = n_sel
                    e1_ref[h, vs, ls] = jnp.exp(x - top[0]) * inv_b
                    n_top = n_top + jnp.where(x >= top[PEER_TOPK - 1], one, zero)
            miscount = jnp.where(_all_sublanes(jnp.add, n_top) != float(PEER_TOPK), one, zero)
            bad = jnp.maximum(bad, jnp.maximum(tied, miscount))
        bad_tiles.append(bad)
    return bad_tiles


def _peer_weights_kernel(xt_ref, wq_ref, keys_ref, r2_ref, e2_ref, n_ref, e1_ref,
                         q_scr, s_scr, t_scr, sv_scr, cnt_scr, z_scr):
    q_scr[...] = jnp.dot(wq_ref[...], xt_ref[...], preferred_element_type=F32).astype(BF16)
    sv_scr[...] = jnp.zeros_like(sv_scr)

    def top_values(h, _):
        _peer_scores(h, keys_ref, q_scr, s_scr)
        _peer_head_top_values(h, s_scr, t_scr, sv_scr)
        return 0

    lax.fori_loop(0, PEER_HEADS, top_values, 0)
    cnt, z = _merge_top16(sv_scr[0], sv_scr[1])
    cnt_scr[...] = cnt
    z_scr[...] = z[0]

    def weights(h, _):
        bad_tiles = _peer_head_distinct(h, s_scr, t_scr, cnt_scr, z_scr,
                                        r2_ref, e2_ref, n_ref, e1_ref)

        @pl.when(jnp.max(_tree(jnp.maximum, bad_tiles)) > 0.0)
        def _():
            for lt, bad in enumerate(bad_tiles):
                @pl.when(jnp.max(bad) > 0.0)
                def _():
                    _peer_head_exact(h, lt, s_scr, r2_ref, e2_ref, n_ref, e1_ref)

        return 0

    lax.fori_loop(0, PEER_HEADS, weights, 0)


def peer_weights(xt, wq_t, keys):
    d, t = xt.shape
    tm = PEER_PRE_TILE
    out16 = jax.ShapeDtypeStruct((PEER_HEADS, PEER_N_KEYS, t), BF16)
    out32 = jax.ShapeDtypeStruct((PEER_HEADS, PEER_N_KEYS, t), F32)
    ospec = pl.BlockSpec((PEER_HEADS, PEER_N_KEYS, tm), lambda i: (0, 0, i))
    return pl.pallas_call(
        _peer_weights_kernel,
        grid=(t // tm,),
        in_specs=[pl.BlockSpec((d, tm), lambda i: (0, i)),
                  pl.BlockSpec(wq_t.shape, lambda i: (0, 0)),
                  pl.BlockSpec(keys.shape, lambda i: (0, 0, 0))],
        out_specs=[ospec, ospec, ospec, ospec],
        out_shape=[out16, out16, out32, out32],
        scratch_shapes=[pltpu.VMEM((wq_t.shape[0], tm), BF16),
                        pltpu.VMEM((PEER_HEADS, 2, PEER_N_KEYS, tm), F32),
                        pltpu.VMEM((PEER_HEADS, 2, PEER_TOPK, 8, tm), F32),
                        pltpu.VMEM((2, PEER_TOPK, PEER_HEADS, tm), F32),
                        pltpu.VMEM((PEER_TOPK, PEER_HEADS, tm), F32),
                        pltpu.VMEM((PEER_HEADS, tm), F32)],
        compiler_params=_cparams(("parallel",)),
        name="peer_weights",
    )(xt, wq_t, keys)


def _peer_hidden_rows(i, il, a_ref, hid_ref, r2_ref, e2_ref, n_ref, e1_ref):
    nk, lanes, pack = PEER_N_KEYS, 128, 16
    row0 = il * nk
    n_rows = [n_ref[h, pl.ds(i, 1), :] for h in range(PEER_HEADS)]
    e1_rows = [e1_ref[h, pl.ds(i, 1), :] for h in range(PEER_HEADS)]
    zero = jnp.zeros((pack, lanes), BF16)
    for lt in range(a_ref.shape[1] // lanes):
        ls = slice(lt * lanes, (lt + 1) * lanes)
        n_b = [jnp.broadcast_to(r[:, ls], (pack, lanes)).astype(BF16) for r in n_rows]
        e1_b = [jnp.broadcast_to(r[:, ls], (pack, lanes)).astype(BF16) for r in e1_rows]
        for jb in range(nk // pack):
            js = slice(jb * pack, (jb + 1) * pack)
            g = None
            for h in range(PEER_HEADS):
                term = jnp.where(r2_ref[h, js, ls] < n_b[h], e2_ref[h, js, ls] * e1_b[h], zero)
                g = term if g is None else g + term
            rs = slice(row0 + jb * pack, row0 + (jb + 1) * pack)
            a = a_ref[rs, ls]
            act = a * (1.0 + lax.erf(a * math.sqrt(0.5)))
            hid_ref[rs, ls] = act.astype(BF16) * g


def _peer_dense_kernel(xt_ref, r2_ref, e2_ref, n_ref, e1_ref, wup_ref, wdn_ref, h_ref, g_ref, o_ref,
                       acc_ref, a0_ref, a1_ref, hid0_ref, hid1_ref, r2_scr, e2_scr, *, out_norm):
    e = pl.program_id(1)
    grp = a0_ref.shape[0]
    rows_per_grp = grp // PEER_N_KEYS
    n_grp = wup_ref.shape[0] // grp

    @pl.when(e == 0)
    def _():
        acc_ref[...] = jnp.zeros_like(acc_ref)
        r2_scr[...] = r2_ref[...]
        e2_scr[...] = e2_ref[...]

    a_refs = (a0_ref, a1_ref)
    hid_refs = (hid0_ref, hid1_ref)
    for p in range(n_grp + 2):
        if p < n_grp:
            a_refs[p % 2][...] = jnp.dot(wup_ref[p * grp:(p + 1) * grp, :], xt_ref[...],
                                         preferred_element_type=F32)
        if 1 <= p <= n_grp:
            g = p - 1
            for il in range(rows_per_grp):
                _peer_hidden_rows((e * n_grp + g) * rows_per_grp + il, il, a_refs[g % 2],
                                  hid_refs[g % 2], r2_scr, e2_scr, n_ref, e1_ref)
        if p >= 2:
            g = p - 2
            acc_ref[...] += lax.dot_general(hid_refs[g % 2][...], wdn_ref[g * grp:(g + 1) * grp, :],
                                            (((0,), (0,)), ((), ())), preferred_element_type=F32)

    @pl.when(e == pl.num_programs(1) - 1)
    def _():
        h_new = h_ref[...] + acc_ref[...]
        o_ref[...] = _rms(h_new, g_ref[...]) if out_norm else h_new


def peer_dense(xt, r2, e2, n_sel, e1n, w_up, w_down, layer, h, out_gain, out_norm):
    d, t = xt.shape
    n_exp = w_up.shape[1]
    tm, te = PEER_TOK_TILE, PEER_EXP_TILE
    wspec = pl.BlockSpec((PEER_HEADS, PEER_N_KEYS, tm), lambda i, e: (0, 0, i))
    return pl.pallas_call(
        functools.partial(_peer_dense_kernel, out_norm=out_norm),
        grid=(t // tm, n_exp // te),
        in_specs=[pl.BlockSpec((d, tm), lambda i, e: (0, i)),
                  wspec, wspec, wspec, wspec,
                  pl.BlockSpec((None, te, d), lambda i, e: (layer, e, 0)),
                  pl.BlockSpec((None, te, d), lambda i, e: (layer, e, 0)),
                  pl.BlockSpec((tm, d), lambda i, e: (i, 0)),
                  pl.BlockSpec((1, d), lambda i, e: (0, 0))],
        out_specs=pl.BlockSpec((tm, d), lambda i, e: (i, 0)),
        out_shape=jax.ShapeDtypeStruct((t, d), F32),
        scratch_shapes=[pltpu.VMEM((tm, d), F32),
                        pltpu.VMEM((PEER_STAGE_ROWS, tm), F32),
                        pltpu.VMEM((PEER_STAGE_ROWS, tm), F32),
                        pltpu.VMEM((PEER_STAGE_ROWS, tm), BF16),
                        pltpu.VMEM((PEER_STAGE_ROWS, tm), BF16),
                        pltpu.VMEM((PEER_HEADS, PEER_N_KEYS, tm), BF16),
                        pltpu.VMEM((PEER_HEADS, PEER_N_KEYS, tm), BF16)],
        compiler_params=_cparams(("parallel", "arbitrary")),
        name="peer_dense",
    )(xt, r2, e2, n_sel, e1n, w_up, w_down, h, out_gain.reshape(1, d))


def kernel(x, norm_mix, norm_ffn, norm_final, attn_w_qkv, attn_w_o, attn_rel_bias,
           conv_w_in, conv_w_conv, conv_w_out, peer_w_query, peer_sub_keys,
           peer_w_up, peer_w_down):
    batch, seq, d = x.shape
    depth = norm_mix.shape[0]
    rows = seq // GRID_W
    h = x.reshape(batch * seq, d)
    w_up_all = peer_w_up.astype(BF16)
    w_down_all = peer_w_down.astype(BF16)
    for i in range(depth):
        j = i // 2
        if i % 2 == 0:
            qkv = norm_matmul(h, norm_mix[i], attn_w_qkv[j].astype(BF16), BF16)
            bias_tabs = _attention_bias_tables(attn_rel_bias[j], rows)
            mixed = neighbourhood_attention_core(qkv, bias_tabs, batch, seq)
            w_out = attn_w_o[j]
        else:
            y = norm_matmul(h, norm_mix[i], conv_w_in[j].astype(BF16), BF16)
            mixed = short_conv_gate(y, conv_w_conv[j], seq)
            w_out = conv_w_out[j]
        h, xt = matmul_residual_norm(mixed, w_out.astype(BF16), h, norm_ffn[i])
        keys = peer_sub_keys[i].reshape(PEER_HEADS * 2, PEER_N_KEYS, PEER_HALF_DIM).astype(BF16)
        r2, e2, n_sel, e1n = peer_weights(xt, peer_w_query[i].T.astype(BF16), keys)
        h = peer_dense(xt, r2, e2, n_sel, e1n, w_up_all, w_down_all, i, h, norm_final,
                       out_norm=(i == depth - 1))
    return h.reshape(batch, seq, d)
```

```python
import functools
import math

import numpy as np
import jax
import jax.numpy as jnp
from jax import lax
from jax.experimental import pallas as pl
from jax.experimental.pallas import tpu as pltpu

F32 = jnp.float32
BF16 = jnp.bfloat16

RMS_EPS = 1e-6
NEG_INF = -1e30

GRID_W = 64
N_HEADS = 16
HEAD_DIM = 64
MAX_WIN_ROWS = 8
WIN_COLS = 16
ATT_Q_ROWS = 4
ATT_BAND_ROWS = ATT_Q_ROWS + MAX_WIN_ROWS - 1
ATT_LANES = 128
ATT_BLOCKS_PER_STEP = 8

PEER_HEADS = 8
PEER_N_KEYS = 128
PEER_TOPK = 16
PEER_HALF_DIM = 128
GELU_HALF = 0.5

TOK_TILE = 1024
PEER_PRE_TILE = 512
PEER_TOK_TILE = 256
PEER_EXP_TILE = 4096
PEER_STAGE_ROWS = 256
VMEM_LIMIT = 56 * 1024 * 1024


def _cparams(sem):
    return pltpu.CompilerParams(dimension_semantics=sem, vmem_limit_bytes=VMEM_LIMIT)


def _rms(x, g):
    return x * lax.rsqrt(jnp.mean(x * x, axis=-1, keepdims=True) + RMS_EPS) * g


def _norm_matmul_kernel(x_ref, g_ref, w_ref, o_ref):
    y = _rms(x_ref[...], g_ref[...])
    o_ref[...] = jnp.dot(y.astype(BF16), w_ref[...],
                         preferred_element_type=F32).astype(o_ref.dtype)


def norm_matmul(x, gain, w, out_dtype):
    t, d = x.shape
    n = w.shape[1]
    return pl.pallas_call(
        _norm_matmul_kernel,
        grid=(t // TOK_TILE,),
        in_specs=[pl.BlockSpec((TOK_TILE, d), lambda i: (i, 0)),
                  pl.BlockSpec((1, d), lambda i: (0, 0)),
                  pl.BlockSpec((d, n), lambda i: (0, 0))],
        out_specs=pl.BlockSpec((TOK_TILE, n), lambda i: (i, 0)),
        out_shape=jax.ShapeDtypeStruct((t, n), out_dtype),
        compiler_params=_cparams(("parallel",)),
        name="norm_matmul",
    )(x, gain.reshape(1, d), w)


def _matmul_residual_norm_kernel(a_ref, w_ref, h_ref, g_ref, o_ref, xt_ref):
    h_new = h_ref[...] + jnp.dot(a_ref[...], w_ref[...], preferred_element_type=F32)
    o_ref[...] = h_new
    xt_ref[...] = _rms(h_new, g_ref[...]).T.astype(xt_ref.dtype)


def matmul_residual_norm(a, w, h, gain):
    t, k = a.shape
    d = w.shape[1]
    return pl.pallas_call(
        _matmul_residual_norm_kernel,
        grid=(t // TOK_TILE,),
        in_specs=[pl.BlockSpec((TOK_TILE, k), lambda i: (i, 0)),
                  pl.BlockSpec((k, d), lambda i: (0, 0)),
                  pl.BlockSpec((TOK_TILE, d), lambda i: (i, 0)),
                  pl.BlockSpec((1, d), lambda i: (0, 0))],
        out_specs=[pl.BlockSpec((TOK_TILE, d), lambda i: (i, 0)),
                   pl.BlockSpec((d, TOK_TILE), lambda i: (0, i))],
        out_shape=[jax.ShapeDtypeStruct((t, d), F32), jax.ShapeDtypeStruct((d, t), BF16)],
        compiler_params=_cparams(("parallel",)),
        name="matmul_residual_norm",
    )(a, w, h, gain.reshape(1, d))


def _attention_bias_tables(rel_bias, rows):
    n_h = rel_bias.shape[0]
    n_dr = 2 * MAX_WIN_ROWS - 1
    w = GRID_W
    edge = w - WIN_COLS
    line = jnp.pad(rel_bias, ((0, 0), (0, 0), (edge, edge + 1)), mode="edge")
    flat = jnp.tile(line, (1, 1, w))[:, :, w - 1:w - 1 + w * (2 * w - 1)]
    toep = flat.reshape(n_h, n_dr, w, 2 * w - 1)[:, :, :, :w]
    qc = np.arange(w)[:, None]
    kc = np.arange(w)[None, :]
    ws = np.clip(qc - WIN_COLS // 2, 0, w - WIN_COLS)
    toep = jnp.where(((kc >= ws) & (kc < ws + WIN_COLS))[None, None], toep, NEG_INF).astype(BF16)
    pad = ATT_Q_ROWS - 1
    toep = jnp.pad(toep, ((0, 0), (pad, pad), (0, 0), (0, 0)), constant_values=NEG_INF)
    tabs = []
    for r0 in (0, ATT_Q_ROWS, rows - ATT_Q_ROWS):
        band = int(np.clip(r0 - MAX_WIN_ROWS // 2, 0, rows - ATT_BAND_ROWS))
        blocks = []
        for ql in range(ATT_Q_ROWS):
            lo = band - (r0 + ql) + MAX_WIN_ROWS - 1 + pad
            blk = toep[:, lo:lo + ATT_BAND_ROWS]
            blocks.append(blk.transpose(0, 2, 1, 3).reshape(n_h, w, ATT_BAND_ROWS * w))
        tab = jnp.stack(blocks, axis=1).reshape(n_h, ATT_Q_ROWS * w, ATT_BAND_ROWS * w)
        qr = r0 + np.arange(ATT_Q_ROWS)[:, None]
        kr = band + np.arange(ATT_BAND_ROWS)[None, :]
        rs = np.clip(qr - MAX_WIN_ROWS // 2, 0, rows - MAX_WIN_ROWS)
        row_ok = (kr >= rs) & (kr < rs + MAX_WIN_ROWS)
        row_ok = np.repeat(np.repeat(row_ok, w, axis=0), w, axis=1)
        tabs.append(jnp.where(row_ok[None], tab, NEG_INF))
    return jnp.stack(tabs)


def _attention_kernel(q_ref, k_ref, v_ref, *rest, rows):
    bias_refs, o_ref = rest[:ATT_BLOCKS_PER_STEP], rest[ATT_BLOCKS_PER_STEP]
    q_tok = ATT_Q_ROWS * GRID_W
    n_keys = ATT_BAND_ROWS * GRID_W
    scale = 1.0 / math.sqrt(HEAD_DIM)
    lane = lax.broadcasted_iota(jnp.int32, (q_tok, ATT_LANES), 1)
    for sub in range(ATT_BLOCKS_PER_STEP):
        rb = pl.program_id(2) * ATT_BLOCKS_PER_STEP + sub
        band = jnp.clip(rb * ATT_Q_ROWS - MAX_WIN_ROWS // 2, 0, rows - ATT_BAND_ROWS)
        start = pl.multiple_of(band * GRID_W, GRID_W)
        kb = k_ref[pl.ds(start, n_keys), :]
        vb = v_ref[pl.ds(start, n_keys), :]
        q = q_ref[sub * q_tok:(sub + 1) * q_tok, :]
        zero = jnp.zeros_like(q)
        qs = jnp.concatenate([jnp.where(lane < HEAD_DIM, q, zero),
                              jnp.where(lane >= HEAD_DIM, q, zero)], axis=0) * jnp.asarray(scale, q.dtype)
        s = lax.dot_general(qs, kb, (((1,), (1,)), ((), ())), preferred_element_type=F32)
        s = s + bias_refs[sub][0].reshape(2 * q_tok, n_keys).astype(F32)
        m = jnp.max(s, axis=-1, keepdims=True)
        p = jnp.exp(s - m)
        l = jnp.sum(p, axis=-1, keepdims=True)
        o = jnp.dot(p.astype(BF16), vb, preferred_element_type=F32) / l
        o_ref[sub * q_tok:(sub + 1) * q_tok, :] = jnp.where(lane < HEAD_DIM, o[:q_tok],
                                                             o[q_tok:]).astype(o_ref.dtype)


def neighbourhood_attention_core(qkv, bias_tabs, batch, seq):
    t = qkv.shape[0]
    d = qkv.shape[1] // 3
    rows = seq // GRID_W
    n_rb = rows // ATT_Q_ROWS
    n_steps = n_rb // ATT_BLOCKS_PER_STEP
    n_hp = d // ATT_LANES
    q_tok = ATT_Q_ROWS * GRID_W * ATT_BLOCKS_PER_STEP

    def bias_spec(sub):
        def index_map(b, hp, st):
            rb = st * ATT_BLOCKS_PER_STEP + sub
            return (jnp.where(rb == 0, 0, jnp.where(rb == n_rb - 1, 2, 1)), hp, 0, 0)
        return pl.BlockSpec((1, 2, ATT_Q_ROWS * GRID_W, ATT_BAND_ROWS * GRID_W), index_map)

    return pl.pallas_call(
        functools.partial(_attention_kernel, rows=rows),
        grid=(batch, n_hp, n_steps),
        in_specs=[pl.BlockSpec((q_tok, ATT_LANES), lambda b, hp, st: (b * n_steps + st, hp)),
                  pl.BlockSpec((seq, ATT_LANES), lambda b, hp, st: (b, n_hp + hp)),
                  pl.BlockSpec((seq, ATT_LANES), lambda b, hp, st: (b, 2 * n_hp + hp))]
                 + [bias_spec(sub) for sub in range(ATT_BLOCKS_PER_STEP)],
        out_specs=pl.BlockSpec((q_tok, ATT_LANES), lambda b, hp, st: (b * n_steps + st, hp)),
        out_shape=jax.ShapeDtypeStruct((t, d), BF16),
        compiler_params=_cparams(("parallel", "parallel", "arbitrary")),
        name="natten_core",
    )(qkv, qkv, qkv, *([bias_tabs] * ATT_BLOCKS_PER_STEP))


def _conv_residual_norm_kernel(b_ref, c_ref, v_ref, cp_ref, vp_ref, cn_ref, vn_ref, wc_ref,
                               w_ref, h_ref, g_ref, o_ref, xt_ref, gate_scr, *, seq):
    _short_conv_kernel(b_ref, c_ref, v_ref, cp_ref, vp_ref, cn_ref, vn_ref, wc_ref, gate_scr, seq=seq)
    _matmul_residual_norm_kernel(gate_scr, w_ref, h_ref, g_ref, o_ref, xt_ref)


def conv_residual_norm(y, w_conv, w, h, gain, seq):
    t, d = h.shape
    tm = TOK_TILE
    halo = 16
    n_halo = t // halo

    def prev_map(off):
        return lambda i: (jnp.maximum(i * (tm // halo) - 1, 0), off)

    def next_map(off):
        return lambda i: (jnp.minimum((i + 1) * (tm // halo), n_halo - 1), off)

    return pl.pallas_call(
        functools.partial(_conv_residual_norm_kernel, seq=seq),
        grid=(t // tm,),
        in_specs=[pl.BlockSpec((tm, d), lambda i: (i, 0)),
                  pl.BlockSpec((tm, d), lambda i: (i, 1)),
                  pl.BlockSpec((tm, d), lambda i: (i, 2)),
                  pl.BlockSpec((halo, d), prev_map(1)),
                  pl.BlockSpec((halo, d), prev_map(2)),
                  pl.BlockSpec((halo, d), next_map(1)),
                  pl.BlockSpec((halo, d), next_map(2)),
                  pl.BlockSpec((3, d), lambda i: (0, 0)),
                  pl.BlockSpec((d, d), lambda i: (0, 0)),
                  pl.BlockSpec((tm, d), lambda i: (i, 0)),
                  pl.BlockSpec((1, d), lambda i: (0, 0))],
        out_specs=[pl.BlockSpec((tm, d), lambda i: (i, 0)),
                   pl.BlockSpec((d, tm), lambda i: (0, i))],
        out_shape=[jax.ShapeDtypeStruct((t, d), F32), jax.ShapeDtypeStruct((d, t), BF16)],
        scratch_shapes=[pltpu.VMEM((tm, d), BF16)],
        compiler_params=_cparams(("parallel",)),
        name="conv_residual_norm",
    )(y, y, y, y, y, y, y, w_conv, w, h, gain.reshape(1, d))


def _short_conv_kernel(b_ref, c_ref, v_ref, cp_ref, vp_ref, cn_ref, vn_ref, w_ref, o_ref, *, seq):
    i = pl.program_id(0)
    tm = b_ref.shape[0]
    halo = cp_ref.shape[0]
    u = c_ref[...].astype(F32) * v_ref[...].astype(F32)
    u_before = cp_ref[halo - 1:halo, :].astype(F32) * vp_ref[halo - 1:halo, :].astype(F32)
    u_after = cn_ref[0:1, :].astype(F32) * vn_ref[0:1, :].astype(F32)
    row = lax.broadcasted_iota(jnp.int32, u.shape, 0)
    pos = (i * tm + row) % seq
    u_prev = jnp.where(row == 0, u_before, pltpu.roll(u, 1, 0))
    u_prev = jnp.where(pos == 0, 0.0, u_prev)
    u_next = jnp.where(row == tm - 1, u_after, pltpu.roll(u, tm - 1, 0))
    u_next = jnp.where(pos == seq - 1, 0.0, u_next)
    conv = u_prev * w_ref[0:1, :] + u * w_ref[1:2, :] + u_next * w_ref[2:3, :]
    o_ref[...] = (b_ref[...].astype(F32) * conv).astype(o_ref.dtype)


def short_conv_gate(y, w_conv, seq):
    t = y.shape[0]
    d = y.shape[1] // 3
    cw = 512
    nj = d // cw
    tm = TOK_TILE
    halo = 16
    n_halo = t // halo

    def prev_map(off):
        return lambda i, j: (jnp.maximum(i * (tm // halo) - 1, 0), off * nj + j)

    def next_map(off):
        return lambda i, j: (jnp.minimum((i + 1) * (tm // halo), n_halo - 1), off * nj + j)

    return pl.pallas_call(
        functools.partial(_short_conv_kernel, seq=seq),
        grid=(t // tm, nj),
        in_specs=[pl.BlockSpec((tm, cw), lambda i, j: (i, j)),
                  pl.BlockSpec((tm, cw), lambda i, j: (i, nj + j)),
                  pl.BlockSpec((tm, cw), lambda i, j: (i, 2 * nj + j)),
                  pl.BlockSpec((halo, cw), prev_map(1)),
                  pl.BlockSpec((halo, cw), prev_map(2)),
                  pl.BlockSpec((halo, cw), next_map(1)),
                  pl.BlockSpec((halo, cw), next_map(2)),
                  pl.BlockSpec((3, cw), lambda i, j: (0, j))],
        out_specs=pl.BlockSpec((tm, cw), lambda i, j: (i, j)),
        out_shape=jax.ShapeDtypeStruct((t, d), BF16),
        compiler_params=_cparams(("parallel", "parallel")),
        name="short_conv_gate",
    )(y, y, y, y, y, y, y, w_conv)


def _top16(sc, want_rank):
    n, l = sc.shape
    iota_n = lax.broadcasted_iota(jnp.int32, (n, l), 0).astype(F32)
    iota_k = lax.broadcasted_iota(jnp.int32, (PEER_TOPK, l), 0).astype(F32)

    def body(r, carry):
        sc, rank, sv, si = carry
        rf = jnp.asarray(r, jnp.int32).astype(F32)
        m = jnp.max(sc, axis=0, keepdims=True)
        idx = jnp.min(jnp.where(sc == m, iota_n, float(n)), axis=0, keepdims=True)
        sel = iota_n == idx
        if want_rank:
            rank = jnp.where(sel, rf, rank)
        sc = jnp.where(sel, -jnp.inf, sc)
        sv = jnp.where(iota_k == rf, m, sv)
        si = jnp.where(iota_k == rf, idx, si)
        return sc, rank, sv, si

    init = (sc, jnp.full((n, l), float(PEER_TOPK), F32),
            jnp.zeros((PEER_TOPK, l), F32), jnp.zeros((PEER_TOPK, l), F32))
    _, rank, sv, si = lax.fori_loop(0, PEER_TOPK, body, init)
    return sv, si, rank


def _merge_top16(sv1, sv2):
    k = sv1.shape[0]
    iota_k = lax.broadcasted_iota(jnp.int32, sv1.shape, 0).astype(F32)
    top = sv1[0:1] + sv2[0:1]

    def body(_, carry):
        front, cnt, z = carry
        m = jnp.max(front, axis=0, keepdims=True)
        a_star = jnp.min(jnp.where(front == m, iota_k, float(k)), axis=0, keepdims=True)
        sel = iota_k == a_star
        z = z + jnp.exp(m - top)
        cnt = jnp.where(sel, cnt + 1.0, cnt)
        c_sel = jnp.sum(jnp.where(sel, cnt, 0.0), axis=0, keepdims=True)
        nxt = jnp.sum(jnp.where(iota_k == c_sel, sv2, 0.0), axis=0, keepdims=True)
        nxt = jnp.where(c_sel >= float(k), -jnp.inf, nxt)
        front = jnp.where(sel, sv1 + nxt, front)
        return front, cnt, z

    init = (sv1 + sv2[0:1], jnp.zeros(sv1.shape, F32), jnp.zeros(top.shape, F32))
    _, cnt, z = lax.fori_loop(0, PEER_TOPK, body, init)
    return cnt, z


def _tree(op, xs):
    while len(xs) > 1:
        xs = [op(xs[k], xs[k + 1]) for k in range(0, len(xs) - 1, 2)] + (xs[-1:] if len(xs) % 2 else [])
    return xs[0]


def _all_sublanes(op, x):
    for shift in (4, 2, 1):
        x = op(x, pltpu.roll(x, shift, 0))
    return x


def _count_greater(top, x):
    sel = jnp.where
    c8 = top[7] > x
    c4 = sel(c8, top[11], top[3]) > x
    c2 = sel(c8, sel(c4, top[13], top[9]), sel(c4, top[5], top[1])) > x
    c1 = sel(c8, sel(c4, sel(c2, top[14], top[12]), sel(c2, top[10], top[8])),
             sel(c4, sel(c2, top[6], top[4]), sel(c2, top[2], top[0]))) > x
    c16 = top[15] > x
    zero = jnp.zeros_like(x)
    return ((sel(c8, 8.0, zero) + sel(c4, 4.0, zero)) + (sel(c2, 2.0, zero) + sel(c1, 1.0, zero))
            + sel(c16, 1.0, zero))


def _sorting_network(n):
    def merge(lo, hi, r):
        step = r * 2
        if step < hi - lo:
            yield from merge(lo, hi, step)
            yield from merge(lo + r, hi, step)
            yield from [(k, k + r) for k in range(lo + r, hi - r, step)]
        else:
            yield (lo, lo + r)

    def sort(lo, hi):
        if hi - lo >= 1:
            mid = lo + (hi - lo) // 2
            yield from sort(lo, mid)
            yield from sort(mid + 1, hi)
            yield from merge(lo, hi, 1)

    return tuple(sort(0, n - 1))


_SORT16 = _sorting_network(PEER_TOPK)


def _compare_exchange(xs, i, j):
    xs[i], xs[j] = jnp.maximum(xs[i], xs[j]), jnp.minimum(xs[i], xs[j])


def _top16_sorted(rows):
    xs = list(rows)
    for i, j in _SORT16:
        _compare_exchange(xs, i, j)
    k = len(xs)
    for shift in (4, 2, 1):
        ys = [pltpu.roll(x, shift, 0) for x in xs]
        xs = [jnp.maximum(xs[v], ys[k - 1 - v]) for v in range(k)]
        stride = k // 2
        while stride:
            for v in range(k):
                if not v & stride:
                    _compare_exchange(xs, v, v + stride)
            stride //= 2
    return xs


def _peer_scores(h, keys_ref, q_scr, s_scr):
    for p in range(2):
        r0 = pl.multiple_of((2 * h + p) * PEER_HALF_DIM, PEER_HALF_DIM)
        s_scr[h, p] = jnp.dot(keys_ref[2 * h + p], q_scr[pl.ds(r0, PEER_HALF_DIM), :],
                              preferred_element_type=F32)


def _peer_head_exact(h, lt, s_scr, r2_ref, e2_ref, n_ref, e1_ref):
    n_keys, lanes = PEER_N_KEYS, 128
    iota_n = lax.broadcasted_iota(jnp.int32, (n_keys, lanes), 0).astype(F32)
    ls = slice(lt * lanes, (lt + 1) * lanes)
    s1 = s_scr[h, 0, :, ls]
    s2 = s_scr[h, 1, :, ls]
    sv1, si1, _ = _top16(s1, want_rank=False)
    sv2, _, rank2 = _top16(s2, want_rank=True)
    cnt, z = _merge_top16(sv1, sv2)
    n_sel = jnp.zeros((n_keys, lanes), F32)
    for a in range(PEER_TOPK):
        n_sel = jnp.where(iota_n == si1[a:a + 1, :], cnt[a:a + 1, :], n_sel)
    r2_ref[h, :, ls] = rank2.astype(r2_ref.dtype)
    e2_ref[h, :, ls] = jnp.exp(s2 - sv2[0:1, :]).astype(e2_ref.dtype)
    n_ref[h, :, ls] = n_sel
    e1_ref[h, :, ls] = jnp.exp(s1 - sv1[0:1, :]) * (GELU_HALF / z)


def _peer_head_top_values(h, s_scr, t_scr, sv_scr):
    n_keys, lanes, sub = PEER_N_KEYS, 128, 8

    head_row = lax.broadcasted_iota(jnp.int32, (sub, lanes), 0) == h

    def lane_tile(lt, _):
        ls = pl.ds(pl.multiple_of(lt * lanes, lanes), lanes)
        for p in range(2):
            rows = [s_scr[h, p, v * sub:(v + 1) * sub, ls] for v in range(n_keys // sub)]
            top = _top16_sorted(rows)
            for r in range(PEER_TOPK):
                t_scr[h, p, r, :, ls] = top[r]
                sv_scr[p, r, :, ls] = jnp.where(head_row, top[r], sv_scr[p, r, :, ls])
        return 0

    lax.fori_loop(0, s_scr.shape[3] // lanes, lane_tile, 0)


def _peer_head_distinct(h, s_scr, t_scr, cnt_scr, z_scr, r2_ref, e2_ref, n_ref, e1_ref):
    n_keys, lanes, sub = PEER_N_KEYS, 128, 8
    n_vr = n_keys // sub
    one = jnp.ones((sub, lanes), F32)
    zero = jnp.zeros((sub, lanes), F32)
    bad_tiles = []
    cnt_rows = [cnt_scr[a, pl.ds(h, 1), :] for a in range(PEER_TOPK)]
    inv_row = GELU_HALF / z_scr[pl.ds(h, 1), :]
    for lt in range(s_scr.shape[3] // lanes):
        ls = slice(lt * lanes, (lt + 1) * lanes)
        bad = zero
        for p in (1, 0):
            top = [t_scr[h, p, b, :, ls] for b in range(PEER_TOPK)]
            tied = _tree(jnp.maximum, [jnp.where(top[b] > top[b + 1], zero, one)
                                       for b in range(PEER_TOPK - 1)])
            n_top = zero
            if p == 1:
                for v in range(n_vr):
                    vs = slice(v * sub, (v + 1) * sub)
                    x = s_scr[h, 1, vs, ls]
                    rank = _count_greater(top, x)
                    r2_ref[h, vs, ls] = rank.astype(r2_ref.dtype)
                    e2_ref[h, vs, ls] = jnp.exp(x - top[0]).astype(e2_ref.dtype)
                    n_top = n_top + jnp.where(rank < float(PEER_TOPK), one, zero)
            else:
                cnt_b = [jnp.broadcast_to(r[:, ls], (sub, lanes)) for r in cnt_rows]
                inv_b = jnp.broadcast_to(inv_row[:, ls], (sub, lanes))
                for v in range(n_vr):
                    vs = slice(v * sub, (v + 1) * sub)
                    x = s_scr[h, 0, vs, ls]
                    n_sel = zero
                    for a in range(PEER_TOPK):
                        n_sel = jnp.where(x == top[a], cnt_b[a], n_sel)
                    n_ref[h, vs, ls] = n_sel
                    e1_ref[h, vs, ls] = jnp.exp(x - top[0]) * inv_b
                    n_top = n_top + jnp.where(x >= top[PEER_TOPK - 1], one, zero)
            miscount = jnp.where(_all_sublanes(jnp.add, n_top) != float(PEER_TOPK), one, zero)
            bad = jnp.maximum(bad, jnp.maximum(tied, miscount))
        bad_tiles.append(bad)
    return bad_tiles


def _peer_weights_kernel(xt_ref, wq_ref, keys_ref, r2_ref, e2_ref, n_ref, e1_ref,
                         q_scr, s_scr, t_scr, sv_scr, cnt_scr, z_scr):
    q_scr[...] = jnp.dot(wq_ref[...], xt_ref[...], preferred_element_type=F32).astype(BF16)
    sv_scr[...] = jnp.zeros_like(sv_scr)

    def top_values(h, _):
        _peer_scores(h, keys_ref, q_scr, s_scr)
        _peer_head_top_values(h, s_scr, t_scr, sv_scr)
        return 0

    lax.fori_loop(0, PEER_HEADS, top_values, 0)
    cnt, z = _merge_top16(sv_scr[0], sv_scr[1])
    cnt_scr[...] = cnt
    z_scr[...] = z[0]

    def weights(h, _):
        bad_tiles = _peer_head_distinct(h, s_scr, t_scr, cnt_scr, z_scr,
                                        r2_ref, e2_ref, n_ref, e1_ref)

        @pl.when(jnp.max(_tree(jnp.maximum, bad_tiles)) > 0.0)
        def _():
            for lt, bad in enumerate(bad_tiles):
                @pl.when(jnp.max(bad) > 0.0)
                def _():
                    _peer_head_exact(h, lt, s_scr, r2_ref, e2_ref, n_ref, e1_ref)

        return 0

    lax.fori_loop(0, PEER_HEADS, weights, 0)


def peer_weights(xt, wq_t, keys):
    d, t = xt.shape
    tm = PEER_PRE_TILE
    out16 = jax.ShapeDtypeStruct((PEER_HEADS, PEER_N_KEYS, t), BF16)
    out32 = jax.ShapeDtypeStruct((PEER_HEADS, PEER_N_KEYS, t), F32)
    ospec = pl.BlockSpec((PEER_HEADS, PEER_N_KEYS, tm), lambda i: (0, 0, i))
    return pl.pallas_call(
        _peer_weights_kernel,
        grid=(t // tm,),
        in_specs=[pl.BlockSpec((d, tm), lambda i: (0, i)),
                  pl.BlockSpec(wq_t.shape, lambda i: (0, 0)),
                  pl.BlockSpec(keys.shape, lambda i: (0, 0, 0))],
        out_specs=[ospec, ospec, ospec, ospec],
        out_shape=[out16, out16, out32, out32],
        scratch_shapes=[pltpu.VMEM((wq_t.shape[0], tm), BF16),
                        pltpu.VMEM((PEER_HEADS, 2, PEER_N_KEYS, tm), F32),
                        pltpu.VMEM((PEER_HEADS, 2, PEER_TOPK, 8, tm), F32),
                        pltpu.VMEM((2, PEER_TOPK, PEER_HEADS, tm), F32),
                        pltpu.VMEM((PEER_TOPK, PEER_HEADS, tm), F32),
                        pltpu.VMEM((PEER_HEADS, tm), F32)],
        compiler_params=_cparams(("parallel",)),
        name="peer_weights",
    )(xt, wq_t, keys)


def _peer_hidden_rows(i, il, a_ref, hid_ref, r2_ref, e2_ref, n_ref, e1_ref):
    nk, lanes, pack = PEER_N_KEYS, 128, 16
    row0 = il * nk
    n_rows = [n_ref[h, pl.ds(i, 1), :] for h in range(PEER_HEADS)]
    e1_rows = [e1_ref[h, pl.ds(i, 1), :] for h in range(PEER_HEADS)]
    zero = jnp.zeros((pack, lanes), BF16)
    for lt in range(a_ref.shape[1] // lanes):
        ls = slice(lt * lanes, (lt + 1) * lanes)
        n_b = [jnp.broadcast_to(r[:, ls], (pack, lanes)).astype(BF16) for r in n_rows]
        e1_b = [jnp.broadcast_to(r[:, ls], (pack, lanes)).astype(BF16) for r in e1_rows]
        for jb in range(nk // pack):
            js = slice(jb * pack, (jb + 1) * pack)
            g = None
            for h in range(PEER_HEADS):
                term = jnp.where(r2_ref[h, js, ls] < n_b[h], e2_ref[h, js, ls] * e1_b[h], zero)
                g = term if g is None else g + term
            rs = slice(row0 + jb * pack, row0 + (jb + 1) * pack)
            a = a_ref[rs, ls]
            act = a * (1.0 + lax.erf(a * math.sqrt(0.5)))
            hid_ref[rs, ls] = act.astype(BF16) * g


def _peer_dense_kernel(xt_ref, r2_ref, e2_ref, n_ref, e1_ref, wup_ref, wdn_ref, h_ref, g_ref, o_ref,
                       acc_ref, a0_ref, a1_ref, hid0_ref, hid1_ref, r2_scr, e2_scr, *, out_norm):
    e = pl.program_id(1)
    grp = a0_ref.shape[0]
    rows_per_grp = grp // PEER_N_KEYS
    n_grp = wup_ref.shape[0] // grp

    @pl.when(e == 0)
    def _():
        acc_ref[...] = jnp.zeros_like(acc_ref)
        r2_scr[...] = r2_ref[...]
        e2_scr[...] = e2_ref[...]

    a_refs = (a0_ref, a1_ref)
    hid_refs = (hid0_ref, hid1_ref)
    for p in range(n_grp + 2):
        if p < n_grp:
            a_refs[p % 2][...] = jnp.dot(wup_ref[p * grp:(p + 1) * grp, :], xt_ref[...],
                                         preferred_element_type=F32)
        if 1 <= p <= n_grp:
            g = p - 1
            for il in range(rows_per_grp):
                _peer_hidden_rows((e * n_grp + g) * rows_per_grp + il, il, a_refs[g % 2],
                                  hid_refs[g % 2], r2_scr, e2_scr, n_ref, e1_ref)
        if p >= 2:
            g = p - 2
            acc_ref[...] += lax.dot_general(hid_refs[g % 2][...], wdn_ref[g * grp:(g + 1) * grp, :],
                                            (((0,), (0,)), ((), ())), preferred_element_type=F32)

    @pl.when(e == pl.num_programs(1) - 1)
    def _():
        h_new = h_ref[...] + acc_ref[...]
        o_ref[...] = _rms(h_new, g_ref[...]) if out_norm else h_new


def peer_dense(xt, r2, e2, n_sel, e1n, w_up, w_down, layer, h, out_gain, out_norm):
    d, t = xt.shape
    n_exp = w_up.shape[1]
    tm, te = PEER_TOK_TILE, PEER_EXP_TILE
    wspec = pl.BlockSpec((PEER_HEADS, PEER_N_KEYS, tm), lambda i, e: (0, 0, i))
    return pl.pallas_call(
        functools.partial(_peer_dense_kernel, out_norm=out_norm),
        grid=(t // tm, n_exp // te),
        in_specs=[pl.BlockSpec((d, tm), lambda i, e: (0, i)),
                  wspec, wspec, wspec, wspec,
                  pl.BlockSpec((None, te, d), lambda i, e: (layer, e, 0)),
                  pl.BlockSpec((None, te, d), lambda i, e: (layer, e, 0)),
                  pl.BlockSpec((tm, d), lambda i, e: (i, 0)),
                  pl.BlockSpec((1, d), lambda i, e: (0, 0))],
        out_specs=pl.BlockSpec((tm, d), lambda i, e: (i, 0)),
        out_shape=jax.ShapeDtypeStruct((t, d), F32),
        scratch_shapes=[pltpu.VMEM((tm, d), F32),
                        pltpu.VMEM((PEER_STAGE_ROWS, tm), F32),
                        pltpu.VMEM((PEER_STAGE_ROWS, tm), F32),
                        pltpu.VMEM((PEER_STAGE_ROWS, tm), BF16),
                        pltpu.VMEM((PEER_STAGE_ROWS, tm), BF16),
                        pltpu.VMEM((PEER_HEADS, PEER_N_KEYS, tm), BF16),
                        pltpu.VMEM((PEER_HEADS, PEER_N_KEYS, tm), BF16)],
        compiler_params=_cparams(("parallel", "arbitrary")),
        name="peer_dense",
    )(xt, r2, e2, n_sel, e1n, w_up, w_down, h, out_gain.reshape(1, d))


def kernel(x, norm_mix, norm_ffn, norm_final, attn_w_qkv, attn_w_o, attn_rel_bias,
           conv_w_in, conv_w_conv, conv_w_out, peer_w_query, peer_sub_keys,
           peer_w_up, peer_w_down):
    batch, seq, d = x.shape
    depth = norm_mix.shape[0]
    rows = seq // GRID_W
    h = x.reshape(batch * seq, d)
    w_up_all = peer_w_up.astype(BF16)
    w_down_all = peer_w_down.astype(BF16)
    for i in range(depth):
        j = i // 2
        if i % 2 == 0:
            qkv = norm_matmul(h, norm_mix[i], attn_w_qkv[j].astype(BF16), BF16)
            bias_tabs = _attention_bias_tables(attn_rel_bias[j], rows)
            mixed = neighbourhood_attention_core(qkv, bias_tabs, batch, seq)
            h, xt = matmul_residual_norm(mixed, attn_w_o[j].astype(BF16), h, norm_ffn[i])
        else:
            y = norm_matmul(h, norm_mix[i], conv_w_in[j].astype(BF16), BF16)
            h, xt = conv_residual_norm(y, conv_w_conv[j], conv_w_out[j].astype(BF16), h,
                                       norm_ffn[i], seq)
        keys = peer_sub_keys[i].reshape(PEER_HEADS * 2, PEER_N_KEYS, PEER_HALF_DIM).astype(BF16)
        r2, e2, n_sel, e1n = peer_weights(xt, peer_w_query[i].T.astype(BF16), keys)
        h = peer_dense(xt, r2, e2, n_sel, e1n, w_up_all, w_down_all, i, h, norm_final,
                       out_norm=(i == depth - 1))
    return h.reshape(batch, seq, d)
```
